```python
import jax, jax.numpy as jnp
from jax import lax
import numpy as np

D_MODEL = 2048
BATCH = 2
SEQ = 4096
DEPTH = 1
DEC_BATCH = 128
DEC_SEQ = 1
PAST_LEN = 2048
PAGE_SIZE = 128

HEAD_DIM = 128
SB_HEADS = 6
SB_W = SB_HEADS * HEAD_DIM
SB_BLOCK = 128
SB_BIAS_INIT = -5.0
LRU_HEADS = 6
LRU_BLOCK = 128
LRU_W = LRU_HEADS * LRU_BLOCK
LRU_C = 8.0
CONV_W = 4
MEM_HEADS = 4
MEM_W = MEM_HEADS * HEAD_DIM
N_MEM = 256
N_BRANCH = 3
N_EXPERTS = 32
TOP_K = 4
D_FF = D_MODEL
SWIGLU_LIMIT = 7.0
SWIGLU_ALPHA = 1.702
MOE_BLOCK = 128
EPS = 1e-6

kernel_name = "hybrid_stickbreak_rglru_memxattn_moe_step"


def rms_norm(x, g):
    xf = x.astype(jnp.float32)
    y = xf * lax.rsqrt(jnp.mean(xf * xf, axis=-1, keepdims=True) + EPS)
    return (y * g.astype(jnp.float32)).astype(x.dtype)


def stick_breaking(q, k, v, q_pos, k_pos, sb_bias):
    z = jnp.einsum('bqhd,bkhd->bhqk', q, k, preferred_element_type=jnp.float32) * (HEAD_DIM ** -0.5)
    z = z + sb_bias.astype(jnp.float32)[None, :, None, None]
    causal = k_pos[None, :] < q_pos[:, None]
    log_keep = jnp.where(causal, jax.nn.log_sigmoid(-z), 0.0)
    log_w = z + lax.cumsum(log_keep, axis=3, reverse=True)
    w = jnp.exp(jnp.where(causal, log_w, -jnp.inf))
    return jnp.einsum('bhqk,bkhd->bqhd', w.astype(v.dtype), v)


def stick_breaking_sweep(q, k, v, past_len, sb_bias):
    b, t, h, d = q.shape
    k_pos = jnp.arange(k.shape[1])
    if t % SB_BLOCK != 0:
        return stick_breaking(q, k, v, past_len + jnp.arange(t), k_pos, sb_bias)
    nb = t // SB_BLOCK
    qb = q.reshape(b, nb, SB_BLOCK, h, d).transpose(1, 0, 2, 3, 4)

    def one_block(args):
        q_blk, i = args
        q_pos = past_len + i * SB_BLOCK + jnp.arange(SB_BLOCK)
        return stick_breaking(q_blk, k, v, q_pos, k_pos, sb_bias)

    ob = lax.map(one_block, (qb, jnp.arange(nb)))
    return ob.transpose(1, 0, 2, 3, 4).reshape(b, t, h, d)


def causal_conv(x, prev, conv_w, conv_b):
    t = x.shape[1]
    xp = jnp.concatenate([prev.astype(x.dtype), x], axis=1)
    y = conv_b + conv_w[0] * xp[:, 0:t]
    for j in range(1, CONV_W):
        y = y + conv_w[j] * xp[:, j:j + t]
    return y, xp[:, t:]


def _lin_combine(c1, c2):
    a1, b1 = c1
    a2, b2 = c2
    return a1 * a2, a2 * b1 + b2


def rg_lru(x, h0, lru_wa, lru_ba, lru_wx, lru_bx, lru_lambda):
    b, t, w = x.shape
    xh = x.reshape(b, t, LRU_HEADS, LRU_BLOCK)
    r = jax.nn.sigmoid(jnp.einsum('bthi,hij->bthj', xh, lru_wa) + lru_ba).reshape(b, t, w)
    i = jax.nn.sigmoid(jnp.einsum('bthi,hij->bthj', xh, lru_wx) + lru_bx).reshape(b, t, w)
    log_a = -LRU_C * r.astype(jnp.float32) * jax.nn.softplus(-lru_lambda.astype(jnp.float32))
    a = jnp.exp(log_a)
    u = jnp.sqrt(-jnp.expm1(2.0 * log_a)) * (i * x).astype(jnp.float32)
    u = u.at[:, 0].add(a[:, 0] * h0.astype(jnp.float32))
    _, h = lax.associative_scan(_lin_combine, (a, u), axis=1)
    return h.astype(x.dtype), h[:, -1].astype(h0.dtype)


def moe_ffn(h, w_router, b_router, w_up, b_up, w_down, b_down):
    n = h.shape[0]
    n_assign = n * TOP_K
    n_blocks = -(-n_assign // MOE_BLOCK) + N_EXPERTS
    n_rows = n_blocks * MOE_BLOCK
    logits = jnp.einsum('nd,de->ne', h, w_router, preferred_element_type=jnp.float32) + b_router.astype(jnp.float32)
    top_vals, top_idx = lax.top_k(logits, TOP_K)
    gates = jax.nn.softmax(top_vals, axis=-1)
    e_flat = top_idx.reshape(-1)
    g_flat = gates.reshape(-1)
    tok_flat = jnp.arange(n_assign) // TOP_K
    order = jnp.argsort(e_flat)
    e_sorted = e_flat[order]
    sizes = jnp.bincount(e_flat, length=N_EXPERTS)
    padded = (sizes + MOE_BLOCK - 1) // MOE_BLOCK * MOE_BLOCK
    start = jnp.cumsum(sizes) - sizes
    p_end = jnp.cumsum(padded)
    p_start = p_end - padded
    dest = p_start[e_sorted] + (jnp.arange(n_assign) - start[e_sorted])
    rows_tok = jnp.full((n_rows,), n, jnp.int32).at[dest].set(tok_flat[order].astype(jnp.int32))
    rows_gate = jnp.zeros((n_rows,), jnp.float32).at[dest].set(g_flat[order])
    block_expert = jnp.clip(jnp.searchsorted(p_end, jnp.arange(n_blocks) * MOE_BLOCK, side='right'), 0, N_EXPERTS - 1)
    h_pad = jnp.concatenate([h, jnp.zeros((1, h.shape[1]), h.dtype)], axis=0)
    xb = h_pad[rows_tok].reshape(n_blocks, MOE_BLOCK, h.shape[1])

    def expert_block(args):
        x_blk, e = args
        gu = x_blk @ w_up[e] + b_up[e]
        gate, up = gu[:, :D_FF], gu[:, D_FF:]
        gate = jnp.minimum(gate, SWIGLU_LIMIT)
        up = jnp.clip(up, -SWIGLU_LIMIT, SWIGLU_LIMIT)
        act = gate * jax.nn.sigmoid(SWIGLU_ALPHA * gate) * (up + 1.0)
        return act @ w_down[e] + b_down[e]

    yb = lax.map(expert_block, (xb, block_expert)).reshape(n_rows, h.shape[1])
    out = jnp.zeros((n + 1, h.shape[1]), jnp.float32).at[rows_tok].add(yb.astype(jnp.float32) * rows_gate[:, None])
    return out[:n].astype(h.dtype)


def layer(x, k_past, v_past, mem_k, mem_v, conv_prev, h_prev, past_len,
          norm_mix_g, w_in, q_sb_g, k_sb_g, sb_bias, q_mem_g, conv_w, conv_b, lru_wa, lru_ba, lru_wx, lru_bx, lru_lambda,
          p_attn, p_lru, p_mem, w_o, norm_ffn_g, w_router, b_router, w_up, b_up, w_down, b_down):
    b, t, _ = x.shape
    xn = rms_norm(x, norm_mix_g)
    proj = jnp.einsum('btd,dn->btn', xn, w_in)
    cuts = [SB_W, 2 * SB_W, 3 * SB_W, 3 * SB_W + LRU_W, 3 * SB_W + 2 * LRU_W, 3 * SB_W + 2 * LRU_W + MEM_W]
    q_sb, k_sb, v_sb, x_lru, gate_lru, q_mem, gate_logits = jnp.split(proj, cuts, axis=-1)
    q_sb = rms_norm(q_sb.reshape(b, t, SB_HEADS, HEAD_DIM), q_sb_g)
    k_sb = rms_norm(k_sb.reshape(b, t, SB_HEADS, HEAD_DIM), k_sb_g)
    v_sb = v_sb.reshape(b, t, SB_HEADS, HEAD_DIM)
    if k_past is None:
        k_all, v_all = k_sb, v_sb
    else:
        k_all = jnp.concatenate([k_past.astype(k_sb.dtype), k_sb], axis=1)
        v_all = jnp.concatenate([v_past.astype(v_sb.dtype), v_sb], axis=1)
    o_sb = stick_breaking_sweep(q_sb, k_all, v_all, past_len, sb_bias).reshape(b, t, SB_W)
    xc, conv_new = causal_conv(x_lru, conv_prev, conv_w, conv_b)
    o_lru, h_new = rg_lru(xc, h_prev, lru_wa, lru_ba, lru_wx, lru_bx, lru_lambda)
    o_lru = o_lru * jax.nn.gelu(gate_lru)
    q_m = rms_norm(q_mem.reshape(b, t, MEM_HEADS, HEAD_DIM), q_mem_g)
    s = jnp.einsum('bthd,bmhd->bhtm', q_m, mem_k.astype(q_m.dtype), preferred_element_type=jnp.float32) * (HEAD_DIM ** -0.5)
    p = jax.nn.softmax(s, axis=-1).astype(x.dtype)
    o_mem = jnp.einsum('bhtm,bmhd->bthd', p, mem_v.astype(x.dtype)).reshape(b, t, MEM_W)
    g_a, g_l, g_m = jnp.split(jax.nn.sigmoid(gate_logits.astype(jnp.float32)), N_BRANCH, axis=-1)
    merged = (g_a * (o_sb @ p_attn) + g_l * (o_lru @ p_lru) + g_m * (o_mem @ p_mem)).astype(x.dtype)
    hres = x + merged @ w_o
    hn = rms_norm(hres, norm_ffn_g)
    y = hres + moe_ffn(hn.reshape(b * t, D_MODEL), w_router, b_router, w_up, b_up, w_down, b_down).reshape(b, t, D_MODEL)
    return y, k_sb, v_sb, conv_new, h_new


def setup_inputs(seed: int = 0) -> dict:
    key = jax.random.key(seed)
    ks = jax.random.split(key, 37)
    f32 = jnp.float32

    def nrm(k, shape, scale=1.0):
        return jax.random.normal(k, shape, f32) * scale

    n_pages = PAST_LEN // PAGE_SIZE
    n_used = DEC_BATCH * n_pages
    n_pool = n_used + max(1, n_used // 4)
    in_width = 3 * SB_W + 2 * LRU_W + MEM_W + N_BRANCH * D_MODEL
    page_table = jax.random.permutation(ks[5], n_pool)[:n_used].reshape(DEC_BATCH, n_pages).astype(jnp.int32)
    u = jax.random.uniform(ks[24], (LRU_W,), f32, minval=0.9, maxval=0.999)
    lru_lambda = jnp.log(u) - jnp.log1p(-u)
    return {
        'x_prompt': nrm(ks[0], (BATCH, SEQ, D_MODEL)),
        'x_sample': nrm(ks[1], (DEC_BATCH, DEC_SEQ, D_MODEL)),
        'mem_prompt': nrm(ks[2], (BATCH, N_MEM, D_MODEL)),
        'cache_sb_k': nrm(ks[3], (n_pool, PAGE_SIZE, SB_HEADS, HEAD_DIM)),
        'cache_sb_v': nrm(ks[4], (n_pool, PAGE_SIZE, SB_HEADS, HEAD_DIM)),
        'page_table': page_table,
        'cache_mem_k': nrm(ks[6], (DEC_BATCH, N_MEM, MEM_HEADS, HEAD_DIM)),
        'cache_mem_v': nrm(ks[7], (DEC_BATCH, N_MEM, MEM_HEADS, HEAD_DIM)),
        'state_conv': nrm(ks[8], (DEC_BATCH, CONV_W - 1, LRU_W)),
        'state_lru': nrm(ks[9], (DEC_BATCH, LRU_W), 0.5),
        'norm_mix_g': 1.0 + nrm(ks[10], (D_MODEL,), 0.02),
        'norm_mem_g': 1.0 + nrm(ks[11], (D_MODEL,), 0.02),
        'w_in': nrm(ks[12], (D_MODEL, in_width), D_MODEL ** -0.5),
        'q_sb_g': 1.0 + nrm(ks[13], (HEAD_DIM,), 0.02),
        'k_sb_g': 1.0 + nrm(ks[14], (HEAD_DIM,), 0.02),
        'sb_bias': SB_BIAS_INIT + nrm(ks[36], (SB_HEADS,), 0.1),
        'q_mem_g': 1.0 + nrm(ks[15], (HEAD_DIM,), 0.02),
        'k_mem_g': 1.0 + nrm(ks[16], (HEAD_DIM,), 0.02),
        'w_mem_kv': nrm(ks[17], (D_MODEL, 2 * MEM_W), D_MODEL ** -0.5),
        'conv_w': nrm(ks[18], (CONV_W, LRU_W), CONV_W ** -0.5),
        'conv_b': nrm(ks[19], (LRU_W,), 0.01),
        'lru_wa': nrm(ks[20], (LRU_HEADS, LRU_BLOCK, LRU_BLOCK), LRU_BLOCK ** -0.5),
        'lru_ba': nrm(ks[21], (LRU_HEADS, LRU_BLOCK), 0.01),
        'lru_wx': nrm(ks[22], (LRU_HEADS, LRU_BLOCK, LRU_BLOCK), LRU_BLOCK ** -0.5),
        'lru_bx': nrm(ks[23], (LRU_HEADS, LRU_BLOCK), 0.01),
        'lru_lambda': lru_lambda,
        'p_attn': nrm(ks[25], (SB_W, D_MODEL), SB_W ** -0.5),
        'p_lru': nrm(ks[26], (LRU_W, D_MODEL), LRU_W ** -0.5),
        'p_mem': nrm(ks[27], (MEM_W, D_MODEL), MEM_W ** -0.5),
        'w_o': nrm(ks[28], (D_MODEL, D_MODEL), D_MODEL ** -0.5),
        'norm_ffn_g': 1.0 + nrm(ks[29], (D_MODEL,), 0.02),
        'w_router': nrm(ks[30], (D_MODEL, N_EXPERTS), D_MODEL ** -0.5),
        'b_router': nrm(ks[31], (N_EXPERTS,), 0.01),
        'w_up': nrm(ks[32], (N_EXPERTS, D_MODEL, 2 * D_FF), D_MODEL ** -0.5),
        'b_up': nrm(ks[33], (N_EXPERTS, 2 * D_FF), 0.01),
        'w_down': nrm(ks[34], (N_EXPERTS, D_FF, D_MODEL), D_FF ** -0.5),
        'b_down': nrm(ks[35], (N_EXPERTS, D_MODEL), 0.01),
    }


def reference(x_prompt, x_sample, mem_prompt, cache_sb_k, cache_sb_v, page_table, cache_mem_k, cache_mem_v,
              state_conv, state_lru, norm_mix_g, norm_mem_g, w_in, q_sb_g, k_sb_g, sb_bias, q_mem_g, k_mem_g, w_mem_kv,
              conv_w, conv_b, lru_wa, lru_ba, lru_wx, lru_bx, lru_lambda, p_attn, p_lru, p_mem, w_o, norm_ffn_g,
              w_router, b_router, w_up, b_up, w_down, b_down):
    bp = x_prompt.shape[0]
    bs, n_pages = page_table.shape
    past_len = n_pages * cache_sb_k.shape[1]
    mn = rms_norm(mem_prompt, norm_mem_g)
    mk, mv = jnp.split(jnp.einsum('bmd,dn->bmn', mn, w_mem_kv), 2, axis=-1)
    mem_k_prompt = rms_norm(mk.reshape(bp, N_MEM, MEM_HEADS, HEAD_DIM), k_mem_g)
    mem_v_prompt = mv.reshape(bp, N_MEM, MEM_HEADS, HEAD_DIM)
    k_past = cache_sb_k[page_table].reshape(bs, past_len, SB_HEADS, HEAD_DIM)
    v_past = cache_sb_v[page_table].reshape(bs, past_len, SB_HEADS, HEAD_DIM)
    conv0 = jnp.zeros((bp, CONV_W - 1, LRU_W), x_prompt.dtype)
    lru0 = jnp.zeros((bp, LRU_W), state_lru.dtype)
    y_prompt, y_sample = x_prompt, x_sample
    for _ in range(DEPTH):
        y_prompt, sb_k_prompt, sb_v_prompt, conv_prompt, lru_prompt = layer(
            y_prompt, None, None, mem_k_prompt, mem_v_prompt, conv0, lru0, 0,
            norm_mix_g, w_in, q_sb_g, k_sb_g, sb_bias, q_mem_g, conv_w, conv_b, lru_wa, lru_ba, lru_wx, lru_bx, lru_lambda,
            p_attn, p_lru, p_mem, w_o, norm_ffn_g, w_router, b_router, w_up, b_up, w_down, b_down)
        y_sample, sb_k_sample, sb_v_sample, conv_sample, lru_sample = layer(
            y_sample, k_past, v_past, cache_mem_k, cache_mem_v, state_conv, state_lru, past_len,
            norm_mix_g, w_in, q_sb_g, k_sb_g, sb_bias, q_mem_g, conv_w, conv_b, lru_wa, lru_ba, lru_wx, lru_bx, lru_lambda,
            p_attn, p_lru, p_mem, w_o, norm_ffn_g, w_router, b_router, w_up, b_up, w_down, b_down)
    return (y_prompt, y_sample, sb_k_prompt, sb_v_prompt, mem_k_prompt, mem_v_prompt, conv_prompt, lru_prompt,
            sb_k_sample, sb_v_sample, conv_sample, lru_sample)
```

```python
import functools

import jax
import jax.numpy as jnp
from jax import lax
from jax.experimental import pallas as pl
from jax.experimental.pallas import tpu as pltpu

F32 = jnp.float32
BF16 = jnp.bfloat16

HEAD_DIM = 128
SB_HEADS = 6
SB_W = SB_HEADS * HEAD_DIM
LRU_HEADS = 6
LRU_W = LRU_HEADS * HEAD_DIM
LRU_C = 8.0
CONV_W = 4
MEM_HEADS = 4
MEM_W = MEM_HEADS * HEAD_DIM
N_BRANCH = 3
N_EXPERTS = 32
TOP_K = 4
SWIGLU_LIMIT = 7.0
SWIGLU_ALPHA = 1.702
EPS = 1e-6
QK_SCALE = HEAD_DIM ** -0.5

COL_Q, COL_K, COL_V = 0, SB_W, 2 * SB_W
COL_XL, COL_GL = 3 * SB_W, 3 * SB_W + LRU_W
COL_QM = 3 * SB_W + 2 * LRU_W
COL_GATE = COL_QM + MEM_W

MIB = 1024 * 1024
PROJ_TN = 256
SB_BLK = 256
LRU_TT = 256
MOE_SUB = 256
MOE_CHUNK_SUBS = 6
MOE_TF = 256


def _cparams(sem, vmem_mib):
    return pltpu.CompilerParams(dimension_semantics=sem, vmem_limit_bytes=vmem_mib * MIB)


def _softplus(z):
    return jnp.maximum(z, 0.0) + jnp.log1p(jnp.exp(-jnp.abs(z)))


def _sigmoid(z):
    return 1.0 / (1.0 + jnp.exp(-z))


def _split_bf16(x):
    hi = x.astype(BF16)
    lo = (x - hi.astype(F32)).astype(BF16)
    return hi, lo


def _dot(a, b):
    return jnp.dot(a, b, preferred_element_type=F32)


def _dot_nt(a, b):
    return lax.dot_general(a, b, (((1,), (1,)), ((), ())), preferred_element_type=F32)


def _rmsnorm_kernel(x_ref, g_ref, o_ref):
    x = x_ref[...]
    ms = jnp.mean(x * x, axis=-1, keepdims=True)
    o_ref[...] = (x * lax.rsqrt(ms + EPS) * g_ref[...]).astype(o_ref.dtype)


def _rmsnorm(x, g, tm):
    r, d = x.shape
    assert r % tm == 0
    return pl.pallas_call(
        _rmsnorm_kernel,
        out_shape=jax.ShapeDtypeStruct((r, d), BF16),
        grid=(r // tm,),
        in_specs=[pl.BlockSpec((tm, d), lambda i: (i, 0)), pl.BlockSpec((1, d), lambda i: (0, 0))],
        out_specs=pl.BlockSpec((tm, d), lambda i: (i, 0)),
        compiler_params=_cparams(("parallel",), 32),
        name="rmsnorm",
    )(x, g.reshape(1, d))


def _proj_kernel(xn_ref, w_ref, gain_ref, o_ref, *, norm_lo, norm_hi, heads_out):
    j = pl.program_id(1)
    y = _dot(xn_ref[...], w_ref[...].astype(BF16))
    nh = y.shape[1] // HEAD_DIM

    def store(c, val):
        if heads_out:
            o_ref[c] = val
        else:
            o_ref[:, c * HEAD_DIM:(c + 1) * HEAD_DIM] = val

    def plain():
        for c in range(nh):
            store(c, y[:, c * HEAD_DIM:(c + 1) * HEAD_DIM])

    def normed():
        gain = gain_ref[...]
        for c in range(nh):
            yc = y[:, c * HEAD_DIM:(c + 1) * HEAD_DIM]
            ms = jnp.mean(yc * yc, axis=-1, keepdims=True)
            store(c, yc * lax.rsqrt(ms + EPS) * gain[:, c * HEAD_DIM:(c + 1) * HEAD_DIM])

    if norm_hi <= norm_lo:
        plain()
    else:
        is_norm = (j >= norm_lo) & (j < norm_hi)
        pl.when(is_norm)(normed)
        pl.when(jnp.logical_not(is_norm))(plain)


def _proj(xn, w, gain, *, col0, ncols, norm_cols=(0, 0), tm, heads_out=None):
    r, d = xn.shape
    tn = PROJ_TN
    assert r % tm == 0 and col0 % tn == 0 and ncols % tn == 0
    assert norm_cols[0] % tn == 0 and norm_cols[1] % tn == 0
    col_tile0 = col0 // tn
    if gain is None:
        gain = jnp.ones((ncols,), F32)
    if heads_out is None:
        out_shape = jax.ShapeDtypeStruct((r, ncols), F32)
        out_spec = pl.BlockSpec((tm, tn), lambda i, j: (i, j))
    else:
        bsz, t = heads_out
        assert bsz * t == r and t % tm == 0
        nt = t // tm
        hpt = tn // HEAD_DIM
        out_shape = jax.ShapeDtypeStruct((bsz, ncols // HEAD_DIM, t, HEAD_DIM), F32)
        out_spec = pl.BlockSpec((None, hpt, tm, HEAD_DIM), lambda i, j: (i // nt, j, i % nt, 0))
    kern = functools.partial(_proj_kernel, norm_lo=norm_cols[0] // tn, norm_hi=norm_cols[1] // tn,
                             heads_out=heads_out is not None)
    return pl.pallas_call(
        kern,
        out_shape=out_shape,
        grid=(r // tm, ncols // tn),
        in_specs=[
            pl.BlockSpec((tm, d), lambda i, j: (i, 0)),
            pl.BlockSpec((d, tn), lambda i, j: (0, j + col_tile0)),
            pl.BlockSpec((1, tn), lambda i, j: (0, j)),
        ],
        out_specs=out_spec,
        compiler_params=_cparams(("parallel", "parallel"), 40),
        name="proj",
    )(xn, w, gain.reshape(1, ncols))


def _sb_prompt_kernel(bias_ref, q_ref, k_ref, v_ref, o_ref, kb_ref, vb_ref):
    h = pl.program_id(1)
    qi = pl.program_id(2)
    blk = q_ref.shape[0]

    @pl.when(qi == 0)
    def _():
        kb_ref[...] = k_ref[...].astype(BF16)
        vb_ref[...] = v_ref[...].astype(BF16)

    q = q_ref[...].astype(BF16)
    bias = bias_ref[h]
    row = lax.broadcasted_iota(jnp.int32, (blk, blk), 0)
    col = lax.broadcasted_iota(jnp.int32, (blk, blk), 1)
    tri = jnp.where(row >= col, 1.0, 0.0).astype(BF16)
    causal = col < row

    def block(kb, carry, acc, masked):
        start = pl.multiple_of(kb * blk, blk)
        k = kb_ref[pl.ds(start, blk), :]
        v = vb_ref[pl.ds(start, blk), :]
        z = _dot_nt(q, k) * QK_SCALE + bias
        log_keep = -_softplus(z)
        if masked:
            log_keep = jnp.where(causal, log_keep, 0.0)
        hi, lo = _split_bf16(log_keep)
        suffix = _dot(hi, tri) + _dot(lo, tri)
        w = jnp.exp(z + suffix + carry)
        if masked:
            w = jnp.where(causal, w, 0.0)
        acc = acc + _dot(w.astype(BF16), v)
        carry = carry + suffix[:, 0:1]
        return carry, acc

    carry0 = jnp.zeros((blk, 1), F32)
    acc0 = jnp.zeros((blk, HEAD_DIM), F32)
    carry, acc = block(qi, carry0, acc0, True)

    def body(it, c):
        return block(qi - 1 - it, c[0], c[1], False)

    carry, acc = lax.fori_loop(0, qi, body, (carry, acc))
    o_ref[...] = acc.astype(o_ref.dtype)


def _sb_prompt(q, k, v, sb_bias):
    bsz, _, t, _ = k.shape
    blk = SB_BLK
    assert t % blk == 0
    nq = t // blk
    kv_spec = pl.BlockSpec((None, None, t, HEAD_DIM), lambda b, h, i: (b, h, 0, 0))
    return pl.pallas_call(
        _sb_prompt_kernel,
        out_shape=jax.ShapeDtypeStruct((bsz * t, SB_W), BF16),
        grid=(bsz, SB_HEADS, nq),
        in_specs=[
            pl.BlockSpec(memory_space=pltpu.SMEM),
            pl.BlockSpec((blk, HEAD_DIM), lambda b, h, i: (b * nq + i, h)),
            kv_spec, kv_spec,
        ],
        out_specs=pl.BlockSpec((blk, HEAD_DIM), lambda b, h, i: (b * nq + i, h)),
        scratch_shapes=[pltpu.VMEM((t, HEAD_DIM), BF16), pltpu.VMEM((t, HEAD_DIM), BF16)],
        compiler_params=_cparams(("parallel", "parallel", "arbitrary"), 32),
        name="sb_prompt",
    )(sb_bias, q, k, v)


def _sb_sample_kernel(pt_ref, bias_ref, q_ref, *refs, n_pages):
    del pt_ref
    k_refs = refs[:n_pages]
    v_refs = refs[n_pages:2 * n_pages]
    o_ref = refs[2 * n_pages]
    page = k_refs[0].shape[0]
    rep = 8
    n = n_pages * rep

    q8 = jnp.broadcast_to(q_ref[0], (rep, HEAD_DIM)).astype(BF16)
    z = jnp.concatenate([_dot_nt(q8, k_refs[p][...].astype(BF16)) for p in range(n_pages)], axis=0)
    z = z * QK_SCALE + bias_ref[pl.program_id(1)]
    log_keep = -_softplus(z)
    row = lax.broadcasted_iota(jnp.int32, (page, page), 0)
    col = lax.broadcasted_iota(jnp.int32, (page, page), 1)
    tri = jnp.where(row >= col, 1.0, 0.0).astype(BF16)
    hi, lo = _split_bf16(log_keep)
    suffix = _dot(hi, tri) + _dot(lo, tri)
    rn = lax.broadcasted_iota(jnp.int32, (n, n), 0)
    cn = lax.broadcasted_iota(jnp.int32, (n, n), 1)
    later = jnp.where(((rn % rep) == (cn % rep)) & (cn // rep > rn // rep), 1.0, 0.0).astype(BF16)
    shi, slo = _split_bf16(suffix)
    carry = (_dot(later, shi) + _dot(later, slo))[:, 0:1]
    w = jnp.exp(z + suffix + carry).astype(BF16)

    acc = jnp.zeros((rep, HEAD_DIM), F32)
    for p in range(n_pages):
        acc = acc + _dot(w[p * rep:(p + 1) * rep, :], v_refs[p][...].astype(BF16))
    o_ref[0] = acc[0:1, :].astype(o_ref.dtype)


def _sb_sample(q, sb_bias, cache_k, cache_v, page_table):
    bs, n_pages = page_table.shape
    page = cache_k.shape[2]
    assert cache_k.shape[1] == SB_HEADS and cache_k.shape[3] == HEAD_DIM

    def page_spec(p):
        return pl.BlockSpec((None, None, page, HEAD_DIM), lambda b, h, pt: (pt[b, p], h, 0, 0))

    grid_spec = pltpu.PrefetchScalarGridSpec(
        num_scalar_prefetch=1,
        grid=(bs, SB_HEADS),
        in_specs=[pl.BlockSpec(memory_space=pltpu.SMEM),
                  pl.BlockSpec((1, 1, HEAD_DIM), lambda b, h, pt: (b * SB_HEADS + h, 0, 0))]
                 + [page_spec(p) for p in range(n_pages)] * 2,
        out_specs=pl.BlockSpec((1, 1, HEAD_DIM), lambda b, h, pt: (b * SB_HEADS + h, 0, 0)),
    )
    out = pl.pallas_call(
        functools.partial(_sb_sample_kernel, n_pages=n_pages),
        out_shape=jax.ShapeDtypeStruct((bs * SB_HEADS, 1, HEAD_DIM), BF16),
        grid_spec=grid_spec,
        compiler_params=_cparams(("arbitrary", "arbitrary"), 32),
        name="sb_sample",
    )(page_table, sb_bias, q.reshape(bs * SB_HEADS, 1, HEAD_DIM), *([cache_k] * n_pages), *([cache_v] * n_pages))
    return out.reshape(bs, SB_W)


def _lru_gates(xc, wa_ref, ba, wx_ref, bx, nsp_lambda):
    xb = xc.astype(BF16)
    r_parts, i_parts = [], []
    for h in range(LRU_HEADS):
        xh = xb[:, h * HEAD_DIM:(h + 1) * HEAD_DIM]
        r_parts.append(_dot(xh, wa_ref[h].astype(BF16)))
        i_parts.append(_dot(xh, wx_ref[h].astype(BF16)))
    r = _sigmoid(jnp.concatenate(r_parts, axis=1) + ba)
    i = _sigmoid(jnp.concatenate(i_parts, axis=1) + bx)
    log_a = -LRU_C * r * nsp_lambda
    a = jnp.exp(log_a)
    u = jnp.sqrt(1.0 - jnp.exp(2.0 * log_a)) * (i * xc)
    return a, u


def _gelu_tanh(x):
    return 0.5 * x * (1.0 + jnp.tanh(0.7978845608028654 * (x + 0.044715 * (x * x * x))))


def _lru_prompt_kernel(x_ref, gate_ref, prev_ref, h0_ref, cw_ref, cb_ref, wa_ref, ba_ref, wx_ref, bx_ref,
                       lam_ref, o_ref, hlast_ref, xp_ref, h_ref):
    ti = pl.program_id(1)
    tt = x_ref.shape[0]
    pad = 8

    @pl.when(ti == 0)
    def _():
        xp_ref[pad - (CONV_W - 1):pad, :] = prev_ref[0]
        h_ref[...] = h0_ref[0]

    x = x_ref[...]
    xp_ref[pad:pad + tt, :] = x
    cw = cw_ref[...]
    xc = cb_ref[...] + cw[CONV_W - 1:CONV_W, :] * x
    for j in range(CONV_W - 1):
        xc = xc + cw[j:j + 1, :] * xp_ref[pad - (CONV_W - 1) + j:pad - (CONV_W - 1) + j + tt, :]
    xp_ref[pad - (CONV_W - 1):pad, :] = x[tt - (CONV_W - 1):tt, :]

    nsp = _softplus(-lam_ref[...])
    a, u = _lru_gates(xc, wa_ref, ba_ref[...], wx_ref, bx_ref[...], nsp)
    rows = lax.broadcasted_iota(jnp.int32, (tt, LRU_W), 0)
    b = jnp.where(rows == 0, u + a * h_ref[...], u)
    s = 1
    while s < tt:
        keep = rows >= s
        b = jnp.where(keep, a * pltpu.roll(b, s, 0) + b, b)
        if 2 * s < tt:
            a = jnp.where(keep, a * pltpu.roll(a, s, 0), a)
        s *= 2
    h_ref[...] = b[tt - 1:tt, :]
    o_ref[...] = (b * _gelu_tanh(gate_ref[...])).astype(o_ref.dtype)

    @pl.when(ti == pl.num_programs(1) - 1)
    def _():
        hlast_ref[0] = b[tt - 1:tt, :]


def _lru_prompt(proj, conv_prev, h0, conv_w, conv_b, lru_wa, lru_ba, lru_wx, lru_bx, lru_lambda, bsz, t):
    tt = LRU_TT
    assert t % tt == 0
    nt = t // tt
    xl, gl = 0, 1
    full = lambda shape: pl.BlockSpec(shape, lambda b, i: (0,) * len(shape))
    o_lru, h_last = pl.pallas_call(
        _lru_prompt_kernel,
        out_shape=(jax.ShapeDtypeStruct((bsz * t, LRU_W), BF16),
                   jax.ShapeDtypeStruct((bsz, 1, LRU_W), F32)),
        grid=(bsz, nt),
        in_specs=[
            pl.BlockSpec((tt, LRU_W), lambda b, i: (b * nt + i, xl)),
            pl.BlockSpec((tt, LRU_W), lambda b, i: (b * nt + i, gl)),
            pl.BlockSpec((1, CONV_W - 1, LRU_W), lambda b, i: (b, 0, 0)),
            pl.BlockSpec((1, 1, LRU_W), lambda b, i: (b, 0, 0)),
            full((CONV_W, LRU_W)), full((1, LRU_W)),
            full((LRU_HEADS, HEAD_DIM, HEAD_DIM)), full((1, LRU_W)),
            full((LRU_HEADS, HEAD_DIM, HEAD_DIM)), full((1, LRU_W)),
            full((1, LRU_W)),
        ],
        out_specs=(pl.BlockSpec((tt, LRU_W), lambda b, i: (b * nt + i, 0)),
                   pl.BlockSpec((1, 1, LRU_W), lambda b, i: (b, 0, 0))),
        scratch_shapes=[pltpu.VMEM((8 + tt, LRU_W), F32), pltpu.VMEM((1, LRU_W), F32)],
        compiler_params=_cparams(("parallel", "arbitrary"), 32),
        name="lru_prompt",
    )(proj, proj, conv_prev, h0.reshape(bsz, 1, LRU_W), conv_w, conv_b.reshape(1, LRU_W),
      lru_wa, lru_ba.reshape(1, LRU_W), lru_wx, lru_bx.reshape(1, LRU_W), lru_lambda.reshape(1, LRU_W))
    return o_lru, h_last.reshape(bsz, LRU_W)


def _lru_step_kernel(x_ref, gate_ref, p0_ref, p1_ref, p2_ref, h0_ref, cw_ref, cb_ref, wa_ref, ba_ref, wx_ref,
                     bx_ref, lam_ref, o_ref, h_ref):
    cw = cw_ref[...]
    xc = (cb_ref[...] + cw[0:1, :] * p0_ref[...] + cw[1:2, :] * p1_ref[...] + cw[2:3, :] * p2_ref[...]
          + cw[3:4, :] * x_ref[...])
    nsp = _softplus(-lam_ref[...])
    a, u = _lru_gates(xc, wa_ref, ba_ref[...], wx_ref, bx_ref[...], nsp)
    h = u + a * h0_ref[...]
    h_ref[...] = h
    o_ref[...] = (h * _gelu_tanh(gate_ref[...])).astype(o_ref.dtype)


def _lru_step(x_lru, gate_lru, state_conv, h0, conv_w, conv_b, lru_wa, lru_ba, lru_wx, lru_bx, lru_lambda):
    bs = x_lru.shape[0]
    assert CONV_W == 4
    prevs = [state_conv[:, j, :] for j in range(CONV_W - 1)]
    o_lru, h_new = pl.pallas_call(
        _lru_step_kernel,
        out_shape=(jax.ShapeDtypeStruct((bs, LRU_W), BF16), jax.ShapeDtypeStruct((bs, LRU_W), F32)),
        name="lru_step",
    )(x_lru, gate_lru, *prevs, h0, conv_w, conv_b.reshape(1, LRU_W), lru_wa, lru_ba.reshape(1, LRU_W),
      lru_wx, lru_bx.reshape(1, LRU_W), lru_lambda.reshape(1, LRU_W))
    return o_lru, h_new


def _mem_prompt_kernel(q_ref, k_ref, v_ref, o_ref):
    s = _dot_nt(q_ref[...].astype(BF16), k_ref[...].astype(BF16)) * QK_SCALE
    m = jnp.max(s, axis=-1, keepdims=True)
    e = jnp.exp(s - m)
    p = e / jnp.sum(e, axis=-1, keepdims=True)
    o_ref[...] = _dot(p.astype(BF16), v_ref[...].astype(BF16)).astype(o_ref.dtype)


def _mem_prompt(proj, qcol0, mem_k, mem_v, t):
    bsz, _, n_mem, _ = mem_k.shape
    tq = min(t, 512)
    assert t % tq == 0 and qcol0 % HEAD_DIM == 0
    nt = t // tq
    qc = qcol0 // HEAD_DIM
    kv_spec = pl.BlockSpec((None, None, n_mem, HEAD_DIM), lambda b, h, i: (b, h, 0, 0))
    return pl.pallas_call(
        _mem_prompt_kernel,
        out_shape=jax.ShapeDtypeStruct((bsz * t, MEM_W), BF16),
        grid=(bsz, MEM_HEADS, nt),
        in_specs=[pl.BlockSpec((tq, HEAD_DIM), lambda b, h, i: (b * nt + i, qc + h)), kv_spec, kv_spec],
        out_specs=pl.BlockSpec((tq, HEAD_DIM), lambda b, h, i: (b * nt + i, h)),
        compiler_params=_cparams(("parallel", "parallel", "parallel"), 32),
        name="mem_prompt",
    )(proj, mem_k, mem_v)


def _mem_sample_kernel(q_ref, k_ref, v_ref, o_ref):
    group, heads = q_ref.shape[:2]
    rep = 8
    for g in range(group):
        q = q_ref[g]
        outs = []
        for h in range(heads):
            q8 = jnp.broadcast_to(q[h:h + 1, :], (rep, HEAD_DIM)).astype(BF16)
            s = _dot_nt(q8, k_ref[g, h].astype(BF16)) * QK_SCALE
            m = jnp.max(s, axis=-1, keepdims=True)
            e = jnp.exp(s - m)
            p = e / jnp.sum(e, axis=-1, keepdims=True)
            outs.append(_dot(p.astype(BF16), v_ref[g, h].astype(BF16))[0:1, :])
        o_ref[g] = jnp.concatenate(outs, axis=0).astype(o_ref.dtype)


def _mem_sample(q, cache_k, cache_v):
    bs, heads, n_mem, _ = cache_k.shape
    assert heads == MEM_HEADS and cache_k.shape[3] == HEAD_DIM
    group = 8
    assert bs % group == 0
    kv_spec = pl.BlockSpec((group, heads, n_mem, HEAD_DIM), lambda i: (i, 0, 0, 0))
    out = pl.pallas_call(
        _mem_sample_kernel,
        out_shape=jax.ShapeDtypeStruct((bs, heads, HEAD_DIM), BF16),
        grid=(bs // group,),
        in_specs=[pl.BlockSpec((group, heads, HEAD_DIM), lambda i: (i, 0, 0)), kv_spec, kv_spec],
        out_specs=pl.BlockSpec((group, heads, HEAD_DIM), lambda i: (i, 0, 0)),
        compiler_params=_cparams(("parallel",), 32),
        name="mem_sample",
    )(q.reshape(bs, heads, HEAD_DIM), cache_k, cache_v)
    return out.reshape(bs, MEM_W)


def _merge_kernel(oa_ref, ol_ref, om_ref, pa_ref, pl_ref, pm_ref, ga_ref, gl_ref, gm_ref, o_ref):
    ya = _dot(oa_ref[...], pa_ref[...].astype(BF16))
    yl = _dot(ol_ref[...], pl_ref[...].astype(BF16))
    ym = _dot(om_ref[...], pm_ref[...].astype(BF16))
    merged = _sigmoid(ga_ref[...]) * ya + _sigmoid(gl_ref[...]) * yl + _sigmoid(gm_ref[...]) * ym
    o_ref[...] = merged.astype(o_ref.dtype)


def _merge(o_sb, o_lru, o_mem, p_attn, p_lru, p_mem, gates, tm):
    r = o_sb.shape[0]
    d = p_attn.shape[1]
    tn = 512
    assert r % tm == 0 and d % tn == 0
    nj = d // tn
    return pl.pallas_call(
        _merge_kernel,
        out_shape=jax.ShapeDtypeStruct((r, d), BF16),
        grid=(r // tm, nj),
        in_specs=[
            pl.BlockSpec((tm, SB_W), lambda i, j: (i, 0)),
            pl.BlockSpec((tm, LRU_W), lambda i, j: (i, 0)),
            pl.BlockSpec((tm, MEM_W), lambda i, j: (i, 0)),
            pl.BlockSpec((SB_W, tn), lambda i, j: (0, j)),
            pl.BlockSpec((LRU_W, tn), lambda i, j: (0, j)),
            pl.BlockSpec((MEM_W, tn), lambda i, j: (0, j)),
            pl.BlockSpec((tm, tn), lambda i, j: (i, j)),
            pl.BlockSpec((tm, tn), lambda i, j: (i, nj + j)),
            pl.BlockSpec((tm, tn), lambda i, j: (i, 2 * nj + j)),
        ],
        out_specs=pl.BlockSpec((tm, tn), lambda i, j: (i, j)),
        compiler_params=_cparams(("parallel", "parallel"), 44),
        name="merge",
    )(o_sb, o_lru, o_mem, p_attn, p_lru, p_mem, gates, gates, gates)


def _post_kernel(x_ref, m_ref, wo_ref, g_ref, wrh_ref, wrl_ref, br_ref, cnt0_ref,
                 hres_ref, hnp_ref, idx_ref, gate_ref, rank_ref, cnt_ref):
    i = pl.program_id(0)
    tm, d = x_ref.shape

    @pl.when(i == 0)
    def _():
        cnt_ref[...] = cnt0_ref[...]

    hres = x_ref[...] + _dot(m_ref[...], wo_ref[...])
    hres_ref[...] = hres
    ms = jnp.mean(hres * hres, axis=-1, keepdims=True)
    hn = hres * lax.rsqrt(ms + EPS) * g_ref[...]
    hb = hn.astype(BF16)
    wa = lax.bitcast_convert_type(hb[:, :d // 2].astype(F32), jnp.uint32)
    wb = lax.bitcast_convert_type(hb[:, d // 2:].astype(F32), jnp.uint32)
    hnp_ref[...] = wa | (wb >> 16)

    hn_lo = (hn - hb.astype(F32)).astype(BF16)
    logits = (_dot_nt(wrh_ref[...], hb) + _dot_nt(wrh_ref[...], hn_lo) + _dot_nt(wrl_ref[...], hb)
              + br_ref[...])
    ne = logits.shape[0]
    eid = lax.broadcasted_iota(jnp.int32, (ne, tm), 0)
    work = logits
    vals, idxs, onehots = [], [], []
    for _ in range(TOP_K):
        mx = jnp.max(work, axis=0, keepdims=True)
        sel = jnp.min(jnp.where(work == mx, eid, ne), axis=0, keepdims=True)
        oh = eid == sel
        vals.append(mx)
        idxs.append(sel)
        onehots.append(oh)
        work = jnp.where(oh, -jnp.inf, work)
    es = [jnp.exp(v - vals[0]) for v in vals]
    den = functools.reduce(jnp.add, es)
    gate_ref[...] = jnp.concatenate([e / den for e in es], axis=0)
    idx_ref[...] = jnp.concatenate(idxs, axis=0)

    chosen = functools.reduce(jnp.logical_or, onehots)
    r_ = lax.broadcasted_iota(jnp.int32, (tm, tm), 0)
    c_ = lax.broadcasted_iota(jnp.int32, (tm, tm), 1)
    before = jnp.where(r_ < c_, 1.0, 0.0).astype(BF16)
    chosen_f = jnp.where(chosen, 1.0, 0.0)
    prior = _dot(chosen_f.astype(BF16), before) + cnt_ref[...]
    rank_ref[...] = jnp.concatenate(
        [jnp.sum(jnp.where(oh, prior, 0.0), axis=0, keepdims=True) for oh in onehots], axis=0).astype(jnp.int32)
    cnt_ref[...] = cnt_ref[...] + jnp.sum(chosen_f, axis=1, keepdims=True)


def _post(x, merged, wo_b, g, wr_hi, wr_lo, b_router, cnt0, tm):
    r, d = x.shape
    ne = wr_hi.shape[0]
    assert r % tm == 0
    full = lambda shape: pl.BlockSpec(shape, lambda i: (0,) * len(shape))
    return pl.pallas_call(
        _post_kernel,
        out_shape=(jax.ShapeDtypeStruct((r, d), F32),
                   jax.ShapeDtypeStruct((r, d // 2), jnp.uint32),
                   jax.ShapeDtypeStruct((TOP_K, r), jnp.int32),
                   jax.ShapeDtypeStruct((TOP_K, r), F32),
                   jax.ShapeDtypeStruct((TOP_K, r), jnp.int32),
                   jax.ShapeDtypeStruct((ne, 1), F32)),
        grid=(r // tm,),
        in_specs=[pl.BlockSpec((tm, d), lambda i: (i, 0)),
                  pl.BlockSpec((tm, d), lambda i: (i, 0)),
                  full((d, d)), full((1, d)), full((ne, d)), full((ne, d)), full((ne, 1)), full((ne, 1))],
        out_specs=(pl.BlockSpec((tm, d), lambda i: (i, 0)),
                   pl.BlockSpec((tm, d // 2), lambda i: (i, 0)),
                   pl.BlockSpec((TOP_K, tm), lambda i: (0, i)),
                   pl.BlockSpec((TOP_K, tm), lambda i: (0, i)),
                   pl.BlockSpec((TOP_K, tm), lambda i: (0, i)),
                   full((ne, 1))),
        compiler_params=_cparams(("arbitrary",), 48),
        name="post",
    )(x, merged, wo_b, g.reshape(1, d), wr_hi, wr_lo, b_router.reshape(ne, 1), cnt0)


def _dispatch_kernel(dest_ref, hn_ref, xs_in_ref, xs_ref, sem):
    del xs_in_ref
    i = pl.program_id(0)
    tm = dest_ref.shape[2] // TOP_K

    def row_copy(t, k):
        return pltpu.make_async_copy(hn_ref.at[pl.ds(i * tm + t, 1)],
                                     xs_ref.at[pl.ds(dest_ref[0, 0, k * tm + t], 1)], sem)

    def start(t, c):
        for k in range(TOP_K):
            row_copy(t, k).start()
        return c

    def wait(t, c):
        for k in range(TOP_K):
            row_copy(t, k).wait()
        return c

    lax.fori_loop(0, tm, start, 0)
    lax.fori_loop(0, tm, wait, 0)


def _dispatch(hnp, dest, n_rows, tm):
    r, w = hnp.shape
    assert r % tm == 0
    nt = r // tm
    dest_t = dest.reshape(TOP_K, nt, tm).transpose(1, 0, 2).reshape(nt, 1, TOP_K * tm)
    xs0 = jnp.zeros((n_rows, w), hnp.dtype)
    return pl.pallas_call(
        _dispatch_kernel,
        out_shape=jax.ShapeDtypeStruct((n_rows, w), hnp.dtype),
        grid=(nt,),
        in_specs=[pl.BlockSpec((1, 1, TOP_K * tm), lambda i: (i, 0, 0), memory_space=pltpu.SMEM),
                  pl.BlockSpec(memory_space=pl.ANY),
                  pl.BlockSpec(memory_space=pl.ANY)],
        out_specs=pl.BlockSpec(memory_space=pl.ANY),
        scratch_shapes=[pltpu.SemaphoreType.DMA],
        input_output_aliases={2: 0},
        compiler_params=_cparams(("arbitrary",), 16),
        name="dispatch",
    )(dest_t, hnp, xs0)


def _moe_kernel(ce_ref, cb_ref, ns_ref, xs_ref, wg_ref, wu_ref, bg_ref, bu_ref, wd_ref, bd_ref, o_ref,
                xa_ref, xb_ref, act_ref, wgb_ref, wub_ref, wdb_ref, *, nf):
    del ce_ref, cb_ref
    c = pl.program_id(0)
    s = pl.program_id(1)
    nsub = ns_ref[c]
    sub = MOE_SUB
    half = xa_ref.shape[1]

    @pl.when((s == 0) & (nsub > 0))
    def _():
        def unpack(i, carry):
            rows = pl.ds(pl.multiple_of(i * sub, sub), sub)
            word = xs_ref[rows, :]
            xa_ref[rows, :] = lax.bitcast_convert_type(word & jnp.uint32(0xFFFF0000), F32).astype(BF16)
            xb_ref[rows, :] = lax.bitcast_convert_type(word << 16, F32).astype(BF16)
            return carry
        lax.fori_loop(0, nsub, unpack, 0)

    @pl.when((s < nf) & (nsub > 0))
    def _():
        wgb_ref[...] = wg_ref[...].astype(BF16)
        wub_ref[...] = wu_ref[...].astype(BF16)
        bg = bg_ref[...]
        bu = bu_ref[...]

        def up(i, carry):
            rows = pl.ds(pl.multiple_of(i * sub, sub), sub)
            xa = xa_ref[rows, :]
            xb = xb_ref[rows, :]
            gate = _dot(xa, wgb_ref[:half, :]) + _dot(xb, wgb_ref[half:, :]) + bg
            upv = _dot(xa, wub_ref[:half, :]) + _dot(xb, wub_ref[half:, :]) + bu
            gate = jnp.minimum(gate, SWIGLU_LIMIT)
            upv = jnp.clip(upv, -SWIGLU_LIMIT, SWIGLU_LIMIT)
            act = gate * _sigmoid(SWIGLU_ALPHA * gate) * (upv + 1.0)
            act_ref[s, rows, :] = act.astype(BF16)
            return carry
        lax.fori_loop(0, nsub, up, 0)

    @pl.when((s >= nf) & (nsub > 0))
    def _():
        wdb_ref[...] = wd_ref[...].astype(BF16)
        bd = bd_ref[...]
        tf = act_ref.shape[2]

        def down(i, carry):
            rows = pl.ds(pl.multiple_of(i * sub, sub), sub)
            y = _dot(act_ref[0, rows, :], wdb_ref[0:tf, :])
            for f in range(1, nf):
                y = y + _dot(act_ref[f, rows, :], wdb_ref[f * tf:(f + 1) * tf, :])
            o_ref[rows, :] = y + bd
            return carry
        lax.fori_loop(0, nsub, down, 0)


def _moe(xs, chunk_expert, chunk_block, chunk_nsub, w_up, b_up, w_down, b_down):
    ne, d, dff2 = w_up.shape
    dff = dff2 // 2
    rb = MOE_SUB * MOE_CHUNK_SUBS
    n_chunks = xs.shape[0] // rb
    tf = MOE_TF
    nf = dff // tf
    nn = d // tf
    assert dff % tf == 0 and d % tf == 0

    def up_f(s, ns_c):
        return jnp.where(ns_c > 0, jnp.minimum(s, nf - 1), nf - 1)

    def dn_n(s, ns_c):
        return jnp.where(ns_c > 0, jnp.maximum(s - nf, 0), nn - 1)

    grid_spec = pltpu.PrefetchScalarGridSpec(
        num_scalar_prefetch=3,
        grid=(n_chunks, nf + nn),
        in_specs=[
            pl.BlockSpec((rb, d // 2), lambda c, s, ce, cb, ns: (cb[c], 0)),
            pl.BlockSpec((None, d, tf), lambda c, s, ce, cb, ns: (ce[c], 0, up_f(s, ns[c]))),
            pl.BlockSpec((None, d, tf), lambda c, s, ce, cb, ns: (ce[c], 0, nf + up_f(s, ns[c]))),
            pl.BlockSpec((None, 1, tf), lambda c, s, ce, cb, ns: (ce[c], 0, up_f(s, ns[c]))),
            pl.BlockSpec((None, 1, tf), lambda c, s, ce, cb, ns: (ce[c], 0, nf + up_f(s, ns[c]))),
            pl.BlockSpec((None, dff, tf), lambda c, s, ce, cb, ns: (ce[c], 0, dn_n(s, ns[c]))),
            pl.BlockSpec((None, 1, tf), lambda c, s, ce, cb, ns: (ce[c], 0, dn_n(s, ns[c]))),
        ],
        out_specs=pl.BlockSpec((rb, tf), lambda c, s, ce, cb, ns: (cb[c], dn_n(s, ns[c]))),
        scratch_shapes=[pltpu.VMEM((rb, d // 2), BF16), pltpu.VMEM((rb, d // 2), BF16),
                        pltpu.VMEM((nf, rb, tf), BF16),
                        pltpu.VMEM((d, tf), BF16), pltpu.VMEM((d, tf), BF16), pltpu.VMEM((dff, tf), BF16)],
    )
    return pl.pallas_call(
        functools.partial(_moe_kernel, nf=nf),
        out_shape=jax.ShapeDtypeStruct((n_chunks * rb, d), F32),
        grid_spec=grid_spec,
        compiler_params=_cparams(("arbitrary", "arbitrary"), 52),
        name="moe",
    )(chunk_expert, chunk_block, chunk_nsub, xs, w_up, w_up, b_up.reshape(ne, 1, dff2), b_up.reshape(ne, 1, dff2),
      w_down, b_down.reshape(ne, 1, d))


def _combine_kernel(dest_ref, hres_ref, gate_ref, yb_ref, o_ref, buf_ref, sem):
    tm = hres_ref.shape[0]

    def row_copy(t, k):
        return pltpu.make_async_copy(yb_ref.at[pl.ds(dest_ref[0, 0, k * tm + t], 1)],
                                     buf_ref.at[k, pl.ds(t, 1)], sem)

    def start(t, c):
        for k in range(TOP_K):
            row_copy(t, k).start()
        return c

    def wait(t, c):
        for k in range(TOP_K):
            row_copy(t, k).wait()
        return c

    lax.fori_loop(0, tm, start, 0)
    lax.fori_loop(0, tm, wait, 0)
    g = gate_ref[...]
    y = hres_ref[...]
    for k in range(TOP_K):
        y = y + g[:, k:k + 1] * buf_ref[k]
    o_ref[...] = y


def _combine(hres, gates, dest, yb, tm):
    r, d = hres.shape
    assert r % tm == 0
    nt = r // tm
    dest_t = dest.reshape(TOP_K, nt, tm).transpose(1, 0, 2).reshape(nt, 1, TOP_K * tm)
    return pl.pallas_call(
        _combine_kernel,
        out_shape=jax.ShapeDtypeStruct((r, d), F32),
        grid=(nt,),
        in_specs=[pl.BlockSpec((1, 1, TOP_K * tm), lambda i: (i, 0, 0), memory_space=pltpu.SMEM),
                  pl.BlockSpec((tm, d), lambda i: (i, 0)),
                  pl.BlockSpec((tm, TOP_K), lambda i: (i, 0)),
                  pl.BlockSpec(memory_space=pl.ANY)],
        out_specs=pl.BlockSpec((tm, d), lambda i: (i, 0)),
        scratch_shapes=[pltpu.VMEM((TOP_K, tm, d), F32), pltpu.SemaphoreType.DMA],
        compiler_params=_cparams(("arbitrary",), 32),
        name="combine",
    )(dest_t, hres, gates.T, yb)


def _row_tile(r, cap):
    tm = min(r, cap)
    assert r % tm == 0
    return tm


def kernel(x_prompt, x_sample, mem_prompt, cache_sb_k, cache_sb_v, page_table, cache_mem_k, cache_mem_v, state_conv, state_lru, norm_mix_g, norm_mem_g, w_in, q_sb_g, k_sb_g, sb_bias, q_mem_g, k_mem_g, w_mem_kv, conv_w, conv_b, lru_wa, lru_ba, lru_wx, lru_bx, lru_lambda, p_attn, p_lru, p_mem, w_o, norm_ffn_g, w_router, b_router, w_up, b_up, w_down, b_down):
    bp, t, d = x_prompt.shape
    bs, ts, _ = x_sample.shape
    assert ts == 1, "sample group decodes one token per sequence"
    n_mem = mem_prompt.shape[1]
    q_gain = jnp.tile(q_sb_g, SB_HEADS)
    k_gain = jnp.tile(k_sb_g, SB_HEADS)
    qm_gain = jnp.tile(q_mem_g, MEM_HEADS)
    heads_major = lambda a: jnp.transpose(a, (0, 2, 1, 3))

    memn = _rmsnorm(mem_prompt.reshape(bp * n_mem, d), norm_mem_g, _row_tile(bp * n_mem, 256))
    mem_tm = _row_tile(n_mem, 256)
    mem_k_hm = _proj(memn, w_mem_kv, jnp.tile(k_mem_g, MEM_HEADS), col0=0, ncols=MEM_W, norm_cols=(0, MEM_W),
                     tm=mem_tm, heads_out=(bp, n_mem))
    mem_v_hm = _proj(memn, w_mem_kv, None, col0=MEM_W, ncols=MEM_W, tm=mem_tm, heads_out=(bp, n_mem))

    xp2 = x_prompt.reshape(bp * t, d)
    tm_p = _row_tile(t, 1024)
    xn_p = _rmsnorm(xp2, norm_mix_g, _row_tile(bp * t, 512))
    q_p = _proj(xn_p, w_in, q_gain, col0=COL_Q, ncols=SB_W, norm_cols=(0, SB_W), tm=tm_p)
    k_p = _proj(xn_p, w_in, k_gain, col0=COL_K, ncols=SB_W, norm_cols=(0, SB_W), tm=tm_p, heads_out=(bp, t))
    v_p = _proj(xn_p, w_in, None, col0=COL_V, ncols=SB_W, tm=tm_p, heads_out=(bp, t))
    rest_gain = jnp.concatenate([jnp.ones((2 * LRU_W,), F32), qm_gain])
    rest_p = _proj(xn_p, w_in, rest_gain, col0=COL_XL, ncols=COL_GATE - COL_XL,
                   norm_cols=(COL_QM - COL_XL, COL_GATE - COL_XL), tm=tm_p)
    gates_p = _proj(xn_p, w_in, None, col0=COL_GATE, ncols=N_BRANCH * d, tm=tm_p)
    o_sb_p = _sb_prompt(q_p, k_p, v_p, sb_bias)
    conv0 = jnp.zeros((bp, CONV_W - 1, LRU_W), x_prompt.dtype)
    lru0 = jnp.zeros((bp, LRU_W), state_lru.dtype)
    o_lru_p, lru_prompt = _lru_prompt(rest_p, conv0, lru0, conv_w, conv_b, lru_wa, lru_ba, lru_wx, lru_bx,
                                      lru_lambda, bp, t)
    o_mem_p = _mem_prompt(rest_p, COL_QM - COL_XL, mem_k_hm, mem_v_hm, t)
    xl_tail = rest_p.reshape(bp, t, -1)[:, t - (CONV_W - 1):, :LRU_W]
    conv_prompt = jnp.concatenate([conv0, xl_tail], axis=1)[:, -(CONV_W - 1):]

    xs2 = x_sample.reshape(bs, d)
    xn_s = _rmsnorm(xs2, norm_mix_g, bs)
    qk_s = _proj(xn_s, w_in, jnp.concatenate([q_gain, k_gain]), col0=COL_Q, ncols=2 * SB_W,
                 norm_cols=(0, 2 * SB_W), tm=bs)
    mid_s = _proj(xn_s, w_in, None, col0=COL_V, ncols=COL_QM - COL_V, tm=bs)
    qm_s = _proj(xn_s, w_in, qm_gain, col0=COL_QM, ncols=MEM_W, norm_cols=(0, MEM_W), tm=bs)
    gates_s = _proj(xn_s, w_in, None, col0=COL_GATE, ncols=N_BRANCH * d, tm=bs)
    o_sb_s = _sb_sample(qk_s[:, :SB_W], sb_bias, heads_major(cache_sb_k), heads_major(cache_sb_v), page_table)
    xl_s = mid_s[:, SB_W:SB_W + LRU_W]
    o_lru_s, lru_sample = _lru_step(xl_s, mid_s[:, SB_W + LRU_W:], state_conv, state_lru, conv_w, conv_b,
                                    lru_wa, lru_ba, lru_wx, lru_bx, lru_lambda)
    conv_sample = jnp.concatenate([state_conv, xl_s[:, None, :]], axis=1)[:, -(CONV_W - 1):]
    o_mem_s = _mem_sample(qm_s, heads_major(cache_mem_k), heads_major(cache_mem_v))

    wo_b = w_o.astype(BF16)
    wr_t = w_router.T
    wr_hi = wr_t.astype(BF16)
    wr_lo = (wr_t - wr_hi.astype(F32)).astype(BF16)
    merged_p = _merge(o_sb_p, o_lru_p, o_mem_p, p_attn, p_lru, p_mem, gates_p, _row_tile(bp * t, 1024))
    merged_s = _merge(o_sb_s, o_lru_s, o_mem_s, p_attn, p_lru, p_mem, gates_s, _row_tile(bs, 1024))
    cnt0 = jnp.zeros((N_EXPERTS, 1), F32)
    hres_p, hnp_p, idx_p, gate_p, rank_p, cnt_p = _post(xp2, merged_p, wo_b, norm_ffn_g, wr_hi, wr_lo, b_router,
                                                        cnt0, _row_tile(bp * t, 256))
    hres_s, hnp_s, idx_s, gate_s, rank_s, cnt_all = _post(xs2, merged_s, wo_b, norm_ffn_g, wr_hi, wr_lo, b_router,
                                                          cnt_p, _row_tile(bs, 256))

    rb = MOE_SUB * MOE_CHUNK_SUBS
    n_assign = (bp * t + bs) * TOP_K
    n_chunks = -(-n_assign // rb) + N_EXPERTS
    counts = cnt_all[:, 0].astype(jnp.int32)
    chunks_e = (counts + rb - 1) // rb
    chunk_end = jnp.cumsum(chunks_e)
    chunk_start = chunk_end - chunks_e
    used = chunk_end[-1]
    cidx = jnp.arange(n_chunks, dtype=jnp.int32)
    last = jnp.minimum(cidx, used - 1)
    chunk_expert = jnp.clip(jnp.searchsorted(chunk_end, last, side='right'), 0, N_EXPERTS - 1).astype(jnp.int32)
    rows_left = counts[chunk_expert] - (last - chunk_start[chunk_expert]) * rb
    chunk_nsub = jnp.where(cidx < used, (jnp.clip(rows_left, 0, rb) + MOE_SUB - 1) // MOE_SUB, 0).astype(jnp.int32)
    chunk_block = last.astype(jnp.int32)
    dest_p = chunk_start[idx_p] * rb + rank_p
    dest_s = chunk_start[idx_s] * rb + rank_s

    hnp = jnp.concatenate([hnp_p, hnp_s], axis=0)
    dest = jnp.concatenate([dest_p, dest_s], axis=1)
    xs = _dispatch(hnp, dest, n_chunks * rb, _row_tile(bp * t + bs, 128))
    yb = _moe(xs, chunk_expert, chunk_block, chunk_nsub, w_up, b_up, w_down, b_down)
    y_prompt = _combine(hres_p, gate_p, dest_p, yb, _row_tile(bp * t, 256)).reshape(bp, t, d)
    y_sample = _combine(hres_s, gate_s, dest_s, yb, _row_tile(bs, 256)).reshape(bs, 1, d)

    sb_k_sample = qk_s[:, SB_W:].reshape(bs, 1, SB_HEADS, HEAD_DIM)
    sb_v_sample = mid_s[:, :SB_W].reshape(bs, 1, SB_HEADS, HEAD_DIM)
    return (y_prompt, y_sample, heads_major(k_p), heads_major(v_p), heads_major(mem_k_hm), heads_major(mem_v_hm),
            conv_prompt, lru_prompt, sb_k_sample, sb_v_sample, conv_sample, lru_sample)
```

```python
import functools

import jax
import jax.numpy as jnp
from jax import lax
from jax.experimental import pallas as pl
from jax.experimental.pallas import tpu as pltpu

F32 = jnp.float32
BF16 = jnp.bfloat16

HEAD_DIM = 128
SB_HEADS = 6
SB_W = SB_HEADS * HEAD_DIM
LRU_HEADS = 6
LRU_W = LRU_HEADS * HEAD_DIM
LRU_C = 8.0
CONV_W = 4
MEM_HEADS = 4
MEM_W = MEM_HEADS * HEAD_DIM
N_BRANCH = 3
N_EXPERTS = 32
TOP_K = 4
SWIGLU_LIMIT = 7.0
SWIGLU_ALPHA = 1.702
EPS = 1e-6
QK_SCALE = HEAD_DIM ** -0.5

COL_Q, COL_K, COL_V = 0, SB_W, 2 * SB_W
COL_XL, COL_GL = 3 * SB_W, 3 * SB_W + LRU_W
COL_QM = 3 * SB_W + 2 * LRU_W
COL_GATE = COL_QM + MEM_W

MIB = 1024 * 1024
PROJ_TN = 256
SB_BLK = 256
SB_HEADS_PER_STEP = 2
LRU_TT = 256
MOE_SUB = 256
MOE_CHUNK_SUBS = 6
MOE_TF = 256


def _cparams(sem, vmem_mib):
    return pltpu.CompilerParams(dimension_semantics=sem, vmem_limit_bytes=vmem_mib * MIB)


def _softplus(z):
    return jnp.maximum(z, 0.0) + jnp.log1p(jnp.exp(-jnp.abs(z)))


def _softplus_fast(z):
    return jnp.maximum(z, 0.0) + jnp.log(1.0 + jnp.exp(-jnp.abs(z)))


def _sigmoid(z):
    return 1.0 / (1.0 + jnp.exp(-z))


def _split_bf16(x):
    hi = x.astype(BF16)
    lo = (x - hi.astype(F32)).astype(BF16)
    return hi, lo


def _dot(a, b):
    return jnp.dot(a, b, preferred_element_type=F32)


def _dot_nt(a, b):
    return lax.dot_general(a, b, (((1,), (1,)), ((), ())), preferred_element_type=F32)


def _rmsnorm_kernel(x_ref, g_ref, o_ref):
    x = x_ref[...]
    ms = jnp.mean(x * x, axis=-1, keepdims=True)
    o_ref[...] = (x * lax.rsqrt(ms + EPS) * g_ref[...]).astype(o_ref.dtype)


def _rmsnorm(x, g, tm):
    r, d = x.shape
    assert r % tm == 0
    return pl.pallas_call(
        _rmsnorm_kernel,
        out_shape=jax.ShapeDtypeStruct((r, d), BF16),
        grid=(r // tm,),
        in_specs=[pl.BlockSpec((tm, d), lambda i: (i, 0)), pl.BlockSpec((1, d), lambda i: (0, 0))],
        out_specs=pl.BlockSpec((tm, d), lambda i: (i, 0)),
        compiler_params=_cparams(("parallel",), 32),
        name="rmsnorm",
    )(x, g.reshape(1, d))


def _cast_kernel(x_ref, o_ref):
    o_ref[...] = x_ref[...].astype(o_ref.dtype)


def _cast_bf16(w, col0=0, ncols=None):
    r, c = w.shape
    ncols = c - col0 if ncols is None else ncols
    tc = PROJ_TN
    assert col0 % tc == 0 and ncols % tc == 0
    c0 = col0 // tc
    return pl.pallas_call(
        _cast_kernel,
        out_shape=jax.ShapeDtypeStruct((r, ncols), BF16),
        grid=(ncols // tc,),
        in_specs=[pl.BlockSpec((r, tc), lambda j: (0, j + c0))],
        out_specs=pl.BlockSpec((r, tc), lambda j: (0, j)),
        compiler_params=_cparams(("parallel",), 32),
        name="cast_bf16",
    )(w)


def _proj_kernel(xn_ref, w_ref, gain_ref, o_ref, *, patterns, heads_out):
    j = pl.program_id(1)
    y = _dot(xn_ref[...], w_ref[...].astype(BF16))
    nh = y.shape[1] // HEAD_DIM

    def store(c, val):
        if heads_out:
            o_ref[c] = val
        else:
            o_ref[:, c * HEAD_DIM:(c + 1) * HEAD_DIM] = val

    def emit(pattern):
        gain = gain_ref[...] if any(pattern) else None
        for c in range(nh):
            yc = y[:, c * HEAD_DIM:(c + 1) * HEAD_DIM]
            if pattern[c]:
                ms = jnp.mean(yc * yc, axis=-1, keepdims=True)
                yc = yc * lax.rsqrt(ms + EPS) * gain[:, c * HEAD_DIM:(c + 1) * HEAD_DIM]
            store(c, yc)

    distinct = sorted(set(patterns))
    if len(distinct) == 1:
        emit(distinct[0])
    else:
        for pattern in distinct:
            tiles = [jj for jj, p in enumerate(patterns) if p == pattern]
            pred = functools.reduce(jnp.logical_or, [j == jj for jj in tiles])
            pl.when(pred)(functools.partial(emit, pattern))


def _proj(xn, w, gain, *, col0=0, ncols, norm_cols=(0, 0), tm, tn, heads_out=None):
    r, d = xn.shape
    assert r % tm == 0 and col0 % tn == 0 and ncols % tn == 0 and tn % HEAD_DIM == 0
    assert norm_cols[0] % HEAD_DIM == 0 and norm_cols[1] % HEAD_DIM == 0
    col_tile0 = col0 // tn
    patterns = tuple(tuple(norm_cols[0] <= jj * tn + c * HEAD_DIM < norm_cols[1] for c in range(tn // HEAD_DIM))
                     for jj in range(ncols // tn))
    if gain is None:
        gain = jnp.ones((ncols,), F32)
    if heads_out is None:
        out_shape = jax.ShapeDtypeStruct((r, ncols), F32)
        out_spec = pl.BlockSpec((tm, tn), lambda i, j: (i, j))
    else:
        bsz, t = heads_out
        assert bsz * t == r and t % tm == 0
        nt = t // tm
        hpt = tn // HEAD_DIM
        out_shape = jax.ShapeDtypeStruct((bsz, ncols // HEAD_DIM, t, HEAD_DIM), F32)
        out_spec = pl.BlockSpec((None, hpt, tm, HEAD_DIM), lambda i, j: (i // nt, j, i % nt, 0))
    kern = functools.partial(_proj_kernel, patterns=patterns, heads_out=heads_out is not None)
    return pl.pallas_call(
        kern,
        out_shape=out_shape,
        grid=(r // tm, ncols // tn),
        in_specs=[
            pl.BlockSpec((tm, d), lambda i, j: (i, 0)),
            pl.BlockSpec((d, tn), lambda i, j: (0, j + col_tile0)),
            pl.BlockSpec((1, tn), lambda i, j: (0, j)),
        ],
        out_specs=out_spec,
        compiler_params=_cparams(("parallel", "parallel"), 40),
        name="proj",
    )(xn, w, gain.reshape(1, ncols))


def _sb_prompt_kernel(bias_ref, q_ref, k_ref, v_ref, o_ref, kb_ref, vb_ref):
    hg = pl.program_id(1)
    qi = pl.program_id(2)
    blk = q_ref.shape[0]
    nh = k_ref.shape[0]

    @pl.when(qi == 0)
    def _():
        kb_ref[...] = k_ref[...].astype(BF16)
        vb_ref[...] = v_ref[...].astype(BF16)

    qs = [q_ref[:, j * HEAD_DIM:(j + 1) * HEAD_DIM].astype(BF16) for j in range(nh)]
    biases = [bias_ref[hg * nh + j] for j in range(nh)]
    row = lax.broadcasted_iota(jnp.int32, (blk, blk), 0)
    col = lax.broadcasted_iota(jnp.int32, (blk, blk), 1)
    tri = jnp.where(row >= col, 1.0, 0.0).astype(BF16)
    causal = col < row

    def block(kb, state, masked):
        start = pl.multiple_of(kb * blk, blk)
        new = []
        for j in range(nh):
            carry, acc = state[j]
            k = kb_ref[j, pl.ds(start, blk), :]
            v = vb_ref[j, pl.ds(start, blk), :]
            z = _dot_nt(qs[j], k) * QK_SCALE + biases[j]
            log_keep = -_softplus_fast(z)
            if masked:
                log_keep = jnp.where(causal, log_keep, 0.0)
            hi, lo = _split_bf16(log_keep)
            suffix = _dot(hi, tri) + _dot(lo, tri)
            w = jnp.exp(z + suffix + carry)
            if masked:
                w = jnp.where(causal, w, 0.0)
            new.append((carry + suffix[:, 0:1], acc + _dot(w.astype(BF16), v)))
        return tuple(new)

    zero = (jnp.zeros((blk, 1), F32), jnp.zeros((blk, HEAD_DIM), F32))
    state = block(qi, (zero,) * nh, True)
    state = lax.fori_loop(0, qi, lambda it, st: block(qi - 1 - it, st, False), state)
    for j in range(nh):
        o_ref[:, j * HEAD_DIM:(j + 1) * HEAD_DIM] = state[j][1].astype(o_ref.dtype)


def _sb_prompt(q, k, v, sb_bias):
    bsz, heads, t, _ = k.shape
    blk = SB_BLK
    nh = SB_HEADS_PER_STEP
    assert t % blk == 0 and heads % nh == 0
    nq = t // blk
    kv_spec = pl.BlockSpec((None, nh, t, HEAD_DIM), lambda b, h, i: (b, h, 0, 0))
    return pl.pallas_call(
        _sb_prompt_kernel,
        out_shape=jax.ShapeDtypeStruct((bsz * t, SB_W), BF16),
        grid=(bsz, heads // nh, nq),
        in_specs=[
            pl.BlockSpec(memory_space=pltpu.SMEM),
            pl.BlockSpec((blk, nh * HEAD_DIM), lambda b, h, i: (b * nq + i, h)),
            kv_spec, kv_spec,
        ],
        out_specs=pl.BlockSpec((blk, nh * HEAD_DIM), lambda b, h, i: (b * nq + i, h)),
        scratch_shapes=[pltpu.VMEM((nh, t, HEAD_DIM), BF16), pltpu.VMEM((nh, t, HEAD_DIM), BF16)],
        compiler_params=_cparams(("parallel", "parallel", "arbitrary"), 40),
        name="sb_prompt",
    )(sb_bias, q, k, v)


def _sb_sample_kernel(pt_ref, bias_ref, q_ref, *refs, n_pages):
    del pt_ref
    k_refs = refs[:n_pages]
    v_refs = refs[n_pages:2 * n_pages]
    o_ref = refs[2 * n_pages]
    heads, page = k_refs[0].shape[:2]
    rep = 8
    n = n_pages * rep

    row = lax.broadcasted_iota(jnp.int32, (page, page), 0)
    col = lax.broadcasted_iota(jnp.int32, (page, page), 1)
    tri = jnp.where(row >= col, 1.0, 0.0).astype(BF16)
    rn = lax.broadcasted_iota(jnp.int32, (n, n), 0)
    cn = lax.broadcasted_iota(jnp.int32, (n, n), 1)
    later = jnp.where(((rn % rep) == (cn % rep)) & (cn // rep > rn // rep), 1.0, 0.0).astype(BF16)
    q = q_ref[0]
    outs = []
    for h in range(heads):
        q8 = jnp.broadcast_to(q[h:h + 1, :], (rep, HEAD_DIM)).astype(BF16)
        z = jnp.concatenate([_dot_nt(q8, k_refs[p][h].astype(BF16)) for p in range(n_pages)], axis=0)
        z = z * QK_SCALE + bias_ref[h]
        hi, lo = _split_bf16(-_softplus_fast(z))
        suffix = _dot(hi, tri) + _dot(lo, tri)
        shi, slo = _split_bf16(suffix)
        carry = (_dot(later, shi) + _dot(later, slo))[:, 0:1]
        w = jnp.exp(z + suffix + carry).astype(BF16)
        acc = jnp.zeros((rep, HEAD_DIM), F32)
        for p in range(n_pages):
            acc = acc + _dot(w[p * rep:(p + 1) * rep, :], v_refs[p][h].astype(BF16))
        outs.append(acc[0:1, :])
    o_ref[0] = jnp.concatenate(outs, axis=0).astype(o_ref.dtype)


def _sb_sample(q, sb_bias, cache_k, cache_v, page_table):
    bs, n_pages = page_table.shape
    heads, page = cache_k.shape[1:3]
    assert heads == SB_HEADS and cache_k.shape[3] == HEAD_DIM

    def page_spec(p):
        return pl.BlockSpec((None, heads, page, HEAD_DIM), lambda b, pt: (pt[b, p], 0, 0, 0))

    grid_spec = pltpu.PrefetchScalarGridSpec(
        num_scalar_prefetch=1,
        grid=(bs,),
        in_specs=[pl.BlockSpec(memory_space=pltpu.SMEM),
                  pl.BlockSpec((1, heads, HEAD_DIM), lambda b, pt: (b, 0, 0))]
                 + [page_spec(p) for p in range(n_pages)] * 2,
        out_specs=pl.BlockSpec((1, heads, HEAD_DIM), lambda b, pt: (b, 0, 0)),
    )
    out = pl.pallas_call(
        functools.partial(_sb_sample_kernel, n_pages=n_pages),
        out_shape=jax.ShapeDtypeStruct((bs, heads, HEAD_DIM), BF16),
        grid_spec=grid_spec,
        compiler_params=_cparams(("arbitrary",), 40),
        name="sb_sample",
    )(page_table, sb_bias, q.reshape(bs, heads, HEAD_DIM), *([cache_k] * n_pages), *([cache_v] * n_pages))
    return out.reshape(bs, SB_W)


def _lru_gates(xc, wa_ref, ba, wx_ref, bx, nsp_lambda):
    xb = xc.astype(BF16)
    r_parts, i_parts = [], []
    for h in range(LRU_HEADS):
        xh = xb[:, h * HEAD_DIM:(h + 1) * HEAD_DIM]
        r_parts.append(_dot(xh, wa_ref[h].astype(BF16)))
        i_parts.append(_dot(xh, wx_ref[h].astype(BF16)))
    r = _sigmoid(jnp.concatenate(r_parts, axis=1) + ba)
    i = _sigmoid(jnp.concatenate(i_parts, axis=1) + bx)
    log_a = -LRU_C * r * nsp_lambda
    a = jnp.exp(log_a)
    u = jnp.sqrt(1.0 - jnp.exp(2.0 * log_a)) * (i * xc)
    return a, u


def _gelu_tanh(x):
    return 0.5 * x * (1.0 + jnp.tanh(0.7978845608028654 * (x + 0.044715 * (x * x * x))))


def _lru_prompt_kernel(x_ref, gate_ref, prev_ref, h0_ref, cw_ref, cb_ref, wa_ref, ba_ref, wx_ref, bx_ref,
                       lam_ref, o_ref, hlast_ref, xp_ref, h_ref):
    ti = pl.program_id(1)
    tt = x_ref.shape[0]
    pad = 8

    @pl.when(ti == 0)
    def _():
        xp_ref[pad - (CONV_W - 1):pad, :] = prev_ref[0]
        h_ref[...] = h0_ref[0]

    x = x_ref[...]
    xp_ref[pad:pad + tt, :] = x
    cw = cw_ref[...]
    xc = cb_ref[...] + cw[CONV_W - 1:CONV_W, :] * x
    for j in range(CONV_W - 1):
        xc = xc + cw[j:j + 1, :] * xp_ref[pad - (CONV_W - 1) + j:pad - (CONV_W - 1) + j + tt, :]
    xp_ref[pad - (CONV_W - 1):pad, :] = x[tt - (CONV_W - 1):tt, :]

    nsp = _softplus(-lam_ref[...])
    a, u = _lru_gates(xc, wa_ref, ba_ref[...], wx_ref, bx_ref[...], nsp)
    rows = lax.broadcasted_iota(jnp.int32, (tt, LRU_W), 0)
    b = jnp.where(rows == 0, u + a * h_ref[...], u)
    s = 1
    while s < tt:
        keep = rows >= s
        b = jnp.where(keep, a * pltpu.roll(b, s, 0) + b, b)
        if 2 * s < tt:
            a = jnp.where(keep, a * pltpu.roll(a, s, 0), a)
        s *= 2
    h_ref[...] = b[tt - 1:tt, :]
    o_ref[...] = (b * _gelu_tanh(gate_ref[...])).astype(o_ref.dtype)

    @pl.when(ti == pl.num_programs(1) - 1)
    def _():
        hlast_ref[0] = b[tt - 1:tt, :]


def _lru_prompt(proj, conv_prev, h0, conv_w, conv_b, lru_wa, lru_ba, lru_wx, lru_bx, lru_lambda, bsz, t):
    tt = LRU_TT
    assert t % tt == 0
    nt = t // tt
    xl, gl = 0, 1
    full = lambda shape: pl.BlockSpec(shape, lambda b, i: (0,) * len(shape))
    o_lru, h_last = pl.pallas_call(
        _lru_prompt_kernel,
        out_shape=(jax.ShapeDtypeStruct((bsz * t, LRU_W), BF16),
                   jax.ShapeDtypeStruct((bsz, 1, LRU_W), F32)),
        grid=(bsz, nt),
        in_specs=[
            pl.BlockSpec((tt, LRU_W), lambda b, i: (b * nt + i, xl)),
            pl.BlockSpec((tt, LRU_W), lambda b, i: (b * nt + i, gl)),
            pl.BlockSpec((1, CONV_W - 1, LRU_W), lambda b, i: (b, 0, 0)),
            pl.BlockSpec((1, 1, LRU_W), lambda b, i: (b, 0, 0)),
            full((CONV_W, LRU_W)), full((1, LRU_W)),
            full((LRU_HEADS, HEAD_DIM, HEAD_DIM)), full((1, LRU_W)),
            full((LRU_HEADS, HEAD_DIM, HEAD_DIM)), full((1, LRU_W)),
            full((1, LRU_W)),
        ],
        out_specs=(pl.BlockSpec((tt, LRU_W), lambda b, i: (b * nt + i, 0)),
                   pl.BlockSpec((1, 1, LRU_W), lambda b, i: (b, 0, 0))),
        scratch_shapes=[pltpu.VMEM((8 + tt, LRU_W), F32), pltpu.VMEM((1, LRU_W), F32)],
        compiler_params=_cparams(("parallel", "arbitrary"), 32),
        name="lru_prompt",
    )(proj, proj, conv_prev, h0.reshape(bsz, 1, LRU_W), conv_w, conv_b.reshape(1, LRU_W),
      lru_wa, lru_ba.reshape(1, LRU_W), lru_wx, lru_bx.reshape(1, LRU_W), lru_lambda.reshape(1, LRU_W))
    return o_lru, h_last.reshape(bsz, LRU_W)


def _lru_step_kernel(x_ref, gate_ref, p0_ref, p1_ref, p2_ref, h0_ref, cw_ref, cb_ref, wa_ref, ba_ref, wx_ref,
                     bx_ref, lam_ref, o_ref, h_ref):
    cw = cw_ref[...]
    xc = (cb_ref[...] + cw[0:1, :] * p0_ref[...] + cw[1:2, :] * p1_ref[...] + cw[2:3, :] * p2_ref[...]
          + cw[3:4, :] * x_ref[...])
    nsp = _softplus(-lam_ref[...])
    a, u = _lru_gates(xc, wa_ref, ba_ref[...], wx_ref, bx_ref[...], nsp)
    h = u + a * h0_ref[...]
    h_ref[...] = h
    o_ref[...] = (h * _gelu_tanh(gate_ref[...])).astype(o_ref.dtype)


def _lru_step(x_lru, gate_lru, state_conv, h0, conv_w, conv_b, lru_wa, lru_ba, lru_wx, lru_bx, lru_lambda):
    bs = x_lru.shape[0]
    assert CONV_W == 4
    prevs = [state_conv[:, j, :] for j in range(CONV_W - 1)]
    o_lru, h_new = pl.pallas_call(
        _lru_step_kernel,
        out_shape=(jax.ShapeDtypeStruct((bs, LRU_W), BF16), jax.ShapeDtypeStruct((bs, LRU_W), F32)),
        name="lru_step",
    )(x_lru, gate_lru, *prevs, h0, conv_w, conv_b.reshape(1, LRU_W), lru_wa, lru_ba.reshape(1, LRU_W),
      lru_wx, lru_bx.reshape(1, LRU_W), lru_lambda.reshape(1, LRU_W))
    return o_lru, h_new


def _mem_prompt_kernel(q_ref, k_ref, v_ref, o_ref):
    s = _dot_nt(q_ref[...].astype(BF16), k_ref[...].astype(BF16)) * QK_SCALE
    m = jnp.max(s, axis=-1, keepdims=True)
    e = jnp.exp(s - m)
    p = e / jnp.sum(e, axis=-1, keepdims=True)
    o_ref[...] = _dot(p.astype(BF16), v_ref[...].astype(BF16)).astype(o_ref.dtype)


def _mem_prompt(proj, qcol0, mem_k, mem_v, t):
    bsz, _, n_mem, _ = mem_k.shape
    tq = min(t, 512)
    assert t % tq == 0 and qcol0 % HEAD_DIM == 0
    nt = t // tq
    qc = qcol0 // HEAD_DIM
    kv_spec = pl.BlockSpec((None, None, n_mem, HEAD_DIM), lambda b, h, i: (b, h, 0, 0))
    return pl.pallas_call(
        _mem_prompt_kernel,
        out_shape=jax.ShapeDtypeStruct((bsz * t, MEM_W), BF16),
        grid=(bsz, MEM_HEADS, nt),
        in_specs=[pl.BlockSpec((tq, HEAD_DIM), lambda b, h, i: (b * nt + i, qc + h)), kv_spec, kv_spec],
        out_specs=pl.BlockSpec((tq, HEAD_DIM), lambda b, h, i: (b * nt + i, h)),
        compiler_params=_cparams(("parallel", "parallel", "parallel"), 32),
        name="mem_prompt",
    )(proj, mem_k, mem_v)


def _mem_sample_kernel(q_ref, k_ref, v_ref, o_ref):
    group = q_ref.shape[0]
    for g in range(group):
        q = q_ref[g]
        s = jnp.sum(k_ref[g] * q[None], axis=-1, keepdims=True) * QK_SCALE
        m = jnp.max(s, axis=0, keepdims=True)
        e = jnp.exp(s - m)
        p = e / jnp.sum(e, axis=0, keepdims=True)
        o_ref[g] = jnp.sum(p * v_ref[g], axis=0).astype(o_ref.dtype)


def _mem_sample(q, cache_k, cache_v):
    bs, n_mem, heads, _ = cache_k.shape
    assert heads == MEM_HEADS and cache_k.shape[3] == HEAD_DIM
    group = 8
    assert bs % group == 0
    kv_spec = pl.BlockSpec((group, n_mem, heads, HEAD_DIM), lambda i: (i, 0, 0, 0))
    out = pl.pallas_call(
        _mem_sample_kernel,
        out_shape=jax.ShapeDtypeStruct((bs, heads, HEAD_DIM), BF16),
        grid=(bs // group,),
        in_specs=[pl.BlockSpec((group, heads, HEAD_DIM), lambda i: (i, 0, 0)), kv_spec, kv_spec],
        out_specs=pl.BlockSpec((group, heads, HEAD_DIM), lambda i: (i, 0, 0)),
        compiler_params=_cparams(("parallel",), 32),
        name="mem_sample",
    )(q.reshape(bs, heads, HEAD_DIM), cache_k, cache_v)
    return out.reshape(bs, MEM_W)


def _merge_kernel(oa_ref, ol_ref, om_ref, pa_ref, pl_ref, pm_ref, ga_ref, gl_ref, gm_ref, o_ref):
    ya = _dot(oa_ref[...], pa_ref[...].astype(BF16))
    yl = _dot(ol_ref[...], pl_ref[...].astype(BF16))
    ym = _dot(om_ref[...], pm_ref[...].astype(BF16))
    merged = _sigmoid(ga_ref[...]) * ya + _sigmoid(gl_ref[...]) * yl + _sigmoid(gm_ref[...]) * ym
    o_ref[...] = merged.astype(o_ref.dtype)


def _merge(o_sb, o_lru, o_mem, p_attn, p_lru, p_mem, gates, tm):
    r = o_sb.shape[0]
    d = p_attn.shape[1]
    tn = 512
    assert r % tm == 0 and d % tn == 0
    nj = d // tn
    return pl.pallas_call(
        _merge_kernel,
        out_shape=jax.ShapeDtypeStruct((r, d), BF16),
        grid=(r // tm, nj),
        in_specs=[
            pl.BlockSpec((tm, SB_W), lambda i, j: (i, 0)),
            pl.BlockSpec((tm, LRU_W), lambda i, j: (i, 0)),
            pl.BlockSpec((tm, MEM_W), lambda i, j: (i, 0)),
            pl.BlockSpec((SB_W, tn), lambda i, j: (0, j)),
            pl.BlockSpec((LRU_W, tn), lambda i, j: (0, j)),
            pl.BlockSpec((MEM_W, tn), lambda i, j: (0, j)),
            pl.BlockSpec((tm, tn), lambda i, j: (i, j)),
            pl.BlockSpec((tm, tn), lambda i, j: (i, nj + j)),
            pl.BlockSpec((tm, tn), lambda i, j: (i, 2 * nj + j)),
        ],
        out_specs=pl.BlockSpec((tm, tn), lambda i, j: (i, j)),
        compiler_params=_cparams(("parallel", "parallel"), 44),
        name="merge",
    )(o_sb, o_lru, o_mem, p_attn, p_lru, p_mem, gates, gates, gates)


def _post_kernel(x_ref, m_ref, wo_ref, g_ref, wrh_ref, wrl_ref, br_ref, cnt0_ref,
                 hres_ref, hnp_ref, idx_ref, gate_ref, rank_ref, cnt_ref):
    i = pl.program_id(0)
    tm, d = x_ref.shape

    @pl.when(i == 0)
    def _():
        cnt_ref[...] = cnt0_ref[...]

    hres = x_ref[...] + _dot(m_ref[...], wo_ref[...])
    hres_ref[...] = hres
    ms = jnp.mean(hres * hres, axis=-1, keepdims=True)
    hn = hres * lax.rsqrt(ms + EPS) * g_ref[...]
    hb = hn.astype(BF16)
    wa = lax.bitcast_convert_type(hb[:, :d // 2].astype(F32), jnp.uint32)
    wb = lax.bitcast_convert_type(hb[:, d // 2:].astype(F32), jnp.uint32)
    hnp_ref[...] = wa | (wb >> 16)

    hn_lo = (hn - hb.astype(F32)).astype(BF16)
    logits = (_dot_nt(wrh_ref[...], hb) + _dot_nt(wrh_ref[...], hn_lo) + _dot_nt(wrl_ref[...], hb)
              + br_ref[...])
    ne = logits.shape[0]
    eid = lax.broadcasted_iota(jnp.int32, (ne, tm), 0)
    work = logits
    vals, idxs, onehots = [], [], []
    for _ in range(TOP_K):
        mx = jnp.max(work, axis=0, keepdims=True)
        sel = jnp.min(jnp.where(work == mx, eid, ne), axis=0, keepdims=True)
        oh = eid == sel
        vals.append(mx)
        idxs.append(sel)
        onehots.append(oh)
        work = jnp.where(oh, -jnp.inf, work)
    es = [jnp.exp(v - vals[0]) for v in vals]
    den = functools.reduce(jnp.add, es)
    gate_ref[...] = jnp.concatenate([e / den for e in es], axis=0)
    idx_ref[...] = jnp.concatenate(idxs, axis=0)

    chosen = functools.reduce(jnp.logical_or, onehots)
    r_ = lax.broadcasted_iota(jnp.int32, (tm, tm), 0)
    c_ = lax.broadcasted_iota(jnp.int32, (tm, tm), 1)
    before = jnp.where(r_ < c_, 1.0, 0.0).astype(BF16)
    chosen_f = jnp.where(chosen, 1.0, 0.0)
    prior = _dot(chosen_f.astype(BF16), before) + cnt_ref[...]
    rank_ref[...] = jnp.concatenate(
        [jnp.sum(jnp.where(oh, prior, 0.0), axis=0, keepdims=True) for oh in onehots], axis=0).astype(jnp.int32)
    cnt_ref[...] = cnt_ref[...] + jnp.sum(chosen_f, axis=1, keepdims=True)


def _post(x, merged, wo_b, g, wr_hi, wr_lo, b_router, cnt0, tm):
    r, d = x.shape
    ne = wr_hi.shape[0]
    assert r % tm == 0
    full = lambda shape: pl.BlockSpec(shape, lambda i: (0,) * len(shape))
    return pl.pallas_call(
        _post_kernel,
        out_shape=(jax.ShapeDtypeStruct((r, d), F32),
                   jax.ShapeDtypeStruct((r, d // 2), jnp.uint32),
                   jax.ShapeDtypeStruct((TOP_K, r), jnp.int32),
                   jax.ShapeDtypeStruct((TOP_K, r), F32),
                   jax.ShapeDtypeStruct((TOP_K, r), jnp.int32),
                   jax.ShapeDtypeStruct((ne, 1), F32)),
        grid=(r // tm,),
        in_specs=[pl.BlockSpec((tm, d), lambda i: (i, 0)),
                  pl.BlockSpec((tm, d), lambda i: (i, 0)),
                  full((d, d)), full((1, d)), full((ne, d)), full((ne, d)), full((ne, 1)), full((ne, 1))],
        out_specs=(pl.BlockSpec((tm, d), lambda i: (i, 0)),
                   pl.BlockSpec((tm, d // 2), lambda i: (i, 0)),
                   pl.BlockSpec((TOP_K, tm), lambda i: (0, i)),
                   pl.BlockSpec((TOP_K, tm), lambda i: (0, i)),
                   pl.BlockSpec((TOP_K, tm), lambda i: (0, i)),
                   full((ne, 1))),
        compiler_params=_cparams(("arbitrary",), 48),
        name="post",
    )(x, merged, wo_b, g.reshape(1, d), wr_hi, wr_lo, b_router.reshape(ne, 1), cnt0)


def _dispatch_kernel(tail_ref, dest_ref, hn_ref, xs_ref, zero_ref, sem, zsem):
    i = pl.program_id(0)
    tm = dest_ref.shape[2] // TOP_K

    @pl.when(i == 0)
    def _():
        zero_ref[...] = jnp.zeros_like(zero_ref)

        def tail_copy(e):
            start = pl.multiple_of(tail_ref[e], MOE_SUB)
            return pltpu.make_async_copy(zero_ref, xs_ref.at[pl.ds(start, MOE_SUB)], zsem)

        def zstart(e, c):
            pl.when(tail_ref[e] >= 0)(lambda: tail_copy(e).start())
            return c

        def zwait(e, c):
            pl.when(tail_ref[e] >= 0)(lambda: tail_copy(e).wait())
            return c

        lax.fori_loop(0, tail_ref.shape[0], zstart, 0)
        lax.fori_loop(0, tail_ref.shape[0], zwait, 0)

    def row_copy(t, k):
        return pltpu.make_async_copy(hn_ref.at[pl.ds(i * tm + t, 1)],
                                     xs_ref.at[pl.ds(dest_ref[0, 0, k * tm + t], 1)], sem)

    def start(t, c):
        for k in range(TOP_K):
            row_copy(t, k).start()
        return c

    def wait(t, c):
        for k in range(TOP_K):
            row_copy(t, k).wait()
        return c

    lax.fori_loop(0, tm, start, 0)
    lax.fori_loop(0, tm, wait, 0)


def _dispatch(hnp, dest, tail, n_rows, tm):
    r, w = hnp.shape
    assert r % tm == 0
    nt = r // tm
    dest_t = dest.reshape(TOP_K, nt, tm).transpose(1, 0, 2).reshape(nt, 1, TOP_K * tm)
    grid_spec = pltpu.PrefetchScalarGridSpec(
        num_scalar_prefetch=1,
        grid=(nt,),
        in_specs=[pl.BlockSpec((1, 1, TOP_K * tm), lambda i, tl: (i, 0, 0), memory_space=pltpu.SMEM),
                  pl.BlockSpec(memory_space=pl.ANY)],
        out_specs=pl.BlockSpec(memory_space=pl.ANY),
        scratch_shapes=[pltpu.VMEM((MOE_SUB, w), hnp.dtype), pltpu.SemaphoreType.DMA, pltpu.SemaphoreType.DMA],
    )
    return pl.pallas_call(
        _dispatch_kernel,
        out_shape=jax.ShapeDtypeStruct((n_rows, w), hnp.dtype),
        grid_spec=grid_spec,
        compiler_params=_cparams(("arbitrary",), 16),
        name="dispatch",
    )(tail, dest_t, hnp)


def _moe_kernel(ce_ref, cb_ref, ns_ref, xs_ref, wg_ref, wu_ref, bg_ref, bu_ref, wd_ref, bd_ref, o_ref,
                xa_ref, xb_ref, act_ref, wgb_ref, wub_ref, wdb_ref, *, nf):
    del ce_ref, cb_ref
    c = pl.program_id(0)
    s = pl.program_id(1)
    nsub = ns_ref[c]
    sub = MOE_SUB
    half = xa_ref.shape[1]

    @pl.when((s == 0) & (nsub > 0))
    def _():
        def unpack(i, carry):
            rows = pl.ds(pl.multiple_of(i * sub, sub), sub)
            word = xs_ref[rows, :]
            xa_ref[rows, :] = lax.bitcast_convert_type(word & jnp.uint32(0xFFFF0000), F32).astype(BF16)
            xb_ref[rows, :] = lax.bitcast_convert_type(word << 16, F32).astype(BF16)
            return carry
        lax.fori_loop(0, nsub, unpack, 0)

    @pl.when((s < nf) & (nsub > 0))
    def _():
        wgb_ref[...] = wg_ref[...].astype(BF16)
        wub_ref[...] = wu_ref[...].astype(BF16)
        bg = bg_ref[...]
        bu = bu_ref[...]

        def up(i, carry):
            rows = pl.ds(pl.multiple_of(i * sub, sub), sub)
            xa = xa_ref[rows, :]
            xb = xb_ref[rows, :]
            gate = _dot(xa, wgb_ref[:half, :]) + _dot(xb, wgb_ref[half:, :]) + bg
            upv = _dot(xa, wub_ref[:half, :]) + _dot(xb, wub_ref[half:, :]) + bu
            gate = jnp.minimum(gate, SWIGLU_LIMIT)
            upv = jnp.clip(upv, -SWIGLU_LIMIT, SWIGLU_LIMIT)
            act = gate * _sigmoid(SWIGLU_ALPHA * gate) * (upv + 1.0)
            act_ref[s, rows, :] = act.astype(BF16)
            return carry
        lax.fori_loop(0, nsub, up, 0)

    @pl.when((s >= nf) & (nsub > 0))
    def _():
        wdb_ref[...] = wd_ref[...].astype(BF16)
        bd = bd_ref[...]
        tf = act_ref.shape[2]

        def down(i, carry):
            rows = pl.ds(pl.multiple_of(i * sub, sub), sub)
            y = _dot(act_ref[0, rows, :], wdb_ref[0:tf, :])
            for f in range(1, nf):
                y = y + _dot(act_ref[f, rows, :], wdb_ref[f * tf:(f + 1) * tf, :])
            o_ref[rows, :] = y + bd
            return carry
        lax.fori_loop(0, nsub, down, 0)


def _moe(xs, chunk_expert, chunk_block, chunk_nsub, w_up, b_up, w_down, b_down):
    ne, d, dff2 = w_up.shape
    dff = dff2 // 2
    rb = MOE_SUB * MOE_CHUNK_SUBS
    n_chunks = xs.shape[0] // rb
    tf = MOE_TF
    nf = dff // tf
    nn = d // tf
    assert dff % tf == 0 and d % tf == 0

    def up_f(s, ns_c):
        return jnp.where(ns_c > 0, jnp.minimum(s, nf - 1), nf - 1)

    def dn_n(s, ns_c):
        return jnp.where(ns_c > 0, jnp.maximum(s - nf, 0), nn - 1)

    grid_spec = pltpu.PrefetchScalarGridSpec(
        num_scalar_prefetch=3,
        grid=(n_chunks, nf + nn),
        in_specs=[
            pl.BlockSpec((rb, d // 2), lambda c, s, ce, cb, ns: (cb[c], 0)),
            pl.BlockSpec((None, d, tf), lambda c, s, ce, cb, ns: (ce[c], 0, up_f(s, ns[c]))),
            pl.BlockSpec((None, d, tf), lambda c, s, ce, cb, ns: (ce[c], 0, nf + up_f(s, ns[c]))),
            pl.BlockSpec((None, 1, tf), lambda c, s, ce, cb, ns: (ce[c], 0, up_f(s, ns[c]))),
            pl.BlockSpec((None, 1, tf), lambda c, s, ce, cb, ns: (ce[c], 0, nf + up_f(s, ns[c]))),
            pl.BlockSpec((None, dff, tf), lambda c, s, ce, cb, ns: (ce[c], 0, dn_n(s, ns[c]))),
            pl.BlockSpec((None, 1, tf), lambda c, s, ce, cb, ns: (ce[c], 0, dn_n(s, ns[c]))),
        ],
        out_specs=pl.BlockSpec((rb, tf), lambda c, s, ce, cb, ns: (cb[c], dn_n(s, ns[c]))),
        scratch_shapes=[pltpu.VMEM((rb, d // 2), BF16), pltpu.VMEM((rb, d // 2), BF16),
                        pltpu.VMEM((nf, rb, tf), BF16),
                        pltpu.VMEM((d, tf), BF16), pltpu.VMEM((d, tf), BF16), pltpu.VMEM((dff, tf), BF16)],
    )
    return pl.pallas_call(
        functools.partial(_moe_kernel, nf=nf),
        out_shape=jax.ShapeDtypeStruct((n_chunks * rb, d), F32),
        grid_spec=grid_spec,
        compiler_params=_cparams(("arbitrary", "arbitrary"), 52),
        name="moe",
    )(chunk_expert, chunk_block, chunk_nsub, xs, w_up, w_up, b_up.reshape(ne, 1, dff2), b_up.reshape(ne, 1, dff2),
      w_down, b_down.reshape(ne, 1, d))


def _combine_kernel(dest_ref, hres_ref, gate_ref, yb_ref, o_ref, buf_ref, sem):
    tm = hres_ref.shape[0]

    def row_copy(t, k):
        return pltpu.make_async_copy(yb_ref.at[pl.ds(dest_ref[0, 0, k * tm + t], 1)],
                                     buf_ref.at[k, pl.ds(t, 1)], sem)

    def start(t, c):
        for k in range(TOP_K):
            row_copy(t, k).start()
        return c

    def wait(t, c):
        for k in range(TOP_K):
            row_copy(t, k).wait()
        return c

    lax.fori_loop(0, tm, start, 0)
    lax.fori_loop(0, tm, wait, 0)
    g = gate_ref[...]
    y = hres_ref[...]
    for k in range(TOP_K):
        y = y + g[:, k:k + 1] * buf_ref[k]
    o_ref[...] = y


def _combine(hres, gates, dest, yb, tm):
    r, d = hres.shape
    assert r % tm == 0
    nt = r // tm
    dest_t = dest.reshape(TOP_K, nt, tm).transpose(1, 0, 2).reshape(nt, 1, TOP_K * tm)
    return pl.pallas_call(
        _combine_kernel,
        out_shape=jax.ShapeDtypeStruct((r, d), F32),
        grid=(nt,),
        in_specs=[pl.BlockSpec((1, 1, TOP_K * tm), lambda i: (i, 0, 0), memory_space=pltpu.SMEM),
                  pl.BlockSpec((tm, d), lambda i: (i, 0)),
                  pl.BlockSpec((tm, TOP_K), lambda i: (i, 0)),
                  pl.BlockSpec(memory_space=pl.ANY)],
        out_specs=pl.BlockSpec((tm, d), lambda i: (i, 0)),
        scratch_shapes=[pltpu.VMEM((TOP_K, tm, d), F32), pltpu.SemaphoreType.DMA],
        compiler_params=_cparams(("arbitrary",), 32),
        name="combine",
    )(dest_t, hres, gates.T, yb)


def _row_tile(r, cap):
    tm = min(r, cap)
    assert r % tm == 0
    return tm


def kernel(x_prompt, x_sample, mem_prompt, cache_sb_k, cache_sb_v, page_table, cache_mem_k, cache_mem_v, state_conv, state_lru, norm_mix_g, norm_mem_g, w_in, q_sb_g, k_sb_g, sb_bias, q_mem_g, k_mem_g, w_mem_kv, conv_w, conv_b, lru_wa, lru_ba, lru_wx, lru_bx, lru_lambda, p_attn, p_lru, p_mem, w_o, norm_ffn_g, w_router, b_router, w_up, b_up, w_down, b_down):
    bp, t, d = x_prompt.shape
    bs, ts, _ = x_sample.shape
    assert ts == 1, "sample group decodes one token per sequence"
    n_mem = mem_prompt.shape[1]
    q_gain = jnp.tile(q_sb_g, SB_HEADS)
    k_gain = jnp.tile(k_sb_g, SB_HEADS)
    qm_gain = jnp.tile(q_mem_g, MEM_HEADS)
    heads_major = lambda a: jnp.transpose(a, (0, 2, 1, 3))

    memn = _rmsnorm(mem_prompt.reshape(bp * n_mem, d), norm_mem_g, _row_tile(bp * n_mem, 256))
    mem_tm = _row_tile(n_mem, 256)
    mem_k_hm = _proj(memn, w_mem_kv, jnp.tile(k_mem_g, MEM_HEADS), col0=0, ncols=MEM_W, norm_cols=(0, MEM_W),
                     tm=mem_tm, tn=MEM_W, heads_out=(bp, n_mem))
    mem_v_hm = _proj(memn, w_mem_kv, None, col0=MEM_W, ncols=MEM_W, tm=mem_tm, tn=MEM_W, heads_out=(bp, n_mem))

    rest_w = COL_GATE - COL_XL
    qm_cols = (COL_QM - COL_XL, rest_w)
    w_q = _cast_bf16(w_in, COL_Q, SB_W)
    w_k = _cast_bf16(w_in, COL_K, SB_W)
    w_v = _cast_bf16(w_in, COL_V, SB_W)
    w_rest = _cast_bf16(w_in, COL_XL, rest_w)
    w_gates = _cast_bf16(w_in, COL_GATE, N_BRANCH * d)
    rest_gain = jnp.concatenate([jnp.ones((2 * LRU_W,), F32), qm_gain])

    xp2 = x_prompt.reshape(bp * t, d)
    tm_p = _row_tile(t, 1024)
    xn_p = _rmsnorm(xp2, norm_mix_g, _row_tile(bp * t, 512))
    q_p = _proj(xn_p, w_q, q_gain, ncols=SB_W, norm_cols=(0, SB_W), tm=tm_p, tn=SB_W)
    k_p = _proj(xn_p, w_k, k_gain, ncols=SB_W, norm_cols=(0, SB_W), tm=tm_p, tn=SB_W, heads_out=(bp, t))
    v_p = _proj(xn_p, w_v, None, ncols=SB_W, tm=tm_p, tn=SB_W, heads_out=(bp, t))
    rest_p = _proj(xn_p, w_rest, rest_gain, ncols=rest_w, norm_cols=qm_cols, tm=tm_p, tn=1024)
    gates_p = _proj(xn_p, w_gates, None, ncols=N_BRANCH * d, tm=tm_p, tn=1024)
    o_sb_p = _sb_prompt(q_p, k_p, v_p, sb_bias)
    conv0 = jnp.zeros((bp, CONV_W - 1, LRU_W), x_prompt.dtype)
    lru0 = jnp.zeros((bp, LRU_W), state_lru.dtype)
    o_lru_p, lru_prompt = _lru_prompt(rest_p, conv0, lru0, conv_w, conv_b, lru_wa, lru_ba, lru_wx, lru_bx,
                                      lru_lambda, bp, t)
    o_mem_p = _mem_prompt(rest_p, COL_QM - COL_XL, mem_k_hm, mem_v_hm, t)
    xl_tail = rest_p.reshape(bp, t, -1)[:, t - (CONV_W - 1):, :LRU_W]
    conv_prompt = jnp.concatenate([conv0, xl_tail], axis=1)[:, -(CONV_W - 1):]

    xs2 = x_sample.reshape(bs, d)
    xn_s = _rmsnorm(xs2, norm_mix_g, bs)
    q_s = _proj(xn_s, w_q, q_gain, ncols=SB_W, norm_cols=(0, SB_W), tm=bs, tn=SB_W)
    k_s = _proj(xn_s, w_k, k_gain, ncols=SB_W, norm_cols=(0, SB_W), tm=bs, tn=SB_W)
    v_s = _proj(xn_s, w_v, None, ncols=SB_W, tm=bs, tn=SB_W)
    rest_s = _proj(xn_s, w_rest, rest_gain, ncols=rest_w, norm_cols=qm_cols, tm=bs, tn=1024)
    gates_s = _proj(xn_s, w_gates, None, ncols=N_BRANCH * d, tm=bs, tn=1024)
    o_sb_s = _sb_sample(q_s, sb_bias, heads_major(cache_sb_k), heads_major(cache_sb_v), page_table)
    xl_s = rest_s[:, :LRU_W]
    o_lru_s, lru_sample = _lru_step(xl_s, rest_s[:, LRU_W:2 * LRU_W], state_conv, state_lru, conv_w, conv_b,
                                    lru_wa, lru_ba, lru_wx, lru_bx, lru_lambda)
    conv_sample = jnp.concatenate([state_conv, xl_s[:, None, :]], axis=1)[:, -(CONV_W - 1):]
    o_mem_s = _mem_sample(rest_s[:, 2 * LRU_W:], cache_mem_k, cache_mem_v)

    wo_b = w_o.astype(BF16)
    wr_t = w_router.T
    wr_hi = wr_t.astype(BF16)
    wr_lo = (wr_t - wr_hi.astype(F32)).astype(BF16)
    merged_p = _merge(o_sb_p, o_lru_p, o_mem_p, p_attn, p_lru, p_mem, gates_p, _row_tile(bp * t, 1024))
    merged_s = _merge(o_sb_s, o_lru_s, o_mem_s, p_attn, p_lru, p_mem, gates_s, _row_tile(bs, 1024))
    cnt0 = jnp.zeros((N_EXPERTS, 1), F32)
    hres_p, hnp_p, idx_p, gate_p, rank_p, cnt_p = _post(xp2, merged_p, wo_b, norm_ffn_g, wr_hi, wr_lo, b_router,
                                                        cnt0, _row_tile(bp * t, 256))
    hres_s, hnp_s, idx_s, gate_s, rank_s, cnt_all = _post(xs2, merged_s, wo_b, norm_ffn_g, wr_hi, wr_lo, b_router,
                                                          cnt_p, _row_tile(bs, 256))

    rb = MOE_SUB * MOE_CHUNK_SUBS
    n_assign = (bp * t + bs) * TOP_K
    n_chunks = -(-n_assign // rb) + N_EXPERTS
    counts = cnt_all[:, 0].astype(jnp.int32)
    chunks_e = (counts + rb - 1) // rb
    chunk_end = jnp.cumsum(chunks_e)
    chunk_start = chunk_end - chunks_e
    used = chunk_end[-1]
    cidx = jnp.arange(n_chunks, dtype=jnp.int32)
    last = jnp.minimum(cidx, used - 1)
    chunk_expert = jnp.clip(jnp.searchsorted(chunk_end, last, side='right'), 0, N_EXPERTS - 1).astype(jnp.int32)
    rows_left = counts[chunk_expert] - (last - chunk_start[chunk_expert]) * rb
    chunk_nsub = jnp.where(cidx < used, (jnp.clip(rows_left, 0, rb) + MOE_SUB - 1) // MOE_SUB, 0).astype(jnp.int32)
    chunk_block = last.astype(jnp.int32)
    row_start = chunk_start * rb
    experts = jnp.arange(N_EXPERTS, dtype=jnp.int32)

    def slots(idx, rank):
        return jnp.sum(jnp.where(idx[..., None] == experts, row_start, 0), axis=-1) + rank

    dest_p = slots(idx_p, rank_p)
    dest_s = slots(idx_s, rank_s)
    tail = jnp.where(counts % MOE_SUB != 0, row_start + counts // MOE_SUB * MOE_SUB, -1).astype(jnp.int32)

    hnp = jnp.concatenate([hnp_p, hnp_s], axis=0)
    dest = jnp.concatenate([dest_p, dest_s], axis=1)
    xs = _dispatch(hnp, dest, tail, n_chunks * rb, _row_tile(bp * t + bs, 128))
    yb = _moe(xs, chunk_expert, chunk_block, chunk_nsub, w_up, b_up, w_down, b_down)
    y_prompt = _combine(hres_p, gate_p, dest_p, yb, _row_tile(bp * t, 256)).reshape(bp, t, d)
    y_sample = _combine(hres_s, gate_s, dest_s, yb, _row_tile(bs, 256)).reshape(bs, 1, d)

    sb_k_sample = k_s.reshape(bs, 1, SB_HEADS, HEAD_DIM)
    sb_v_sample = v_s.reshape(bs, 1, SB_HEADS, HEAD_DIM)
    return (y_prompt, y_sample, heads_major(k_p), heads_major(v_p), heads_major(mem_k_hm), heads_major(mem_v_hm),
            conv_prompt, lru_prompt, sb_k_sample, sb_v_sample, conv_sample, lru_sample)
```

```python
import functools

import jax
import jax.numpy as jnp
from jax import lax
from jax.experimental import pallas as pl
from jax.experimental.pallas import tpu as pltpu

F32 = jnp.float32
BF16 = jnp.bfloat16

HEAD_DIM = 128
SB_HEADS = 6
SB_W = SB_HEADS * HEAD_DIM
LRU_HEADS = 6
LRU_W = LRU_HEADS * HEAD_DIM
LRU_C = 8.0
CONV_W = 4
MEM_HEADS = 4
MEM_W = MEM_HEADS * HEAD_DIM
N_BRANCH = 3
N_EXPERTS = 32
TOP_K = 4
SWIGLU_LIMIT = 7.0
SWIGLU_ALPHA = 1.702
EPS = 1e-6
QK_SCALE = HEAD_DIM ** -0.5

COL_Q, COL_K, COL_V = 0, SB_W, 2 * SB_W
COL_XL, COL_GL = 3 * SB_W, 3 * SB_W + LRU_W
COL_QM = 3 * SB_W + 2 * LRU_W
COL_GATE = COL_QM + MEM_W

MIB = 1024 * 1024
PROJ_TN = 256
SB_BLK = 256
SB_HEADS_PER_STEP = 3
LRU_TT = 256
MOE_SUB = 128
MOE_BLOCK_SUBS = (8, 4, 1)
MOE_CHUNK_SUBS = 12
MOE_TF = 256


def _cparams(sem, vmem_mib):
    return pltpu.CompilerParams(dimension_semantics=sem, vmem_limit_bytes=vmem_mib * MIB)


def _softplus(z):
    return jnp.maximum(z, 0.0) + jnp.log1p(jnp.exp(-jnp.abs(z)))


def _softplus_fast(z):
    return jnp.maximum(z, 0.0) + jnp.log(1.0 + jnp.exp(-jnp.abs(z)))


def _sigmoid(z):
    return 1.0 / (1.0 + jnp.exp(-z))


def _split_bf16(x):
    hi = x.astype(BF16)
    lo = (x - hi.astype(F32)).astype(BF16)
    return hi, lo


def _dot(a, b):
    return jnp.dot(a, b, preferred_element_type=F32)


def _dot_nt(a, b):
    return lax.dot_general(a, b, (((1,), (1,)), ((), ())), preferred_element_type=F32)


def _rmsnorm_kernel(x_ref, g_ref, o_ref):
    x = x_ref[...]
    ms = jnp.mean(x * x, axis=-1, keepdims=True)
    o_ref[...] = (x * lax.rsqrt(ms + EPS) * g_ref[...]).astype(o_ref.dtype)


def _rmsnorm(x, g, tm):
    r, d = x.shape
    assert r % tm == 0
    return pl.pallas_call(
        _rmsnorm_kernel,
        out_shape=jax.ShapeDtypeStruct((r, d), BF16),
        grid=(r // tm,),
        in_specs=[pl.BlockSpec((tm, d), lambda i: (i, 0)), pl.BlockSpec((1, d), lambda i: (0, 0))],
        out_specs=pl.BlockSpec((tm, d), lambda i: (i, 0)),
        compiler_params=_cparams(("parallel",), 32),
        name="rmsnorm",
    )(x, g.reshape(1, d))


def _cast_kernel(x_ref, o_ref):
    o_ref[...] = x_ref[...].astype(o_ref.dtype)


def _cast_bf16(w, col0=0, ncols=None):
    r, c = w.shape
    ncols = c - col0 if ncols is None else ncols
    tc = PROJ_TN
    assert col0 % tc == 0 and ncols % tc == 0
    c0 = col0 // tc
    return pl.pallas_call(
        _cast_kernel,
        out_shape=jax.ShapeDtypeStruct((r, ncols), BF16),
        grid=(ncols // tc,),
        in_specs=[pl.BlockSpec((r, tc), lambda j: (0, j + c0))],
        out_specs=pl.BlockSpec((r, tc), lambda j: (0, j)),
        compiler_params=_cparams(("parallel",), 32),
        name="cast_bf16",
    )(w)


def _proj_kernel(xn_ref, w_ref, gain_ref, o_ref, *, patterns, heads_out):
    j = pl.program_id(1)
    y = _dot(xn_ref[...], w_ref[...].astype(BF16))
    nh = y.shape[1] // HEAD_DIM

    def store(c, val):
        if heads_out:
            o_ref[c] = val
        else:
            o_ref[:, c * HEAD_DIM:(c + 1) * HEAD_DIM] = val

    def emit(pattern):
        gain = gain_ref[...] if any(pattern) else None
        for c in range(nh):
            yc = y[:, c * HEAD_DIM:(c + 1) * HEAD_DIM]
            if pattern[c]:
                ms = jnp.mean(yc * yc, axis=-1, keepdims=True)
                yc = yc * lax.rsqrt(ms + EPS) * gain[:, c * HEAD_DIM:(c + 1) * HEAD_DIM]
            store(c, yc)

    distinct = sorted(set(patterns))
    if len(distinct) == 1:
        emit(distinct[0])
    else:
        for pattern in distinct:
            tiles = [jj for jj, p in enumerate(patterns) if p == pattern]
            pred = functools.reduce(jnp.logical_or, [j == jj for jj in tiles])
            pl.when(pred)(functools.partial(emit, pattern))


def _proj(xn, w, gain, *, col0=0, ncols, norm_cols=(0, 0), tm, tn, heads_out=None):
    r, d = xn.shape
    assert r % tm == 0 and col0 % tn == 0 and ncols % tn == 0 and tn % HEAD_DIM == 0
    assert norm_cols[0] % HEAD_DIM == 0 and norm_cols[1] % HEAD_DIM == 0
    col_tile0 = col0 // tn
    patterns = tuple(tuple(norm_cols[0] <= jj * tn + c * HEAD_DIM < norm_cols[1] for c in range(tn // HEAD_DIM))
                     for jj in range(ncols // tn))
    if gain is None:
        gain = jnp.ones((ncols,), F32)
    if heads_out is None:
        out_shape = jax.ShapeDtypeStruct((r, ncols), F32)
        out_spec = pl.BlockSpec((tm, tn), lambda i, j: (i, j))
    else:
        bsz, t = heads_out
        assert bsz * t == r and t % tm == 0
        nt = t // tm
        hpt = tn // HEAD_DIM
        out_shape = jax.ShapeDtypeStruct((bsz, ncols // HEAD_DIM, t, HEAD_DIM), F32)
        out_spec = pl.BlockSpec((None, hpt, tm, HEAD_DIM), lambda i, j: (i // nt, j, i % nt, 0))
    kern = functools.partial(_proj_kernel, patterns=patterns, heads_out=heads_out is not None)
    return pl.pallas_call(
        kern,
        out_shape=out_shape,
        grid=(r // tm, ncols // tn),
        in_specs=[
            pl.BlockSpec((tm, d), lambda i, j: (i, 0)),
            pl.BlockSpec((d, tn), lambda i, j: (0, j + col_tile0)),
            pl.BlockSpec((1, tn), lambda i, j: (0, j)),
        ],
        out_specs=out_spec,
        compiler_params=_cparams(("parallel", "parallel"), 40),
        name="proj",
    )(xn, w, gain.reshape(1, ncols))


def _sb_prompt_kernel(bias_ref, q_ref, k_ref, v_ref, o_ref, kb_ref, vb_ref):
    hg = pl.program_id(1)
    qi = pl.program_id(2)
    blk = q_ref.shape[0]
    nh = k_ref.shape[0]

    @pl.when(qi == 0)
    def _():
        kb_ref[...] = k_ref[...].astype(BF16)
        vb_ref[...] = v_ref[...].astype(BF16)

    qs = [q_ref[:, j * HEAD_DIM:(j + 1) * HEAD_DIM].astype(BF16) for j in range(nh)]
    biases = [bias_ref[hg * nh + j] for j in range(nh)]
    row = lax.broadcasted_iota(jnp.int32, (blk, blk), 0)
    col = lax.broadcasted_iota(jnp.int32, (blk, blk), 1)
    tri = jnp.where(row >= col, 1.0, 0.0).astype(BF16)
    causal = col < row

    def block(kb, state, masked):
        start = pl.multiple_of(kb * blk, blk)
        heads = range(nh)
        zs = [_dot_nt(qs[j], kb_ref[j, pl.ds(start, blk), :]) * QK_SCALE + biases[j] for j in heads]
        log_keeps = [-_softplus_fast(z) for z in zs]
        if masked:
            log_keeps = [jnp.where(causal, lk, 0.0) for lk in log_keeps]
        splits = [_split_bf16(lk) for lk in log_keeps]
        suffixes = [_dot(hi, tri) + _dot(lo, tri) for hi, lo in splits]
        ws = [jnp.exp(zs[j] + suffixes[j] + state[j][0]) for j in heads]
        if masked:
            ws = [jnp.where(causal, w, 0.0) for w in ws]
        pvs = [_dot(ws[j].astype(BF16), vb_ref[j, pl.ds(start, blk), :]) for j in heads]
        return tuple((state[j][0] + suffixes[j][:, 0:1], state[j][1] + pvs[j]) for j in heads)

    zero = (jnp.zeros((blk, 1), F32), jnp.zeros((blk, HEAD_DIM), F32))
    state = block(qi, (zero,) * nh, True)
    state = lax.fori_loop(0, qi, lambda it, st: block(qi - 1 - it, st, False), state)
    for j in range(nh):
        o_ref[:, j * HEAD_DIM:(j + 1) * HEAD_DIM] = state[j][1].astype(o_ref.dtype)


def _sb_prompt(q, k, v, sb_bias):
    bsz, heads, t, _ = k.shape
    blk = SB_BLK
    nh = SB_HEADS_PER_STEP
    assert t % blk == 0 and heads % nh == 0
    nq = t // blk
    kv_spec = pl.BlockSpec((None, nh, t, HEAD_DIM), lambda b, h, i: (b, h, 0, 0))
    return pl.pallas_call(
        _sb_prompt_kernel,
        out_shape=jax.ShapeDtypeStruct((bsz * t, SB_W), BF16),
        grid=(bsz, heads // nh, nq),
        in_specs=[
            pl.BlockSpec(memory_space=pltpu.SMEM),
            pl.BlockSpec((blk, nh * HEAD_DIM), lambda b, h, i: (b * nq + i, h)),
            kv_spec, kv_spec,
        ],
        out_specs=pl.BlockSpec((blk, nh * HEAD_DIM), lambda b, h, i: (b * nq + i, h)),
        scratch_shapes=[pltpu.VMEM((nh, t, HEAD_DIM), BF16), pltpu.VMEM((nh, t, HEAD_DIM), BF16)],
        compiler_params=_cparams(("parallel", "parallel", "arbitrary"), 40),
        name="sb_prompt",
    )(sb_bias, q, k, v)


def _sb_sample_kernel(pt_ref, bias_ref, q_ref, *refs, n_pages):
    del pt_ref
    k_refs = refs[:n_pages]
    v_refs = refs[n_pages:2 * n_pages]
    o_ref = refs[2 * n_pages]
    heads, page = k_refs[0].shape[:2]
    rep = 8
    n = n_pages * rep

    row = lax.broadcasted_iota(jnp.int32, (page, page), 0)
    col = lax.broadcasted_iota(jnp.int32, (page, page), 1)
    tri = jnp.where(row >= col, 1.0, 0.0).astype(BF16)
    rn = lax.broadcasted_iota(jnp.int32, (n, n), 0)
    cn = lax.broadcasted_iota(jnp.int32, (n, n), 1)
    later = jnp.where(((rn % rep) == (cn % rep)) & (cn // rep > rn // rep), 1.0, 0.0).astype(BF16)
    q = q_ref[0]
    zs = []
    for h in range(heads):
        q8 = jnp.broadcast_to(q[h:h + 1, :], (rep, HEAD_DIM)).astype(BF16)
        z = jnp.concatenate([_dot_nt(q8, k_refs[p][h].astype(BF16)) for p in range(n_pages)], axis=0)
        zs.append(z * QK_SCALE + bias_ref[h])
    z = jnp.concatenate(zs, axis=0)
    hi, lo = _split_bf16(-_softplus_fast(z))
    suffix = _dot(hi, tri) + _dot(lo, tri)
    shi, slo = _split_bf16(suffix)
    carry = jnp.concatenate(
        [(_dot(later, shi[h * n:(h + 1) * n]) + _dot(later, slo[h * n:(h + 1) * n]))[:, 0:1] for h in range(heads)],
        axis=0)
    w = jnp.exp(z + suffix + carry).astype(BF16)
    outs = []
    for h in range(heads):
        acc = jnp.zeros((rep, HEAD_DIM), F32)
        for p in range(n_pages):
            r0 = h * n + p * rep
            acc = acc + _dot(w[r0:r0 + rep, :], v_refs[p][h].astype(BF16))
        outs.append(acc[0:1, :])
    o_ref[0] = jnp.concatenate(outs, axis=0).astype(o_ref.dtype)


def _sb_sample(q, sb_bias, cache_k, cache_v, page_table):
    bs, n_pages = page_table.shape
    heads, page = cache_k.shape[1:3]
    assert heads == SB_HEADS and cache_k.shape[3] == HEAD_DIM

    def page_spec(p):
        return pl.BlockSpec((None, heads, page, HEAD_DIM), lambda b, pt: (pt[b, p], 0, 0, 0))

    grid_spec = pltpu.PrefetchScalarGridSpec(
        num_scalar_prefetch=1,
        grid=(bs,),
        in_specs=[pl.BlockSpec(memory_space=pltpu.SMEM),
                  pl.BlockSpec((1, heads, HEAD_DIM), lambda b, pt: (b, 0, 0))]
                 + [page_spec(p) for p in range(n_pages)] * 2,
        out_specs=pl.BlockSpec((1, heads, HEAD_DIM), lambda b, pt: (b, 0, 0)),
    )
    out = pl.pallas_call(
        functools.partial(_sb_sample_kernel, n_pages=n_pages),
        out_shape=jax.ShapeDtypeStruct((bs, heads, HEAD_DIM), BF16),
        grid_spec=grid_spec,
        compiler_params=_cparams(("arbitrary",), 40),
        name="sb_sample",
    )(page_table, sb_bias, q.reshape(bs, heads, HEAD_DIM), *([cache_k] * n_pages), *([cache_v] * n_pages))
    return out.reshape(bs, SB_W)


def _lru_gates(xc, wa_ref, ba, wx_ref, bx, nsp_lambda):
    xb = xc.astype(BF16)
    r_parts, i_parts = [], []
    for h in range(LRU_HEADS):
        xh = xb[:, h * HEAD_DIM:(h + 1) * HEAD_DIM]
        r_parts.append(_dot(xh, wa_ref[h].astype(BF16)))
        i_parts.append(_dot(xh, wx_ref[h].astype(BF16)))
    r = _sigmoid(jnp.concatenate(r_parts, axis=1) + ba)
    i = _sigmoid(jnp.concatenate(i_parts, axis=1) + bx)
    log_a = -LRU_C * r * nsp_lambda
    a = jnp.exp(log_a)
    u = jnp.sqrt(1.0 - jnp.exp(2.0 * log_a)) * (i * xc)
    return a, u


def _gelu_tanh(x):
    return 0.5 * x * (1.0 + jnp.tanh(0.7978845608028654 * (x + 0.044715 * (x * x * x))))


def _lru_prompt_kernel(x_ref, gate_ref, prev_ref, h0_ref, cw_ref, cb_ref, wa_ref, ba_ref, wx_ref, bx_ref,
                       lam_ref, o_ref, hlast_ref, xp_ref, h_ref):
    ti = pl.program_id(1)
    tt = x_ref.shape[0]
    pad = 8

    @pl.when(ti == 0)
    def _():
        xp_ref[pad - (CONV_W - 1):pad, :] = prev_ref[0]
        h_ref[...] = h0_ref[0]

    x = x_ref[...]
    xp_ref[pad:pad + tt, :] = x
    cw = cw_ref[...]
    xc = cb_ref[...] + cw[CONV_W - 1:CONV_W, :] * x
    for j in range(CONV_W - 1):
        xc = xc + cw[j:j + 1, :] * xp_ref[pad - (CONV_W - 1) + j:pad - (CONV_W - 1) + j + tt, :]
    xp_ref[pad - (CONV_W - 1):pad, :] = x[tt - (CONV_W - 1):tt, :]

    nsp = _softplus(-lam_ref[...])
    a, u = _lru_gates(xc, wa_ref, ba_ref[...], wx_ref, bx_ref[...], nsp)
    rows = lax.broadcasted_iota(jnp.int32, (tt, LRU_W), 0)
    b = jnp.where(rows == 0, u + a * h_ref[...], u)
    s = 1
    while s < tt:
        keep = rows >= s
        b = jnp.where(keep, a * pltpu.roll(b, s, 0) + b, b)
        if 2 * s < tt:
            a = jnp.where(keep, a * pltpu.roll(a, s, 0), a)
        s *= 2
    h_ref[...] = b[tt - 1:tt, :]
    o_ref[...] = (b * _gelu_tanh(gate_ref[...])).astype(o_ref.dtype)

    @pl.when(ti == pl.num_programs(1) - 1)
    def _():
        hlast_ref[0] = b[tt - 1:tt, :]


def _lru_prompt(proj, conv_prev, h0, conv_w, conv_b, lru_wa, lru_ba, lru_wx, lru_bx, lru_lambda, bsz, t):
    tt = LRU_TT
    assert t % tt == 0
    nt = t // tt
    xl, gl = 0, 1
    full = lambda shape: pl.BlockSpec(shape, lambda b, i: (0,) * len(shape))
    o_lru, h_last = pl.pallas_call(
        _lru_prompt_kernel,
        out_shape=(jax.ShapeDtypeStruct((bsz * t, LRU_W), BF16),
                   jax.ShapeDtypeStruct((bsz, 1, LRU_W), F32)),
        grid=(bsz, nt),
        in_specs=[
            pl.BlockSpec((tt, LRU_W), lambda b, i: (b * nt + i, xl)),
            pl.BlockSpec((tt, LRU_W), lambda b, i: (b * nt + i, gl)),
            pl.BlockSpec((1, CONV_W - 1, LRU_W), lambda b, i: (b, 0, 0)),
            pl.BlockSpec((1, 1, LRU_W), lambda b, i: (b, 0, 0)),
            full((CONV_W, LRU_W)), full((1, LRU_W)),
            full((LRU_HEADS, HEAD_DIM, HEAD_DIM)), full((1, LRU_W)),
            full((LRU_HEADS, HEAD_DIM, HEAD_DIM)), full((1, LRU_W)),
            full((1, LRU_W)),
        ],
        out_specs=(pl.BlockSpec((tt, LRU_W), lambda b, i: (b * nt + i, 0)),
                   pl.BlockSpec((1, 1, LRU_W), lambda b, i: (b, 0, 0))),
        scratch_shapes=[pltpu.VMEM((8 + tt, LRU_W), F32), pltpu.VMEM((1, LRU_W), F32)],
        compiler_params=_cparams(("parallel", "arbitrary"), 32),
        name="lru_prompt",
    )(proj, proj, conv_prev, h0.reshape(bsz, 1, LRU_W), conv_w, conv_b.reshape(1, LRU_W),
      lru_wa, lru_ba.reshape(1, LRU_W), lru_wx, lru_bx.reshape(1, LRU_W), lru_lambda.reshape(1, LRU_W))
    return o_lru, h_last.reshape(bsz, LRU_W)


def _lru_step_kernel(x_ref, gate_ref, p0_ref, p1_ref, p2_ref, h0_ref, cw_ref, cb_ref, wa_ref, ba_ref, wx_ref,
                     bx_ref, lam_ref, o_ref, h_ref):
    cw = cw_ref[...]
    xc = (cb_ref[...] + cw[0:1, :] * p0_ref[...] + cw[1:2, :] * p1_ref[...] + cw[2:3, :] * p2_ref[...]
          + cw[3:4, :] * x_ref[...])
    nsp = _softplus(-lam_ref[...])
    a, u = _lru_gates(xc, wa_ref, ba_ref[...], wx_ref, bx_ref[...], nsp)
    h = u + a * h0_ref[...]
    h_ref[...] = h
    o_ref[...] = (h * _gelu_tanh(gate_ref[...])).astype(o_ref.dtype)


def _lru_step(x_lru, gate_lru, state_conv, h0, conv_w, conv_b, lru_wa, lru_ba, lru_wx, lru_bx, lru_lambda):
    bs = x_lru.shape[0]
    assert CONV_W == 4
    prevs = [state_conv[:, j, :] for j in range(CONV_W - 1)]
    o_lru, h_new = pl.pallas_call(
        _lru_step_kernel,
        out_shape=(jax.ShapeDtypeStruct((bs, LRU_W), BF16), jax.ShapeDtypeStruct((bs, LRU_W), F32)),
        name="lru_step",
    )(x_lru, gate_lru, *prevs, h0, conv_w, conv_b.reshape(1, LRU_W), lru_wa, lru_ba.reshape(1, LRU_W),
      lru_wx, lru_bx.reshape(1, LRU_W), lru_lambda.reshape(1, LRU_W))
    return o_lru, h_new


def _mem_prompt_kernel(q_ref, k_ref, v_ref, o_ref):
    s = _dot_nt(q_ref[...].astype(BF16), k_ref[...].astype(BF16)) * QK_SCALE
    m = jnp.max(s, axis=-1, keepdims=True)
    e = jnp.exp(s - m)
    p = e / jnp.sum(e, axis=-1, keepdims=True)
    o_ref[...] = _dot(p.astype(BF16), v_ref[...].astype(BF16)).astype(o_ref.dtype)


def _mem_prompt(proj, qcol0, mem_k, mem_v, t):
    bsz, _, n_mem, _ = mem_k.shape
    tq = min(t, 512)
    assert t % tq == 0 and qcol0 % HEAD_DIM == 0
    nt = t // tq
    qc = qcol0 // HEAD_DIM
    kv_spec = pl.BlockSpec((None, None, n_mem, HEAD_DIM), lambda b, h, i: (b, h, 0, 0))
    return pl.pallas_call(
        _mem_prompt_kernel,
        out_shape=jax.ShapeDtypeStruct((bsz * t, MEM_W), BF16),
        grid=(bsz, MEM_HEADS, nt),
        in_specs=[pl.BlockSpec((tq, HEAD_DIM), lambda b, h, i: (b * nt + i, qc + h)), kv_spec, kv_spec],
        out_specs=pl.BlockSpec((tq, HEAD_DIM), lambda b, h, i: (b * nt + i, h)),
        compiler_params=_cparams(("parallel", "parallel", "parallel"), 32),
        name="mem_prompt",
    )(proj, mem_k, mem_v)


def _mem_sample_kernel(q_ref, k_ref, v_ref, o_ref):
    group = q_ref.shape[0]
    for g in range(group):
        q = q_ref[g]
        s = jnp.sum(k_ref[g] * q[None], axis=-1, keepdims=True) * QK_SCALE
        m = jnp.max(s, axis=0, keepdims=True)
        e = jnp.exp(s - m)
        p = e / jnp.sum(e, axis=0, keepdims=True)
        o_ref[g] = jnp.sum(p * v_ref[g], axis=0).astype(o_ref.dtype)


def _mem_sample(q, cache_k, cache_v):
    bs, n_mem, heads, _ = cache_k.shape
    assert heads == MEM_HEADS and cache_k.shape[3] == HEAD_DIM
    group = 8
    assert bs % group == 0
    kv_spec = pl.BlockSpec((group, n_mem, heads, HEAD_DIM), lambda i: (i, 0, 0, 0))
    out = pl.pallas_call(
        _mem_sample_kernel,
        out_shape=jax.ShapeDtypeStruct((bs, heads, HEAD_DIM), BF16),
        grid=(bs // group,),
        in_specs=[pl.BlockSpec((group, heads, HEAD_DIM), lambda i: (i, 0, 0)), kv_spec, kv_spec],
        out_specs=pl.BlockSpec((group, heads, HEAD_DIM), lambda i: (i, 0, 0)),
        compiler_params=_cparams(("parallel",), 32),
        name="mem_sample",
    )(q.reshape(bs, heads, HEAD_DIM), cache_k, cache_v)
    return out.reshape(bs, MEM_W)


def _merge_kernel(oa_ref, ol_ref, om_ref, pa_ref, pl_ref, pm_ref, ga_ref, gl_ref, gm_ref, o_ref):
    ya = _dot(oa_ref[...], pa_ref[...].astype(BF16))
    yl = _dot(ol_ref[...], pl_ref[...].astype(BF16))
    ym = _dot(om_ref[...], pm_ref[...].astype(BF16))
    merged = _sigmoid(ga_ref[...]) * ya + _sigmoid(gl_ref[...]) * yl + _sigmoid(gm_ref[...]) * ym
    o_ref[...] = merged.astype(o_ref.dtype)


def _merge(o_sb, o_lru, o_mem, p_attn, p_lru, p_mem, gates, tm):
    r = o_sb.shape[0]
    d = p_attn.shape[1]
    tn = 512
    assert r % tm == 0 and d % tn == 0
    nj = d // tn
    return pl.pallas_call(
        _merge_kernel,
        out_shape=jax.ShapeDtypeStruct((r, d), BF16),
        grid=(r // tm, nj),
        in_specs=[
            pl.BlockSpec((tm, SB_W), lambda i, j: (i, 0)),
            pl.BlockSpec((tm, LRU_W), lambda i, j: (i, 0)),
            pl.BlockSpec((tm, MEM_W), lambda i, j: (i, 0)),
            pl.BlockSpec((SB_W, tn), lambda i, j: (0, j)),
            pl.BlockSpec((LRU_W, tn), lambda i, j: (0, j)),
            pl.BlockSpec((MEM_W, tn), lambda i, j: (0, j)),
            pl.BlockSpec((tm, tn), lambda i, j: (i, j)),
            pl.BlockSpec((tm, tn), lambda i, j: (i, nj + j)),
            pl.BlockSpec((tm, tn), lambda i, j: (i, 2 * nj + j)),
        ],
        out_specs=pl.BlockSpec((tm, tn), lambda i, j: (i, j)),
        compiler_params=_cparams(("parallel", "parallel"), 44),
        name="merge",
    )(o_sb, o_lru, o_mem, p_attn, p_lru, p_mem, gates, gates, gates)


def _post_kernel(x_ref, m_ref, wo_ref, g_ref, wr_ref, br_ref, cnt0_ref,
                 hres_ref, hnp_ref, idx_ref, gate_ref, rank_ref, cnt_ref):
    i = pl.program_id(0)
    tm, d = x_ref.shape

    @pl.when(i == 0)
    def _():
        cnt_ref[...] = cnt0_ref[...]

    hres = x_ref[...] + _dot(m_ref[...], wo_ref[...])
    hres_ref[...] = hres
    ms = jnp.mean(hres * hres, axis=-1, keepdims=True)
    hn = hres * lax.rsqrt(ms + EPS) * g_ref[...]
    hb = hn.astype(BF16)
    wa = lax.bitcast_convert_type(hb[:, :d // 2].astype(F32), jnp.uint32)
    wb = lax.bitcast_convert_type(hb[:, d // 2:].astype(F32), jnp.uint32)
    hnp_ref[...] = wa | (wb >> 16)

    logits = _dot_nt(wr_ref[...], hb) + br_ref[...]
    ne = logits.shape[0]
    eid = lax.broadcasted_iota(jnp.int32, (ne, tm), 0)
    work = logits
    vals, idxs, onehots = [], [], []
    for _ in range(TOP_K):
        mx = jnp.max(work, axis=0, keepdims=True)
        sel = jnp.min(jnp.where(work == mx, eid, ne), axis=0, keepdims=True)
        oh = eid == sel
        vals.append(mx)
        idxs.append(sel)
        onehots.append(oh)
        work = jnp.where(oh, -jnp.inf, work)
    es = [jnp.exp(v - vals[0]) for v in vals]
    den = functools.reduce(jnp.add, es)
    gate_ref[...] = jnp.concatenate([e / den for e in es], axis=0)
    idx_ref[...] = jnp.concatenate(idxs, axis=0)

    chosen = functools.reduce(jnp.logical_or, onehots)
    r_ = lax.broadcasted_iota(jnp.int32, (tm, tm), 0)
    c_ = lax.broadcasted_iota(jnp.int32, (tm, tm), 1)
    before = jnp.where(r_ < c_, 1.0, 0.0).astype(BF16)
    chosen_f = jnp.where(chosen, 1.0, 0.0)
    prior = _dot(chosen_f.astype(BF16), before) + cnt_ref[...]
    rank_ref[...] = jnp.concatenate(
        [jnp.sum(jnp.where(oh, prior, 0.0), axis=0, keepdims=True) for oh in onehots], axis=0).astype(jnp.int32)
    cnt_ref[...] = cnt_ref[...] + jnp.sum(chosen_f, axis=1, keepdims=True)


def _post(x, merged, wo_b, g, wr_t, b_router, cnt0, tm):
    r, d = x.shape
    ne = wr_t.shape[0]
    assert r % tm == 0
    full = lambda shape: pl.BlockSpec(shape, lambda i: (0,) * len(shape))
    return pl.pallas_call(
        _post_kernel,
        out_shape=(jax.ShapeDtypeStruct((r, d), F32),
                   jax.ShapeDtypeStruct((r, d // 2), jnp.uint32),
                   jax.ShapeDtypeStruct((TOP_K, r), jnp.int32),
                   jax.ShapeDtypeStruct((TOP_K, r), F32),
                   jax.ShapeDtypeStruct((TOP_K, r), jnp.int32),
                   jax.ShapeDtypeStruct((ne, 1), F32)),
        grid=(r // tm,),
        in_specs=[pl.BlockSpec((tm, d), lambda i: (i, 0)),
                  pl.BlockSpec((tm, d), lambda i: (i, 0)),
                  full((d, d)), full((1, d)), full((ne, d)), full((ne, 1)), full((ne, 1))],
        out_specs=(pl.BlockSpec((tm, d), lambda i: (i, 0)),
                   pl.BlockSpec((tm, d // 2), lambda i: (i, 0)),
                   pl.BlockSpec((TOP_K, tm), lambda i: (0, i)),
                   pl.BlockSpec((TOP_K, tm), lambda i: (0, i)),
                   pl.BlockSpec((TOP_K, tm), lambda i: (0, i)),
                   full((ne, 1))),
        compiler_params=_cparams(("arbitrary",), 48),
        name="post",
    )(x, merged, wo_b, g.reshape(1, d), wr_t, b_router.reshape(ne, 1), cnt0)


def _dispatch_kernel(tail_ref, dest_ref, hn_ref, xs_ref, zero_ref, sem, zsem):
    i = pl.program_id(0)
    tm = dest_ref.shape[2] // TOP_K

    @pl.when(i == 0)
    def _():
        zero_ref[...] = jnp.zeros_like(zero_ref)

        def tail_copy(e):
            start = pl.multiple_of(tail_ref[e], MOE_SUB)
            return pltpu.make_async_copy(zero_ref, xs_ref.at[pl.ds(start, MOE_SUB)], zsem)

        def zstart(e, c):
            pl.when(tail_ref[e] >= 0)(lambda: tail_copy(e).start())
            return c

        def zwait(e, c):
            pl.when(tail_ref[e] >= 0)(lambda: tail_copy(e).wait())
            return c

        lax.fori_loop(0, tail_ref.shape[0], zstart, 0)
        lax.fori_loop(0, tail_ref.shape[0], zwait, 0)

    def row_copy(t, k):
        return pltpu.make_async_copy(hn_ref.at[pl.ds(i * tm + t, 1)],
                                     xs_ref.at[pl.ds(dest_ref[0, 0, k * tm + t], 1)], sem)

    def start(t, c):
        for k in range(TOP_K):
            row_copy(t, k).start()
        return c

    def wait(t, c):
        for k in range(TOP_K):
            row_copy(t, k).wait()
        return c

    def wait_tile():
        lax.fori_loop(0, tm, wait, 0)

    lax.fori_loop(0, tm, start, 0)
    pl.when(i > 0)(wait_tile)
    pl.when(i == pl.num_programs(0) - 1)(wait_tile)


def _dispatch(hnp, dest, tail, n_rows, tm):
    r, w = hnp.shape
    assert r % tm == 0
    nt = r // tm
    dest_t = dest.reshape(TOP_K, nt, tm).transpose(1, 0, 2).reshape(nt, 1, TOP_K * tm)
    grid_spec = pltpu.PrefetchScalarGridSpec(
        num_scalar_prefetch=1,
        grid=(nt,),
        in_specs=[pl.BlockSpec((1, 1, TOP_K * tm), lambda i, tl: (i, 0, 0), memory_space=pltpu.SMEM),
                  pl.BlockSpec(memory_space=pl.ANY)],
        out_specs=pl.BlockSpec(memory_space=pl.ANY),
        scratch_shapes=[pltpu.VMEM((MOE_SUB, w), hnp.dtype), pltpu.SemaphoreType.DMA, pltpu.SemaphoreType.DMA],
    )
    return pl.pallas_call(
        _dispatch_kernel,
        out_shape=jax.ShapeDtypeStruct((n_rows, w), hnp.dtype),
        grid_spec=grid_spec,
        compiler_params=_cparams(("arbitrary",), 16),
        name="dispatch",
    )(tail, dest_t, hnp)


def _moe_kernel(ce_ref, cb_ref, ns_ref, xs_ref, wg_ref, wu_ref, bg_ref, bu_ref, wd_ref, bd_ref, o_ref,
                xa_ref, xb_ref, act_ref, wgb_ref, wub_ref, wdb_ref, *, nf):
    del ce_ref, cb_ref
    c = pl.program_id(0)
    s = pl.program_id(1)
    nsub = ns_ref[c]
    half = xa_ref.shape[1]

    def for_row_blocks(fn):
        done = 0
        for size in MOE_BLOCK_SUBS:
            count = (nsub - done) // size
            rows = size * MOE_SUB

            def body(i, carry, done=done, rows=rows):
                fn(pl.ds(pl.multiple_of(done * MOE_SUB + i * rows, MOE_SUB), rows))
                return carry

            lax.fori_loop(0, count, body, 0)
            done = done + count * size

    @pl.when((s == 0) & (nsub > 0))
    def _():
        def unpack(rows):
            word = xs_ref[rows, :]
            xa_ref[rows, :] = lax.bitcast_convert_type(word & jnp.uint32(0xFFFF0000), F32).astype(BF16)
            xb_ref[rows, :] = lax.bitcast_convert_type(word << 16, F32).astype(BF16)
        for_row_blocks(unpack)

    @pl.when((s < nf) & (nsub > 0))
    def _():
        wgb_ref[...] = wg_ref[...].astype(BF16)
        wub_ref[...] = wu_ref[...].astype(BF16)
        bg = bg_ref[...]
        bu = bu_ref[...]

        def up(rows):
            xa = xa_ref[rows, :]
            xb = xb_ref[rows, :]
            gate = _dot(xa, wgb_ref[:half, :]) + _dot(xb, wgb_ref[half:, :]) + bg
            upv = _dot(xa, wub_ref[:half, :]) + _dot(xb, wub_ref[half:, :]) + bu
            gate = jnp.minimum(gate, SWIGLU_LIMIT)
            upv = jnp.clip(upv, -SWIGLU_LIMIT, SWIGLU_LIMIT)
            act = gate * _sigmoid(SWIGLU_ALPHA * gate) * (upv + 1.0)
            act_ref[s, rows, :] = act.astype(BF16)
        for_row_blocks(up)

    @pl.when((s >= nf) & (nsub > 0))
    def _():
        wdb_ref[...] = wd_ref[...].astype(BF16)
        bd = bd_ref[...]
        tf = act_ref.shape[2]

        def down(rows):
            y = _dot(act_ref[0, rows, :], wdb_ref[0:tf, :])
            for f in range(1, nf):
                y = y + _dot(act_ref[f, rows, :], wdb_ref[f * tf:(f + 1) * tf, :])
            o_ref[rows, :] = y + bd
        for_row_blocks(down)


def _moe(xs, chunk_expert, chunk_block, chunk_nsub, w_up, b_up, w_down, b_down):
    ne, d, dff2 = w_up.shape
    dff = dff2 // 2
    rb = MOE_SUB * MOE_CHUNK_SUBS
    n_chunks = xs.shape[0] // rb
    tf = MOE_TF
    nf = dff // tf
    nn = d // tf
    assert dff % tf == 0 and d % tf == 0

    def up_f(s, ns_c):
        return jnp.where(ns_c > 0, jnp.minimum(s, nf - 1), nf - 1)

    def dn_n(s, ns_c):
        return jnp.where(ns_c > 0, jnp.maximum(s - nf, 0), nn - 1)

    grid_spec = pltpu.PrefetchScalarGridSpec(
        num_scalar_prefetch=3,
        grid=(n_chunks, nf + nn),
        in_specs=[
            pl.BlockSpec((rb, d // 2), lambda c, s, ce, cb, ns: (cb[c], 0)),
            pl.BlockSpec((None, d, tf), lambda c, s, ce, cb, ns: (ce[c], 0, up_f(s, ns[c]))),
            pl.BlockSpec((None, d, tf), lambda c, s, ce, cb, ns: (ce[c], 0, nf + up_f(s, ns[c]))),
            pl.BlockSpec((None, 1, tf), lambda c, s, ce, cb, ns: (ce[c], 0, up_f(s, ns[c]))),
            pl.BlockSpec((None, 1, tf), lambda c, s, ce, cb, ns: (ce[c], 0, nf + up_f(s, ns[c]))),
            pl.BlockSpec((None, dff, tf), lambda c, s, ce, cb, ns: (ce[c], 0, dn_n(s, ns[c]))),
            pl.BlockSpec((None, 1, tf), lambda c, s, ce, cb, ns: (ce[c], 0, dn_n(s, ns[c]))),
        ],
        out_specs=pl.BlockSpec((rb, tf), lambda c, s, ce, cb, ns: (cb[c], dn_n(s, ns[c]))),
        scratch_shapes=[pltpu.VMEM((rb, d // 2), BF16), pltpu.VMEM((rb, d // 2), BF16),
                        pltpu.VMEM((nf, rb, tf), BF16),
                        pltpu.VMEM((d, tf), BF16), pltpu.VMEM((d, tf), BF16), pltpu.VMEM((dff, tf), BF16)],
    )
    return pl.pallas_call(
        functools.partial(_moe_kernel, nf=nf),
        out_shape=jax.ShapeDtypeStruct((n_chunks * rb, d), F32),
        grid_spec=grid_spec,
        compiler_params=_cparams(("arbitrary", "arbitrary"), 52),
        name="moe",
    )(chunk_expert, chunk_block, chunk_nsub, xs, w_up, w_up, b_up.reshape(ne, 1, dff2), b_up.reshape(ne, 1, dff2),
      w_down, b_down.reshape(ne, 1, d))


def _combine_kernel(dest_ref, dest_next_ref, hres_ref, gate_ref, yb_ref, o_ref, buf_ref, sem):
    i = pl.program_id(0)
    tm = hres_ref.shape[0]
    slot = i % 2

    def row_copy(dref, s, t, k):
        return pltpu.make_async_copy(yb_ref.at[pl.ds(dref[0, 0, k * tm + t], 1)],
                                     buf_ref.at[s, k, pl.ds(t, 1)], sem.at[s])

    def issue(dref, s):
        def body(t, c):
            for k in range(TOP_K):
                row_copy(dref, s, t, k).start()
            return c
        lax.fori_loop(0, tm, body, 0)

    def drain(dref, s):
        def body(t, c):
            for k in range(TOP_K):
                row_copy(dref, s, t, k).wait()
            return c
        lax.fori_loop(0, tm, body, 0)

    pl.when(i == 0)(lambda: issue(dest_ref, slot))
    pl.when(i + 1 < pl.num_programs(0))(lambda: issue(dest_next_ref, 1 - slot))
    drain(dest_ref, slot)
    g = gate_ref[...]
    y = hres_ref[...]
    for k in range(TOP_K):
        y = y + g[:, k:k + 1] * buf_ref[slot, k]
    o_ref[...] = y


def _combine(hres, gates, dest, yb, tm):
    r, d = hres.shape
    assert r % tm == 0
    nt = r // tm
    dest_t = dest.reshape(TOP_K, nt, tm).transpose(1, 0, 2).reshape(nt, 1, TOP_K * tm)
    return pl.pallas_call(
        _combine_kernel,
        out_shape=jax.ShapeDtypeStruct((r, d), F32),
        grid=(nt,),
        in_specs=[pl.BlockSpec((1, 1, TOP_K * tm), lambda i: (i, 0, 0), memory_space=pltpu.SMEM),
                  pl.BlockSpec((1, 1, TOP_K * tm), lambda i: (jnp.minimum(i + 1, nt - 1), 0, 0),
                               memory_space=pltpu.SMEM),
                  pl.BlockSpec((tm, d), lambda i: (i, 0)),
                  pl.BlockSpec((tm, TOP_K), lambda i: (i, 0)),
                  pl.BlockSpec(memory_space=pl.ANY)],
        out_specs=pl.BlockSpec((tm, d), lambda i: (i, 0)),
        scratch_shapes=[pltpu.VMEM((2, TOP_K, tm, d), F32), pltpu.SemaphoreType.DMA((2,))],
        compiler_params=_cparams(("arbitrary",), 40),
        name="combine",
    )(dest_t, dest_t, hres, gates.T, yb)


def _row_tile(r, cap):
    tm = min(r, cap)
    assert r % tm == 0
    return tm


def kernel(x_prompt, x_sample, mem_prompt, cache_sb_k, cache_sb_v, page_table, cache_mem_k, cache_mem_v, state_conv, state_lru, norm_mix_g, norm_mem_g, w_in, q_sb_g, k_sb_g, sb_bias, q_mem_g, k_mem_g, w_mem_kv, conv_w, conv_b, lru_wa, lru_ba, lru_wx, lru_bx, lru_lambda, p_attn, p_lru, p_mem, w_o, norm_ffn_g, w_router, b_router, w_up, b_up, w_down, b_down):
    bp, t, d = x_prompt.shape
    bs, ts, _ = x_sample.shape
    assert ts == 1, "sample group decodes one token per sequence"
    n_mem = mem_prompt.shape[1]
    q_gain = jnp.tile(q_sb_g, SB_HEADS)
    k_gain = jnp.tile(k_sb_g, SB_HEADS)
    qm_gain = jnp.tile(q_mem_g, MEM_HEADS)
    heads_major = lambda a: jnp.transpose(a, (0, 2, 1, 3))

    memn = _rmsnorm(mem_prompt.reshape(bp * n_mem, d), norm_mem_g, _row_tile(bp * n_mem, 256))
    mem_tm = _row_tile(n_mem, 256)
    mem_k_hm = _proj(memn, w_mem_kv, jnp.tile(k_mem_g, MEM_HEADS), col0=0, ncols=MEM_W, norm_cols=(0, MEM_W),
                     tm=mem_tm, tn=MEM_W, heads_out=(bp, n_mem))
    mem_v_hm = _proj(memn, w_mem_kv, None, col0=MEM_W, ncols=MEM_W, tm=mem_tm, tn=MEM_W, heads_out=(bp, n_mem))

    rest_w = COL_GATE - COL_XL
    qm_cols = (COL_QM - COL_XL, rest_w)
    w_q = _cast_bf16(w_in, COL_Q, SB_W)
    w_k = _cast_bf16(w_in, COL_K, SB_W)
    w_v = _cast_bf16(w_in, COL_V, SB_W)
    w_rest = _cast_bf16(w_in, COL_XL, rest_w)
    w_gates = _cast_bf16(w_in, COL_GATE, N_BRANCH * d)
    rest_gain = jnp.concatenate([jnp.ones((2 * LRU_W,), F32), qm_gain])

    xp2 = x_prompt.reshape(bp * t, d)
    tm_p = _row_tile(t, 1024)
    xn_p = _rmsnorm(xp2, norm_mix_g, _row_tile(bp * t, 512))
    q_p = _proj(xn_p, w_q, q_gain, ncols=SB_W, norm_cols=(0, SB_W), tm=tm_p, tn=SB_W)
    k_p = _proj(xn_p, w_k, k_gain, ncols=SB_W, norm_cols=(0, SB_W), tm=tm_p, tn=SB_W, heads_out=(bp, t))
    v_p = _proj(xn_p, w_v, None, ncols=SB_W, tm=tm_p, tn=SB_W, heads_out=(bp, t))
    rest_p = _proj(xn_p, w_rest, rest_gain, ncols=rest_w, norm_cols=qm_cols, tm=tm_p, tn=1024)
    gates_p = _proj(xn_p, w_gates, None, ncols=N_BRANCH * d, tm=tm_p, tn=1024)
    o_sb_p = _sb_prompt(q_p, k_p, v_p, sb_bias)
    conv0 = jnp.zeros((bp, CONV_W - 1, LRU_W), x_prompt.dtype)
    lru0 = jnp.zeros((bp, LRU_W), state_lru.dtype)
    o_lru_p, lru_prompt = _lru_prompt(rest_p, conv0, lru0, conv_w, conv_b, lru_wa, lru_ba, lru_wx, lru_bx,
                                      lru_lambda, bp, t)
    o_mem_p = _mem_prompt(rest_p, COL_QM - COL_XL, mem_k_hm, mem_v_hm, t)
    xl_tail = rest_p.reshape(bp, t, -1)[:, t - (CONV_W - 1):, :LRU_W]
    conv_prompt = jnp.concatenate([conv0, xl_tail], axis=1)[:, -(CONV_W - 1):]

    xs2 = x_sample.reshape(bs, d)
    xn_s = _rmsnorm(xs2, norm_mix_g, bs)
    q_s = _proj(xn_s, w_q, q_gain, ncols=SB_W, norm_cols=(0, SB_W), tm=bs, tn=SB_W)
    k_s = _proj(xn_s, w_k, k_gain, ncols=SB_W, norm_cols=(0, SB_W), tm=bs, tn=SB_W)
    v_s = _proj(xn_s, w_v, None, ncols=SB_W, tm=bs, tn=SB_W)
    rest_s = _proj(xn_s, w_rest, rest_gain, ncols=rest_w, norm_cols=qm_cols, tm=bs, tn=1024)
    gates_s = _proj(xn_s, w_gates, None, ncols=N_BRANCH * d, tm=bs, tn=1024)
    o_sb_s = _sb_sample(q_s, sb_bias, heads_major(cache_sb_k), heads_major(cache_sb_v), page_table)
    xl_s = rest_s[:, :LRU_W]
    o_lru_s, lru_sample = _lru_step(xl_s, rest_s[:, LRU_W:2 * LRU_W], state_conv, state_lru, conv_w, conv_b,
                                    lru_wa, lru_ba, lru_wx, lru_bx, lru_lambda)
    conv_sample = jnp.concatenate([state_conv, xl_s[:, None, :]], axis=1)[:, -(CONV_W - 1):]
    o_mem_s = _mem_sample(rest_s[:, 2 * LRU_W:], cache_mem_k, cache_mem_v)

    wo_b = _cast_bf16(w_o)
    wr_t = w_router.T.astype(BF16)
    merged_p = _merge(o_sb_p, o_lru_p, o_mem_p, p_attn, p_lru, p_mem, gates_p, _row_tile(bp * t, 1024))
    merged_s = _merge(o_sb_s, o_lru_s, o_mem_s, p_attn, p_lru, p_mem, gates_s, _row_tile(bs, 1024))
    cnt0 = jnp.zeros((N_EXPERTS, 1), F32)
    hres_p, hnp_p, idx_p, gate_p, rank_p, cnt_p = _post(xp2, merged_p, wo_b, norm_ffn_g, wr_t, b_router,
                                                        cnt0, _row_tile(bp * t, 256))
    hres_s, hnp_s, idx_s, gate_s, rank_s, cnt_all = _post(xs2, merged_s, wo_b, norm_ffn_g, wr_t, b_router,
                                                          cnt_p, _row_tile(bs, 256))

    rb = MOE_SUB * MOE_CHUNK_SUBS
    n_assign = (bp * t + bs) * TOP_K
    n_chunks = -(-n_assign // rb) + N_EXPERTS
    counts = cnt_all[:, 0].astype(jnp.int32)
    chunks_e = (counts + rb - 1) // rb
    chunk_end = jnp.cumsum(chunks_e)
    chunk_start = chunk_end - chunks_e
    used = chunk_end[-1]
    cidx = jnp.arange(n_chunks, dtype=jnp.int32)
    last = jnp.minimum(cidx, used - 1)
    chunk_expert = jnp.clip(jnp.searchsorted(chunk_end, last, side='right'), 0, N_EXPERTS - 1).astype(jnp.int32)
    rows_left = counts[chunk_expert] - (last - chunk_start[chunk_expert]) * rb
    chunk_nsub = jnp.where(cidx < used, (jnp.clip(rows_left, 0, rb) + MOE_SUB - 1) // MOE_SUB, 0).astype(jnp.int32)
    chunk_block = last.astype(jnp.int32)
    row_start = chunk_start * rb
    experts = jnp.arange(N_EXPERTS, dtype=jnp.int32)

    def slots(idx, rank):
        return jnp.sum(jnp.where(idx[..., None] == experts, row_start, 0), axis=-1) + rank

    dest_p = slots(idx_p, rank_p)
    dest_s = slots(idx_s, rank_s)
    tail = jnp.where(counts % MOE_SUB != 0, row_start + counts // MOE_SUB * MOE_SUB, -1).astype(jnp.int32)

    hnp = jnp.concatenate([hnp_p, hnp_s], axis=0)
    dest = jnp.concatenate([dest_p, dest_s], axis=1)
    xs = _dispatch(hnp, dest, tail, n_chunks * rb, _row_tile(bp * t + bs, 128))
    yb = _moe(xs, chunk_expert, chunk_block, chunk_nsub, w_up, b_up, w_down, b_down)
    y_prompt = _combine(hres_p, gate_p, dest_p, yb, _row_tile(bp * t, 256)).reshape(bp, t, d)
    y_sample = _combine(hres_s, gate_s, dest_s, yb, _row_tile(bs, 256)).reshape(bs, 1, d)

    sb_k_sample = k_s.reshape(bs, 1, SB_HEADS, HEAD_DIM)
    sb_v_sample = v_s.reshape(bs, 1, SB_HEADS, HEAD_DIM)
    return (y_prompt, y_sample, heads_major(k_p), heads_major(v_p), heads_major(mem_k_hm), heads_major(mem_v_hm),
            conv_prompt, lru_prompt, sb_k_sample, sb_v_sample, conv_sample, lru_sample)
```

```python
import functools

import jax
import jax.numpy as jnp
from jax import lax
from jax.experimental import pallas as pl
from jax.experimental.pallas import tpu as pltpu

F32 = jnp.float32
BF16 = jnp.bfloat16

HEAD_DIM = 128
SB_HEADS = 6
SB_W = SB_HEADS * HEAD_DIM
LRU_HEADS = 6
LRU_W = LRU_HEADS * HEAD_DIM
LRU_C = 8.0
CONV_W = 4
MEM_HEADS = 4
MEM_W = MEM_HEADS * HEAD_DIM
N_BRANCH = 3
N_EXPERTS = 32
TOP_K = 4
SWIGLU_LIMIT = 7.0
SWIGLU_ALPHA = 1.702
EPS = 1e-6
QK_SCALE = HEAD_DIM ** -0.5

COL_Q, COL_K, COL_V = 0, SB_W, 2 * SB_W
COL_XL, COL_GL = 3 * SB_W, 3 * SB_W + LRU_W
COL_QM = 3 * SB_W + 2 * LRU_W
COL_GATE = COL_QM + MEM_W

MIB = 1024 * 1024
PROJ_TN = 256
SB_BLK = 256
SB_HEADS_PER_STEP = 6
LRU_TT = 256
MOE_SUB = 128
MOE_BLOCK_SUBS = (8, 2, 1)
MOE_CHUNK_SUBS = 10
MOE_TF = 256


def _cparams(sem, vmem_mib):
    return pltpu.CompilerParams(dimension_semantics=sem, vmem_limit_bytes=vmem_mib * MIB)


def _softplus(z):
    return jnp.maximum(z, 0.0) + jnp.log1p(jnp.exp(-jnp.abs(z)))


def _softplus_fast(z):
    return jnp.maximum(z, 0.0) + jnp.log(1.0 + jnp.exp(-jnp.abs(z)))


def _sigmoid(z):
    return 1.0 / (1.0 + jnp.exp(-z))


def _split_bf16(x):
    hi = x.astype(BF16)
    lo = (x - hi.astype(F32)).astype(BF16)
    return hi, lo


def _dot(a, b):
    return jnp.dot(a, b, preferred_element_type=F32)


def _dot_nt(a, b):
    return lax.dot_general(a, b, (((1,), (1,)), ((), ())), preferred_element_type=F32)


def _rmsnorm_kernel(x_ref, g_ref, o_ref):
    x = x_ref[...]
    ms = jnp.mean(x * x, axis=-1, keepdims=True)
    o_ref[...] = (x * lax.rsqrt(ms + EPS) * g_ref[...]).astype(o_ref.dtype)


def _rmsnorm(x, g, tm):
    r, d = x.shape
    assert r % tm == 0
    return pl.pallas_call(
        _rmsnorm_kernel,
        out_shape=jax.ShapeDtypeStruct((r, d), BF16),
        grid=(r // tm,),
        in_specs=[pl.BlockSpec((tm, d), lambda i: (i, 0)), pl.BlockSpec((1, d), lambda i: (0, 0))],
        out_specs=pl.BlockSpec((tm, d), lambda i: (i, 0)),
        compiler_params=_cparams(("parallel",), 32),
        name="rmsnorm",
    )(x, g.reshape(1, d))


def _cast_kernel(x_ref, o_ref):
    o_ref[...] = x_ref[...].astype(o_ref.dtype)


def _cast_bf16(w, col0=0, ncols=None):
    r, c = w.shape
    ncols = c - col0 if ncols is None else ncols
    tc = PROJ_TN
    assert col0 % tc == 0 and ncols % tc == 0
    c0 = col0 // tc
    return pl.pallas_call(
        _cast_kernel,
        out_shape=jax.ShapeDtypeStruct((r, ncols), BF16),
        grid=(ncols // tc,),
        in_specs=[pl.BlockSpec((r, tc), lambda j: (0, j + c0))],
        out_specs=pl.BlockSpec((r, tc), lambda j: (0, j)),
        compiler_params=_cparams(("parallel",), 32),
        name="cast_bf16",
    )(w)


def _proj_kernel(xn_ref, w_ref, gain_ref, *o_refs, patterns, heads_out):
    j = pl.program_id(1)
    y = _dot(xn_ref[...], w_ref[...].astype(BF16))
    nh = y.shape[1] // HEAD_DIM

    def store(c, val):
        for o_ref in o_refs:
            if heads_out:
                o_ref[c] = val.astype(o_ref.dtype)
            else:
                o_ref[:, c * HEAD_DIM:(c + 1) * HEAD_DIM] = val.astype(o_ref.dtype)

    def emit(pattern):
        gain = gain_ref[...] if any(pattern) else None
        for c in range(nh):
            yc = y[:, c * HEAD_DIM:(c + 1) * HEAD_DIM]
            if pattern[c]:
                ms = jnp.mean(yc * yc, axis=-1, keepdims=True)
                yc = yc * lax.rsqrt(ms + EPS) * gain[:, c * HEAD_DIM:(c + 1) * HEAD_DIM]
            store(c, yc)

    distinct = sorted(set(patterns))
    if len(distinct) == 1:
        emit(distinct[0])
    else:
        for pattern in distinct:
            tiles = [jj for jj, p in enumerate(patterns) if p == pattern]
            pred = functools.reduce(jnp.logical_or, [j == jj for jj in tiles])
            pl.when(pred)(functools.partial(emit, pattern))


def _proj(xn, w, gain, *, col0=0, ncols, norm_cols=(0, 0), tm, tn, heads_out=None, bf16_copy=False):
    r, d = xn.shape
    assert r % tm == 0 and col0 % tn == 0 and ncols % tn == 0 and tn % HEAD_DIM == 0
    assert norm_cols[0] % HEAD_DIM == 0 and norm_cols[1] % HEAD_DIM == 0
    col_tile0 = col0 // tn
    patterns = tuple(tuple(norm_cols[0] <= jj * tn + c * HEAD_DIM < norm_cols[1] for c in range(tn // HEAD_DIM))
                     for jj in range(ncols // tn))
    if gain is None:
        gain = jnp.ones((ncols,), F32)
    if heads_out is None:
        out_shape = jax.ShapeDtypeStruct((r, ncols), F32)
        out_spec = pl.BlockSpec((tm, tn), lambda i, j: (i, j))
    else:
        bsz, t = heads_out
        assert bsz * t == r and t % tm == 0
        nt = t // tm
        hpt = tn // HEAD_DIM
        out_shape = jax.ShapeDtypeStruct((bsz, ncols // HEAD_DIM, t, HEAD_DIM), F32)
        out_spec = pl.BlockSpec((None, hpt, tm, HEAD_DIM), lambda i, j: (i // nt, j, i % nt, 0))
    kern = functools.partial(_proj_kernel, patterns=patterns, heads_out=heads_out is not None)
    if bf16_copy:
        out_shape = (out_shape, jax.ShapeDtypeStruct(out_shape.shape, BF16))
        out_spec = (out_spec, out_spec)
    return pl.pallas_call(
        kern,
        out_shape=out_shape,
        grid=(r // tm, ncols // tn),
        in_specs=[
            pl.BlockSpec((tm, d), lambda i, j: (i, 0)),
            pl.BlockSpec((d, tn), lambda i, j: (0, j + col_tile0)),
            pl.BlockSpec((1, tn), lambda i, j: (0, j)),
        ],
        out_specs=out_spec,
        compiler_params=_cparams(("parallel", "parallel"), 40),
        name="proj",
    )(xn, w, gain.reshape(1, ncols))


def _sb_prompt_kernel(bias_ref, q_ref, kb_ref, vb_ref, o_ref):
    hg = pl.program_id(1)
    qi = pl.program_id(2)
    blk = q_ref.shape[0]
    nh = kb_ref.shape[0]

    qs = [q_ref[:, j * HEAD_DIM:(j + 1) * HEAD_DIM] for j in range(nh)]
    biases = [bias_ref[hg * nh + j] for j in range(nh)]
    row = lax.broadcasted_iota(jnp.int32, (blk, blk), 0)
    col = lax.broadcasted_iota(jnp.int32, (blk, blk), 1)
    tri = jnp.where(row >= col, 1.0, 0.0).astype(BF16)
    causal = col < row

    def block(kb, state, masked):
        start = pl.multiple_of(kb * blk, blk)
        heads = range(nh)
        zs = [_dot_nt(qs[j], kb_ref[j, pl.ds(start, blk), :]) * QK_SCALE + biases[j] for j in heads]
        log_keeps = [-_softplus_fast(z) for z in zs]
        if masked:
            log_keeps = [jnp.where(causal, lk, 0.0) for lk in log_keeps]
        splits = [_split_bf16(lk) for lk in log_keeps]
        suffixes = [_dot(hi, tri) + _dot(lo, tri) for hi, lo in splits]
        ws = [jnp.exp(zs[j] + suffixes[j] + state[j][0]) for j in heads]
        if masked:
            ws = [jnp.where(causal, w, 0.0) for w in ws]
        pvs = [_dot(ws[j].astype(BF16), vb_ref[j, pl.ds(start, blk), :]) for j in heads]
        return tuple((state[j][0] + suffixes[j][:, 0:1], state[j][1] + pvs[j]) for j in heads)

    zero = (jnp.zeros((blk, 1), F32), jnp.zeros((blk, HEAD_DIM), F32))
    state = block(qi, (zero,) * nh, True)
    state = lax.fori_loop(0, qi, lambda it, st: block(qi - 1 - it, st, False), state)
    for j in range(nh):
        o_ref[:, j * HEAD_DIM:(j + 1) * HEAD_DIM] = state[j][1].astype(o_ref.dtype)


def _sb_prompt(q, k, v, sb_bias):
    bsz, heads, t, _ = k.shape
    blk = SB_BLK
    nh = SB_HEADS_PER_STEP
    assert t % blk == 0 and heads % nh == 0
    assert q.dtype == BF16 and k.dtype == BF16 and v.dtype == BF16
    nq = t // blk
    kv_spec = pl.BlockSpec((None, nh, t, HEAD_DIM), lambda b, h, i: (b, h, 0, 0))
    return pl.pallas_call(
        _sb_prompt_kernel,
        out_shape=jax.ShapeDtypeStruct((bsz * t, SB_W), BF16),
        grid=(bsz, heads // nh, nq),
        in_specs=[
            pl.BlockSpec(memory_space=pltpu.SMEM),
            pl.BlockSpec((blk, nh * HEAD_DIM), lambda b, h, i: (b * nq + i, h)),
            kv_spec, kv_spec,
        ],
        out_specs=pl.BlockSpec((blk, nh * HEAD_DIM), lambda b, h, i: (b * nq + i, h)),
        compiler_params=_cparams(("parallel", "parallel", "parallel"), 40),
        name="sb_prompt",
    )(sb_bias, q, k, v)


def _sb_sample_kernel(pt_ref, bias_ref, q_ref, *refs, n_pages):
    del pt_ref
    k_refs = refs[:n_pages]
    v_refs = refs[n_pages:2 * n_pages]
    o_ref = refs[2 * n_pages]
    heads, page = k_refs[0].shape[:2]
    rep = 8
    n = n_pages * rep

    row = lax.broadcasted_iota(jnp.int32, (page, page), 0)
    col = lax.broadcasted_iota(jnp.int32, (page, page), 1)
    tri = jnp.where(row >= col, 1.0, 0.0).astype(BF16)
    rn = lax.broadcasted_iota(jnp.int32, (n, n), 0)
    cn = lax.broadcasted_iota(jnp.int32, (n, n), 1)
    later = jnp.where(((rn % rep) == (cn % rep)) & (cn // rep > rn // rep), 1.0, 0.0).astype(BF16)
    q = q_ref[0]
    zs = []
    for h in range(heads):
        q8 = jnp.broadcast_to(q[h:h + 1, :], (rep, HEAD_DIM)).astype(BF16)
        z = jnp.concatenate([_dot_nt(q8, k_refs[p][h].astype(BF16)) for p in range(n_pages)], axis=0)
        zs.append(z * QK_SCALE + bias_ref[h])
    z = jnp.concatenate(zs, axis=0)
    hi, lo = _split_bf16(-_softplus_fast(z))
    suffix = _dot(hi, tri) + _dot(lo, tri)
    shi, slo = _split_bf16(suffix)
    carry = jnp.concatenate(
        [(_dot(later, shi[h * n:(h + 1) * n]) + _dot(later, slo[h * n:(h + 1) * n]))[:, 0:1] for h in range(heads)],
        axis=0)
    w = jnp.exp(z + suffix + carry).astype(BF16)
    outs = []
    for h in range(heads):
        acc = jnp.zeros((rep, HEAD_DIM), F32)
        for p in range(n_pages):
            r0 = h * n + p * rep
            acc = acc + _dot(w[r0:r0 + rep, :], v_refs[p][h].astype(BF16))
        outs.append(acc[0:1, :])
    o_ref[0] = jnp.concatenate(outs, axis=0).astype(o_ref.dtype)


def _sb_sample(q, sb_bias, cache_k, cache_v, page_table):
    bs, n_pages = page_table.shape
    heads, page = cache_k.shape[1:3]
    assert heads == SB_HEADS and cache_k.shape[3] == HEAD_DIM

    def page_spec(p):
        return pl.BlockSpec((None, heads, page, HEAD_DIM), lambda b, pt: (pt[b, p], 0, 0, 0))

    grid_spec = pltpu.PrefetchScalarGridSpec(
        num_scalar_prefetch=1,
        grid=(bs,),
        in_specs=[pl.BlockSpec(memory_space=pltpu.SMEM),
                  pl.BlockSpec((1, heads, HEAD_DIM), lambda b, pt: (b, 0, 0))]
                 + [page_spec(p) for p in range(n_pages)] * 2,
        out_specs=pl.BlockSpec((1, heads, HEAD_DIM), lambda b, pt: (b, 0, 0)),
    )
    out = pl.pallas_call(
        functools.partial(_sb_sample_kernel, n_pages=n_pages),
        out_shape=jax.ShapeDtypeStruct((bs, heads, HEAD_DIM), BF16),
        grid_spec=grid_spec,
        compiler_params=_cparams(("arbitrary",), 40),
        name="sb_sample",
    )(page_table, sb_bias, q.reshape(bs, heads, HEAD_DIM), *([cache_k] * n_pages), *([cache_v] * n_pages))
    return out.reshape(bs, SB_W)


def _lru_gates(xc, wa_ref, ba, wx_ref, bx, nsp_lambda):
    xb = xc.astype(BF16)
    r_parts, i_parts = [], []
    for h in range(LRU_HEADS):
        xh = xb[:, h * HEAD_DIM:(h + 1) * HEAD_DIM]
        r_parts.append(_dot(xh, wa_ref[h].astype(BF16)))
        i_parts.append(_dot(xh, wx_ref[h].astype(BF16)))
    r = _sigmoid(jnp.concatenate(r_parts, axis=1) + ba)
    i = _sigmoid(jnp.concatenate(i_parts, axis=1) + bx)
    log_a = -LRU_C * r * nsp_lambda
    a = jnp.exp(log_a)
    u = jnp.sqrt(1.0 - jnp.exp(2.0 * log_a)) * (i * xc)
    return a, u


def _gelu_tanh(x):
    return 0.5 * x * (1.0 + jnp.tanh(0.7978845608028654 * (x + 0.044715 * (x * x * x))))


def _lru_prompt_kernel(x_ref, gate_ref, prev_ref, h0_ref, cw_ref, cb_ref, wa_ref, ba_ref, wx_ref, bx_ref,
                       lam_ref, o_ref, hlast_ref, xp_ref, h_ref):
    ti = pl.program_id(1)
    tt = x_ref.shape[0]
    pad = 8

    @pl.when(ti == 0)
    def _():
        xp_ref[pad - (CONV_W - 1):pad, :] = prev_ref[0]
        h_ref[...] = h0_ref[0]

    x = x_ref[...]
    xp_ref[pad:pad + tt, :] = x
    cw = cw_ref[...]
    xc = cb_ref[...] + cw[CONV_W - 1:CONV_W, :] * x
    for j in range(CONV_W - 1):
        xc = xc + cw[j:j + 1, :] * xp_ref[pad - (CONV_W - 1) + j:pad - (CONV_W - 1) + j + tt, :]
    xp_ref[pad - (CONV_W - 1):pad, :] = x[tt - (CONV_W - 1):tt, :]

    nsp = _softplus(-lam_ref[...])
    a, u = _lru_gates(xc, wa_ref, ba_ref[...], wx_ref, bx_ref[...], nsp)
    rows = lax.broadcasted_iota(jnp.int32, (tt, LRU_W), 0)
    b = jnp.where(rows == 0, u + a * h_ref[...], u)
    s = 1
    while s < tt:
        keep = rows >= s
        b = jnp.where(keep, a * pltpu.roll(b, s, 0) + b, b)
        if 2 * s < tt:
            a = jnp.where(keep, a * pltpu.roll(a, s, 0), a)
        s *= 2
    h_ref[...] = b[tt - 1:tt, :]
    o_ref[...] = (b * _gelu_tanh(gate_ref[...])).astype(o_ref.dtype)

    @pl.when(ti == pl.num_programs(1) - 1)
    def _():
        hlast_ref[0] = b[tt - 1:tt, :]


def _lru_prompt(proj, conv_prev, h0, conv_w, conv_b, lru_wa, lru_ba, lru_wx, lru_bx, lru_lambda, bsz, t):
    tt = LRU_TT
    assert t % tt == 0
    nt = t // tt
    xl, gl = 0, 1
    full = lambda shape: pl.BlockSpec(shape, lambda b, i: (0,) * len(shape))
    o_lru, h_last = pl.pallas_call(
        _lru_prompt_kernel,
        out_shape=(jax.ShapeDtypeStruct((bsz * t, LRU_W), BF16),
                   jax.ShapeDtypeStruct((bsz, 1, LRU_W), F32)),
        grid=(bsz, nt),
        in_specs=[
            pl.BlockSpec((tt, LRU_W), lambda b, i: (b * nt + i, xl)),
            pl.BlockSpec((tt, LRU_W), lambda b, i: (b * nt + i, gl)),
            pl.BlockSpec((1, CONV_W - 1, LRU_W), lambda b, i: (b, 0, 0)),
            pl.BlockSpec((1, 1, LRU_W), lambda b, i: (b, 0, 0)),
            full((CONV_W, LRU_W)), full((1, LRU_W)),
            full((LRU_HEADS, HEAD_DIM, HEAD_DIM)), full((1, LRU_W)),
            full((LRU_HEADS, HEAD_DIM, HEAD_DIM)), full((1, LRU_W)),
            full((1, LRU_W)),
        ],
        out_specs=(pl.BlockSpec((tt, LRU_W), lambda b, i: (b * nt + i, 0)),
                   pl.BlockSpec((1, 1, LRU_W), lambda b, i: (b, 0, 0))),
        scratch_shapes=[pltpu.VMEM((8 + tt, LRU_W), F32), pltpu.VMEM((1, LRU_W), F32)],
        compiler_params=_cparams(("parallel", "arbitrary"), 32),
        name="lru_prompt",
    )(proj, proj, conv_prev, h0.reshape(bsz, 1, LRU_W), conv_w, conv_b.reshape(1, LRU_W),
      lru_wa, lru_ba.reshape(1, LRU_W), lru_wx, lru_bx.reshape(1, LRU_W), lru_lambda.reshape(1, LRU_W))
    return o_lru, h_last.reshape(bsz, LRU_W)


def _lru_step_kernel(x_ref, gate_ref, p0_ref, p1_ref, p2_ref, h0_ref, cw_ref, cb_ref, wa_ref, ba_ref, wx_ref,
                     bx_ref, lam_ref, o_ref, h_ref):
    cw = cw_ref[...]
    xc = (cb_ref[...] + cw[0:1, :] * p0_ref[...] + cw[1:2, :] * p1_ref[...] + cw[2:3, :] * p2_ref[...]
          + cw[3:4, :] * x_ref[...])
    nsp = _softplus(-lam_ref[...])
    a, u = _lru_gates(xc, wa_ref, ba_ref[...], wx_ref, bx_ref[...], nsp)
    h = u + a * h0_ref[...]
    h_ref[...] = h
    o_ref[...] = (h * _gelu_tanh(gate_ref[...])).astype(o_ref.dtype)


def _lru_step(x_lru, gate_lru, state_conv, h0, conv_w, conv_b, lru_wa, lru_ba, lru_wx, lru_bx, lru_lambda):
    bs = x_lru.shape[0]
    assert CONV_W == 4
    prevs = [state_conv[:, j, :] for j in range(CONV_W - 1)]
    o_lru, h_new = pl.pallas_call(
        _lru_step_kernel,
        out_shape=(jax.ShapeDtypeStruct((bs, LRU_W), BF16), jax.ShapeDtypeStruct((bs, LRU_W), F32)),
        name="lru_step",
    )(x_lru, gate_lru, *prevs, h0, conv_w, conv_b.reshape(1, LRU_W), lru_wa, lru_ba.reshape(1, LRU_W),
      lru_wx, lru_bx.reshape(1, LRU_W), lru_lambda.reshape(1, LRU_W))
    return o_lru, h_new


def _mem_prompt_kernel(q_ref, k_ref, v_ref, o_ref):
    s = _dot_nt(q_ref[...].astype(BF16), k_ref[...].astype(BF16)) * QK_SCALE
    m = jnp.max(s, axis=-1, keepdims=True)
    e = jnp.exp(s - m)
    p = e / jnp.sum(e, axis=-1, keepdims=True)
    o_ref[...] = _dot(p.astype(BF16), v_ref[...].astype(BF16)).astype(o_ref.dtype)


def _mem_prompt(proj, qcol0, mem_k, mem_v, t):
    bsz, _, n_mem, _ = mem_k.shape
    tq = min(t, 512)
    assert t % tq == 0 and qcol0 % HEAD_DIM == 0
    nt = t // tq
    qc = qcol0 // HEAD_DIM
    kv_spec = pl.BlockSpec((None, None, n_mem, HEAD_DIM), lambda b, h, i: (b, h, 0, 0))
    return pl.pallas_call(
        _mem_prompt_kernel,
        out_shape=jax.ShapeDtypeStruct((bsz * t, MEM_W), BF16),
        grid=(bsz, MEM_HEADS, nt),
        in_specs=[pl.BlockSpec((tq, HEAD_DIM), lambda b, h, i: (b * nt + i, qc + h)), kv_spec, kv_spec],
        out_specs=pl.BlockSpec((tq, HEAD_DIM), lambda b, h, i: (b * nt + i, h)),
        compiler_params=_cparams(("parallel", "parallel", "parallel"), 32),
        name="mem_prompt",
    )(proj, mem_k, mem_v)


def _mem_sample_kernel(q_ref, k_ref, v_ref, o_ref):
    group = q_ref.shape[0]
    for g in range(group):
        q = q_ref[g]
        s = jnp.sum(k_ref[g] * q[None], axis=-1, keepdims=True) * QK_SCALE
        m = jnp.max(s, axis=0, keepdims=True)
        e = jnp.exp(s - m)
        p = e / jnp.sum(e, axis=0, keepdims=True)
        o_ref[g] = jnp.sum(p * v_ref[g], axis=0).astype(o_ref.dtype)


def _mem_sample(q, cache_k, cache_v):
    bs, n_mem, heads, _ = cache_k.shape
    assert heads == MEM_HEADS and cache_k.shape[3] == HEAD_DIM
    group = 8
    assert bs % group == 0
    kv_spec = pl.BlockSpec((group, n_mem, heads, HEAD_DIM), lambda i: (i, 0, 0, 0))
    out = pl.pallas_call(
        _mem_sample_kernel,
        out_shape=jax.ShapeDtypeStruct((bs, heads, HEAD_DIM), BF16),
        grid=(bs // group,),
        in_specs=[pl.BlockSpec((group, heads, HEAD_DIM), lambda i: (i, 0, 0)), kv_spec, kv_spec],
        out_specs=pl.BlockSpec((group, heads, HEAD_DIM), lambda i: (i, 0, 0)),
        compiler_params=_cparams(("parallel",), 32),
        name="mem_sample",
    )(q.reshape(bs, heads, HEAD_DIM), cache_k, cache_v)
    return out.reshape(bs, MEM_W)


def _merge_kernel(oa_ref, ol_ref, om_ref, pa_ref, pl_ref, pm_ref, ga_ref, gl_ref, gm_ref, o_ref):
    ya = _dot(oa_ref[...], pa_ref[...].astype(BF16))
    yl = _dot(ol_ref[...], pl_ref[...].astype(BF16))
    ym = _dot(om_ref[...], pm_ref[...].astype(BF16))
    merged = _sigmoid(ga_ref[...]) * ya + _sigmoid(gl_ref[...]) * yl + _sigmoid(gm_ref[...]) * ym
    o_ref[...] = merged.astype(o_ref.dtype)


def _merge(o_sb, o_lru, o_mem, p_attn, p_lru, p_mem, gates, tm):
    r = o_sb.shape[0]
    d = p_attn.shape[1]
    tn = 512
    assert r % tm == 0 and d % tn == 0
    nj = d // tn
    return pl.pallas_call(
        _merge_kernel,
        out_shape=jax.ShapeDtypeStruct((r, d), BF16),
        grid=(r // tm, nj),
        in_specs=[
            pl.BlockSpec((tm, SB_W), lambda i, j: (i, 0)),
            pl.BlockSpec((tm, LRU_W), lambda i, j: (i, 0)),
            pl.BlockSpec((tm, MEM_W), lambda i, j: (i, 0)),
            pl.BlockSpec((SB_W, tn), lambda i, j: (0, j)),
            pl.BlockSpec((LRU_W, tn), lambda i, j: (0, j)),
            pl.BlockSpec((MEM_W, tn), lambda i, j: (0, j)),
            pl.BlockSpec((tm, tn), lambda i, j: (i, j)),
            pl.BlockSpec((tm, tn), lambda i, j: (i, nj + j)),
            pl.BlockSpec((tm, tn), lambda i, j: (i, 2 * nj + j)),
        ],
        out_specs=pl.BlockSpec((tm, tn), lambda i, j: (i, j)),
        compiler_params=_cparams(("parallel", "parallel"), 44),
        name="merge",
    )(o_sb, o_lru, o_mem, p_attn, p_lru, p_mem, gates, gates, gates)


def _post_kernel(x_ref, m_ref, wo_ref, g_ref, wr_ref, br_ref, cnt0_ref,
                 hres_ref, hnp_ref, idx_ref, gate_ref, rank_ref, cnt_ref):
    i = pl.program_id(0)
    tm, d = x_ref.shape

    @pl.when(i == 0)
    def _():
        cnt_ref[...] = cnt0_ref[...]

    hres = x_ref[...] + _dot(m_ref[...], wo_ref[...])
    hres_ref[...] = hres
    ms = jnp.mean(hres * hres, axis=-1, keepdims=True)
    hn = hres * lax.rsqrt(ms + EPS) * g_ref[...]
    hb = hn.astype(BF16)
    wa = lax.bitcast_convert_type(hb[:, :d // 2].astype(F32), jnp.uint32)
    wb = lax.bitcast_convert_type(hb[:, d // 2:].astype(F32), jnp.uint32)
    hnp_ref[...] = wa | (wb >> 16)

    logits = _dot_nt(wr_ref[...], hb) + br_ref[...]
    ne = logits.shape[0]
    eid = lax.broadcasted_iota(jnp.int32, (ne, tm), 0)
    work = logits
    vals, idxs, onehots = [], [], []
    for _ in range(TOP_K):
        mx = jnp.max(work, axis=0, keepdims=True)
        sel = jnp.min(jnp.where(work == mx, eid, ne), axis=0, keepdims=True)
        oh = eid == sel
        vals.append(mx)
        idxs.append(sel)
        onehots.append(oh)
        work = jnp.where(oh, -jnp.inf, work)
    es = [jnp.exp(v - vals[0]) for v in vals]
    den = functools.reduce(jnp.add, es)
    gate_ref[...] = jnp.concatenate([e / den for e in es], axis=0)
    idx_ref[...] = jnp.concatenate(idxs, axis=0)

    chosen = functools.reduce(jnp.logical_or, onehots)
    r_ = lax.broadcasted_iota(jnp.int32, (tm, tm), 0)
    c_ = lax.broadcasted_iota(jnp.int32, (tm, tm), 1)
    before = jnp.where(r_ < c_, 1.0, 0.0).astype(BF16)
    chosen_f = jnp.where(chosen, 1.0, 0.0)
    prior = _dot(chosen_f.astype(BF16), before) + cnt_ref[...]
    rank_ref[...] = jnp.concatenate(
        [jnp.sum(jnp.where(oh, prior, 0.0), axis=0, keepdims=True) for oh in onehots], axis=0).astype(jnp.int32)
    cnt_ref[...] = cnt_ref[...] + jnp.sum(chosen_f, axis=1, keepdims=True)


def _post(x, merged, wo_b, g, wr_t, b_router, cnt0, tm):
    r, d = x.shape
    ne = wr_t.shape[0]
    assert r % tm == 0
    full = lambda shape: pl.BlockSpec(shape, lambda i: (0,) * len(shape))
    return pl.pallas_call(
        _post_kernel,
        out_shape=(jax.ShapeDtypeStruct((r, d), F32),
                   jax.ShapeDtypeStruct((r, d // 2), jnp.uint32),
                   jax.ShapeDtypeStruct((TOP_K, r), jnp.int32),
                   jax.ShapeDtypeStruct((TOP_K, r), F32),
                   jax.ShapeDtypeStruct((TOP_K, r), jnp.int32),
                   jax.ShapeDtypeStruct((ne, 1), F32)),
        grid=(r // tm,),
        in_specs=[pl.BlockSpec((tm, d), lambda i: (i, 0)),
                  pl.BlockSpec((tm, d), lambda i: (i, 0)),
                  full((d, d)), full((1, d)), full((ne, d)), full((ne, 1)), full((ne, 1))],
        out_specs=(pl.BlockSpec((tm, d), lambda i: (i, 0)),
                   pl.BlockSpec((tm, d // 2), lambda i: (i, 0)),
                   pl.BlockSpec((TOP_K, tm), lambda i: (0, i)),
                   pl.BlockSpec((TOP_K, tm), lambda i: (0, i)),
                   pl.BlockSpec((TOP_K, tm), lambda i: (0, i)),
                   full((ne, 1))),
        compiler_params=_cparams(("arbitrary",), 48),
        name="post",
    )(x, merged, wo_b, g.reshape(1, d), wr_t, b_router.reshape(ne, 1), cnt0)


def _dispatch_kernel(tail_ref, dest_ref, hn_ref, xs_ref, zero_ref, sem, zsem):
    i = pl.program_id(0)
    tm = dest_ref.shape[2] // TOP_K

    @pl.when(i == 0)
    def _():
        zero_ref[...] = jnp.zeros_like(zero_ref)

        def tail_copy(e):
            start = pl.multiple_of(tail_ref[e], MOE_SUB)
            return pltpu.make_async_copy(zero_ref, xs_ref.at[pl.ds(start, MOE_SUB)], zsem)

        def zstart(e, c):
            pl.when(tail_ref[e] >= 0)(lambda: tail_copy(e).start())
            return c

        def zwait(e, c):
            pl.when(tail_ref[e] >= 0)(lambda: tail_copy(e).wait())
            return c

        lax.fori_loop(0, tail_ref.shape[0], zstart, 0)
        lax.fori_loop(0, tail_ref.shape[0], zwait, 0)

    def row_copy(t, k):
        return pltpu.make_async_copy(hn_ref.at[pl.ds(i * tm + t, 1)],
                                     xs_ref.at[pl.ds(dest_ref[0, 0, k * tm + t], 1)], sem)

    def start(t, c):
        for k in range(TOP_K):
            row_copy(t, k).start()
        return c

    def wait(t, c):
        for k in range(TOP_K):
            row_copy(t, k).wait()
        return c

    def wait_tile():
        lax.fori_loop(0, tm, wait, 0)

    lax.fori_loop(0, tm, start, 0)
    pl.when(i > 0)(wait_tile)
    pl.when(i == pl.num_programs(0) - 1)(wait_tile)


def _dispatch(hnp, dest, tail, n_rows, tm):
    r, w = hnp.shape
    assert r % tm == 0
    nt = r // tm
    dest_t = dest.reshape(TOP_K, nt, tm).transpose(1, 0, 2).reshape(nt, 1, TOP_K * tm)
    grid_spec = pltpu.PrefetchScalarGridSpec(
        num_scalar_prefetch=1,
        grid=(nt,),
        in_specs=[pl.BlockSpec((1, 1, TOP_K * tm), lambda i, tl: (i, 0, 0), memory_space=pltpu.SMEM),
                  pl.BlockSpec(memory_space=pl.ANY)],
        out_specs=pl.BlockSpec(memory_space=pl.ANY),
        scratch_shapes=[pltpu.VMEM((MOE_SUB, w), hnp.dtype), pltpu.SemaphoreType.DMA, pltpu.SemaphoreType.DMA],
    )
    return pl.pallas_call(
        _dispatch_kernel,
        out_shape=jax.ShapeDtypeStruct((n_rows, w), hnp.dtype),
        grid_spec=grid_spec,
        compiler_params=_cparams(("arbitrary",), 16),
        name="dispatch",
    )(tail, dest_t, hnp)


def _moe_kernel(ce_ref, cb_ref, ns_ref, xs_ref, wg_ref, wu_ref, bg_ref, bu_ref, wd_ref, bd_ref, o_ref,
                xa_ref, xb_ref, act_ref, wgb_ref, wub_ref, wdb_ref, *, nf):
    del ce_ref, cb_ref
    c = pl.program_id(0)
    s = pl.program_id(1)
    nsub = ns_ref[c]
    half = xa_ref.shape[1]

    def for_row_blocks(fn):
        done = 0
        for size in MOE_BLOCK_SUBS:
            count = (nsub - done) // size
            rows = size * MOE_SUB

            def body(i, carry, done=done, rows=rows):
                fn(pl.ds(pl.multiple_of(done * MOE_SUB + i * rows, MOE_SUB), rows))
                return carry

            lax.fori_loop(0, count, body, 0)
            done = done + count * size

    @pl.when((s == 0) & (nsub > 0))
    def _():
        def unpack(rows):
            word = xs_ref[rows, :]
            xa_ref[rows, :] = lax.bitcast_convert_type(word & jnp.uint32(0xFFFF0000), F32).astype(BF16)
            xb_ref[rows, :] = lax.bitcast_convert_type(word << 16, F32).astype(BF16)
        for_row_blocks(unpack)

    @pl.when((s < nf) & (nsub > 0))
    def _():
        wgb_ref[...] = wg_ref[...].astype(BF16)
        wub_ref[...] = wu_ref[...].astype(BF16)
        bg = bg_ref[...]
        bu = bu_ref[...]

        def up(rows):
            xa = xa_ref[rows, :]
            xb = xb_ref[rows, :]
            gate = _dot(xa, wgb_ref[:half, :]) + _dot(xb, wgb_ref[half:, :]) + bg
            upv = _dot(xa, wub_ref[:half, :]) + _dot(xb, wub_ref[half:, :]) + bu
            gate = jnp.minimum(gate, SWIGLU_LIMIT)
            upv = jnp.clip(upv, -SWIGLU_LIMIT, SWIGLU_LIMIT)
            act = gate * _sigmoid(SWIGLU_ALPHA * gate) * (upv + 1.0)
            act_ref[s, rows, :] = act.astype(BF16)
        for_row_blocks(up)

    @pl.when((s >= nf) & (nsub > 0))
    def _():
        wdb_ref[...] = wd_ref[...].astype(BF16)
        bd = bd_ref[...]
        tf = act_ref.shape[2]

        def down(rows):
            y = _dot(act_ref[0, rows, :], wdb_ref[0:tf, :])
            for f in range(1, nf):
                y = y + _dot(act_ref[f, rows, :], wdb_ref[f * tf:(f + 1) * tf, :])
            o_ref[rows, :] = y + bd
        for_row_blocks(down)


def _moe(xs, n_used, chunk_expert, chunk_block, chunk_nsub, w_up, b_up, w_down, b_down):
    ne, d, dff2 = w_up.shape
    dff = dff2 // 2
    rb = MOE_SUB * MOE_CHUNK_SUBS
    n_chunks = xs.shape[0] // rb
    tf = MOE_TF
    nf = dff // tf
    nn = d // tf
    assert dff % tf == 0 and d % tf == 0

    def up_f(s, ns_c):
        return jnp.where(ns_c > 0, jnp.minimum(s, nf - 1), nf - 1)

    def dn_n(s, ns_c):
        return jnp.where(ns_c > 0, jnp.maximum(s - nf, 0), nn - 1)

    grid_spec = pltpu.PrefetchScalarGridSpec(
        num_scalar_prefetch=3,
        grid=(n_used, nf + nn),
        in_specs=[
            pl.BlockSpec((rb, d // 2), lambda c, s, ce, cb, ns: (cb[c], 0)),
            pl.BlockSpec((None, d, tf), lambda c, s, ce, cb, ns: (ce[c], 0, up_f(s, ns[c]))),
            pl.BlockSpec((None, d, tf), lambda c, s, ce, cb, ns: (ce[c], 0, nf + up_f(s, ns[c]))),
            pl.BlockSpec((None, 1, tf), lambda c, s, ce, cb, ns: (ce[c], 0, up_f(s, ns[c]))),
            pl.BlockSpec((None, 1, tf), lambda c, s, ce, cb, ns: (ce[c], 0, nf + up_f(s, ns[c]))),
            pl.BlockSpec((None, dff, tf), lambda c, s, ce, cb, ns: (ce[c], 0, dn_n(s, ns[c]))),
            pl.BlockSpec((None, 1, tf), lambda c, s, ce, cb, ns: (ce[c], 0, dn_n(s, ns[c]))),
        ],
        out_specs=pl.BlockSpec((rb, tf), lambda c, s, ce, cb, ns: (cb[c], dn_n(s, ns[c]))),
        scratch_shapes=[pltpu.VMEM((rb, d // 2), BF16), pltpu.VMEM((rb, d // 2), BF16),
                        pltpu.VMEM((nf, rb, tf), BF16),
                        pltpu.VMEM((d, tf), BF16), pltpu.VMEM((d, tf), BF16), pltpu.VMEM((dff, tf), BF16)],
    )
    return pl.pallas_call(
        functools.partial(_moe_kernel, nf=nf),
        out_shape=jax.ShapeDtypeStruct((n_chunks * rb, d), F32),
        grid_spec=grid_spec,
        compiler_params=_cparams(("arbitrary", "arbitrary"), 52),
        name="moe",
    )(chunk_expert, chunk_block, chunk_nsub, xs, w_up, w_up, b_up.reshape(ne, 1, dff2), b_up.reshape(ne, 1, dff2),
      w_down, b_down.reshape(ne, 1, d))


def _combine_kernel(dest_ref, dest_next_ref, hres_ref, gate_ref, yb_ref, o_ref, buf_ref, sem):
    i = pl.program_id(0)
    tm = hres_ref.shape[0]
    slot = i % 2

    def row_copy(dref, s, t, k):
        return pltpu.make_async_copy(yb_ref.at[pl.ds(dref[0, 0, k * tm + t], 1)],
                                     buf_ref.at[s, k, pl.ds(t, 1)], sem.at[s])

    def issue(dref, s):
        def body(t, c):
            for k in range(TOP_K):
                row_copy(dref, s, t, k).start()
            return c
        lax.fori_loop(0, tm, body, 0)

    def drain(dref, s):
        def body(t, c):
            for k in range(TOP_K):
                row_copy(dref, s, t, k).wait()
            return c
        lax.fori_loop(0, tm, body, 0)

    pl.when(i == 0)(lambda: issue(dest_ref, slot))
    pl.when(i + 1 < pl.num_programs(0))(lambda: issue(dest_next_ref, 1 - slot))
    drain(dest_ref, slot)
    g = gate_ref[...]
    y = hres_ref[...]
    for k in range(TOP_K):
        y = y + g[:, k:k + 1] * buf_ref[slot, k]
    o_ref[...] = y


def _combine(hres, gates, dest, yb, tm):
    r, d = hres.shape
    assert r % tm == 0
    nt = r // tm
    dest_t = dest.reshape(TOP_K, nt, tm).transpose(1, 0, 2).reshape(nt, 1, TOP_K * tm)
    return pl.pallas_call(
        _combine_kernel,
        out_shape=jax.ShapeDtypeStruct((r, d), F32),
        grid=(nt,),
        in_specs=[pl.BlockSpec((1, 1, TOP_K * tm), lambda i: (i, 0, 0), memory_space=pltpu.SMEM),
                  pl.BlockSpec((1, 1, TOP_K * tm), lambda i: (jnp.minimum(i + 1, nt - 1), 0, 0),
                               memory_space=pltpu.SMEM),
                  pl.BlockSpec((tm, d), lambda i: (i, 0)),
                  pl.BlockSpec((tm, TOP_K), lambda i: (i, 0)),
                  pl.BlockSpec(memory_space=pl.ANY)],
        out_specs=pl.BlockSpec((tm, d), lambda i: (i, 0)),
        scratch_shapes=[pltpu.VMEM((2, TOP_K, tm, d), F32), pltpu.SemaphoreType.DMA((2,))],
        compiler_params=_cparams(("arbitrary",), 40),
        name="combine",
    )(dest_t, dest_t, hres, gates.T, yb)


def _row_tile(r, cap):
    tm = min(r, cap)
    assert r % tm == 0
    return tm


def kernel(x_prompt, x_sample, mem_prompt, cache_sb_k, cache_sb_v, page_table, cache_mem_k, cache_mem_v, state_conv, state_lru, norm_mix_g, norm_mem_g, w_in, q_sb_g, k_sb_g, sb_bias, q_mem_g, k_mem_g, w_mem_kv, conv_w, conv_b, lru_wa, lru_ba, lru_wx, lru_bx, lru_lambda, p_attn, p_lru, p_mem, w_o, norm_ffn_g, w_router, b_router, w_up, b_up, w_down, b_down):
    bp, t, d = x_prompt.shape
    bs, ts, _ = x_sample.shape
    assert ts == 1, "sample group decodes one token per sequence"
    n_mem = mem_prompt.shape[1]
    q_gain = jnp.tile(q_sb_g, SB_HEADS)
    k_gain = jnp.tile(k_sb_g, SB_HEADS)
    qm_gain = jnp.tile(q_mem_g, MEM_HEADS)
    heads_major = lambda a: jnp.transpose(a, (0, 2, 1, 3))

    memn = _rmsnorm(mem_prompt.reshape(bp * n_mem, d), norm_mem_g, _row_tile(bp * n_mem, 256))
    mem_tm = _row_tile(n_mem, 256)
    mem_k_hm = _proj(memn, w_mem_kv, jnp.tile(k_mem_g, MEM_HEADS), col0=0, ncols=MEM_W, norm_cols=(0, MEM_W),
                     tm=mem_tm, tn=MEM_W, heads_out=(bp, n_mem))
    mem_v_hm = _proj(memn, w_mem_kv, None, col0=MEM_W, ncols=MEM_W, tm=mem_tm, tn=MEM_W, heads_out=(bp, n_mem))

    rest_w = COL_GATE - COL_XL
    qm_cols = (COL_QM - COL_XL, rest_w)
    w_q = _cast_bf16(w_in, COL_Q, SB_W)
    w_k = _cast_bf16(w_in, COL_K, SB_W)
    w_v = _cast_bf16(w_in, COL_V, SB_W)
    w_rest = _cast_bf16(w_in, COL_XL, rest_w)
    w_gates = _cast_bf16(w_in, COL_GATE, N_BRANCH * d)
    rest_gain = jnp.concatenate([jnp.ones((2 * LRU_W,), F32), qm_gain])

    xp2 = x_prompt.reshape(bp * t, d)
    tm_p = _row_tile(t, 1024)
    xn_p = _rmsnorm(xp2, norm_mix_g, _row_tile(bp * t, 512))
    _, q_pb = _proj(xn_p, w_q, q_gain, ncols=SB_W, norm_cols=(0, SB_W), tm=tm_p, tn=SB_W, bf16_copy=True)
    k_p, k_pb = _proj(xn_p, w_k, k_gain, ncols=SB_W, norm_cols=(0, SB_W), tm=tm_p, tn=SB_W, heads_out=(bp, t),
                      bf16_copy=True)
    v_p, v_pb = _proj(xn_p, w_v, None, ncols=SB_W, tm=tm_p, tn=SB_W, heads_out=(bp, t), bf16_copy=True)
    rest_p = _proj(xn_p, w_rest, rest_gain, ncols=rest_w, norm_cols=qm_cols, tm=tm_p, tn=1024)
    gates_p = _proj(xn_p, w_gates, None, ncols=N_BRANCH * d, tm=tm_p, tn=1024)
    o_sb_p = _sb_prompt(q_pb, k_pb, v_pb, sb_bias)
    conv0 = jnp.zeros((bp, CONV_W - 1, LRU_W), x_prompt.dtype)
    lru0 = jnp.zeros((bp, LRU_W), state_lru.dtype)
    o_lru_p, lru_prompt = _lru_prompt(rest_p, conv0, lru0, conv_w, conv_b, lru_wa, lru_ba, lru_wx, lru_bx,
                                      lru_lambda, bp, t)
    o_mem_p = _mem_prompt(rest_p, COL_QM - COL_XL, mem_k_hm, mem_v_hm, t)
    xl_tail = rest_p.reshape(bp, t, -1)[:, t - (CONV_W - 1):, :LRU_W]
    conv_prompt = jnp.concatenate([conv0, xl_tail], axis=1)[:, -(CONV_W - 1):]

    xs2 = x_sample.reshape(bs, d)
    xn_s = _rmsnorm(xs2, norm_mix_g, bs)
    q_s = _proj(xn_s, w_q, q_gain, ncols=SB_W, norm_cols=(0, SB_W), tm=bs, tn=SB_W)
    k_s = _proj(xn_s, w_k, k_gain, ncols=SB_W, norm_cols=(0, SB_W), tm=bs, tn=SB_W)
    v_s = _proj(xn_s, w_v, None, ncols=SB_W, tm=bs, tn=SB_W)
    rest_s = _proj(xn_s, w_rest, rest_gain, ncols=rest_w, norm_cols=qm_cols, tm=bs, tn=1024)
    gates_s = _proj(xn_s, w_gates, None, ncols=N_BRANCH * d, tm=bs, tn=1024)
    o_sb_s = _sb_sample(q_s, sb_bias, heads_major(cache_sb_k), heads_major(cache_sb_v), page_table)
    xl_s = rest_s[:, :LRU_W]
    o_lru_s, lru_sample = _lru_step(xl_s, rest_s[:, LRU_W:2 * LRU_W], state_conv, state_lru, conv_w, conv_b,
                                    lru_wa, lru_ba, lru_wx, lru_bx, lru_lambda)
    conv_sample = jnp.concatenate([state_conv, xl_s[:, None, :]], axis=1)[:, -(CONV_W - 1):]
    o_mem_s = _mem_sample(rest_s[:, 2 * LRU_W:], cache_mem_k, cache_mem_v)

    wo_b = _cast_bf16(w_o)
    wr_t = w_router.T.astype(BF16)
    merged_p = _merge(o_sb_p, o_lru_p, o_mem_p, p_attn, p_lru, p_mem, gates_p, _row_tile(bp * t, 1024))
    merged_s = _merge(o_sb_s, o_lru_s, o_mem_s, p_attn, p_lru, p_mem, gates_s, _row_tile(bs, 1024))
    cnt0 = jnp.zeros((N_EXPERTS, 1), F32)
    hres_p, hnp_p, idx_p, gate_p, rank_p, cnt_p = _post(xp2, merged_p, wo_b, norm_ffn_g, wr_t, b_router,
                                                        cnt0, _row_tile(bp * t, 256))
    hres_s, hnp_s, idx_s, gate_s, rank_s, cnt_all = _post(xs2, merged_s, wo_b, norm_ffn_g, wr_t, b_router,
                                                          cnt_p, _row_tile(bs, 256))

    rb = MOE_SUB * MOE_CHUNK_SUBS
    n_assign = (bp * t + bs) * TOP_K
    n_chunks = -(-n_assign // rb) + N_EXPERTS
    counts = cnt_all[:, 0].astype(jnp.int32)
    chunks_e = (counts + rb - 1) // rb
    chunk_end = jnp.cumsum(chunks_e)
    chunk_start = chunk_end - chunks_e
    used = chunk_end[-1]
    cidx = jnp.arange(n_chunks, dtype=jnp.int32)
    last = jnp.minimum(cidx, used - 1)
    chunk_expert = jnp.clip(jnp.searchsorted(chunk_end, last, side='right'), 0, N_EXPERTS - 1).astype(jnp.int32)
    rows_left = counts[chunk_expert] - (last - chunk_start[chunk_expert]) * rb
    chunk_nsub = jnp.where(cidx < used, (jnp.clip(rows_left, 0, rb) + MOE_SUB - 1) // MOE_SUB, 0).astype(jnp.int32)
    chunk_block = last.astype(jnp.int32)
    row_start = chunk_start * rb
    experts = jnp.arange(N_EXPERTS, dtype=jnp.int32)

    def slots(idx, rank):
        return jnp.sum(jnp.where(idx[..., None] == experts, row_start, 0), axis=-1) + rank

    dest_p = slots(idx_p, rank_p)
    dest_s = slots(idx_s, rank_s)
    tail = jnp.where(counts % MOE_SUB != 0, row_start + counts // MOE_SUB * MOE_SUB, -1).astype(jnp.int32)

    hnp = jnp.concatenate([hnp_p, hnp_s], axis=0)
    dest = jnp.concatenate([dest_p, dest_s], axis=1)
    xs = _dispatch(hnp, dest, tail, n_chunks * rb, _row_tile(bp * t + bs, 128))
    yb = _moe(xs, used.astype(jnp.int32), chunk_expert, chunk_block, chunk_nsub, w_up, b_up, w_down, b_down)
    y_prompt = _combine(hres_p, gate_p, dest_p, yb, _row_tile(bp * t, 256)).reshape(bp, t, d)
    y_sample = _combine(hres_s, gate_s, dest_s, yb, _row_tile(bs, 256)).reshape(bs, 1, d)

    sb_k_sample = k_s.reshape(bs, 1, SB_HEADS, HEAD_DIM)
    sb_v_sample = v_s.reshape(bs, 1, SB_HEADS, HEAD_DIM)
    return (y_prompt, y_sample, heads_major(k_p), heads_major(v_p), heads_major(mem_k_hm), heads_major(mem_v_hm),
            conv_prompt, lru_prompt, sb_k_sample, sb_v_sample, conv_sample, lru_sample)
```

```python
import functools

import jax
import jax.numpy as jnp
from jax import lax
from jax.experimental import pallas as pl
from jax.experimental.pallas import tpu as pltpu

F32 = jnp.float32
BF16 = jnp.bfloat16

HEAD_DIM = 128
SB_HEADS = 6
SB_W = SB_HEADS * HEAD_DIM
LRU_HEADS = 6
LRU_W = LRU_HEADS * HEAD_DIM
LRU_C = 8.0
CONV_W = 4
MEM_HEADS = 4
MEM_W = MEM_HEADS * HEAD_DIM
N_BRANCH = 3
N_EXPERTS = 32
TOP_K = 4
SWIGLU_LIMIT = 7.0
SWIGLU_ALPHA = 1.702
EPS = 1e-6
QK_SCALE = HEAD_DIM ** -0.5

COL_Q, COL_K, COL_V = 0, SB_W, 2 * SB_W
COL_XL, COL_GL = 3 * SB_W, 3 * SB_W + LRU_W
COL_QM = 3 * SB_W + 2 * LRU_W
COL_GATE = COL_QM + MEM_W

MIB = 1024 * 1024
PROJ_TN = 256
SB_BLK = 256
SB_HEADS_PER_STEP = 6
LRU_TT = 256
MOE_SUB = 128
MOE_BLOCK_SUBS = (9, 8, 1)
MOE_CHUNK_SUBS = 9
MOE_TF = 256
MOE_TN = 512


def _cparams(sem, vmem_mib):
    return pltpu.CompilerParams(dimension_semantics=sem, vmem_limit_bytes=vmem_mib * MIB)


def _softplus(z):
    return jnp.maximum(z, 0.0) + jnp.log1p(jnp.exp(-jnp.abs(z)))


def _softplus_fast(z):
    return jnp.maximum(z, 0.0) + jnp.log(1.0 + jnp.exp(-jnp.abs(z)))


def _sigmoid(z):
    return 1.0 / (1.0 + jnp.exp(-z))


def _split_bf16(x):
    hi = x.astype(BF16)
    lo = (x - hi.astype(F32)).astype(BF16)
    return hi, lo


def _dot(a, b):
    return jnp.dot(a, b, preferred_element_type=F32)


def _dot_nt(a, b):
    return lax.dot_general(a, b, (((1,), (1,)), ((), ())), preferred_element_type=F32)


def _rmsnorm_kernel(x_ref, g_ref, o_ref):
    x = x_ref[...]
    ms = jnp.mean(x * x, axis=-1, keepdims=True)
    o_ref[...] = (x * lax.rsqrt(ms + EPS) * g_ref[...]).astype(o_ref.dtype)


def _rmsnorm(x, g, tm):
    r, d = x.shape
    assert r % tm == 0
    return pl.pallas_call(
        _rmsnorm_kernel,
        out_shape=jax.ShapeDtypeStruct((r, d), BF16),
        grid=(r // tm,),
        in_specs=[pl.BlockSpec((tm, d), lambda i: (i, 0)), pl.BlockSpec((1, d), lambda i: (0, 0))],
        out_specs=pl.BlockSpec((tm, d), lambda i: (i, 0)),
        compiler_params=_cparams(("parallel",), 32),
        name="rmsnorm",
    )(x, g.reshape(1, d))


def _cast_kernel(x_ref, o_ref):
    o_ref[...] = x_ref[...].astype(o_ref.dtype)


def _cast_bf16(w, col0=0, ncols=None):
    r, c = w.shape
    ncols = c - col0 if ncols is None else ncols
    tc = PROJ_TN
    assert col0 % tc == 0 and ncols % tc == 0
    c0 = col0 // tc
    return pl.pallas_call(
        _cast_kernel,
        out_shape=jax.ShapeDtypeStruct((r, ncols), BF16),
        grid=(ncols // tc,),
        in_specs=[pl.BlockSpec((r, tc), lambda j: (0, j + c0))],
        out_specs=pl.BlockSpec((r, tc), lambda j: (0, j)),
        compiler_params=_cparams(("parallel",), 32),
        name="cast_bf16",
    )(w)


def _proj_kernel(xn_ref, w_ref, gain_ref, *o_refs, patterns, heads_out):
    j = pl.program_id(1)
    y = _dot(xn_ref[...], w_ref[...].astype(BF16))
    nh = y.shape[1] // HEAD_DIM

    def store(c, val):
        for o_ref in o_refs:
            if heads_out:
                o_ref[c] = val.astype(o_ref.dtype)
            else:
                o_ref[:, c * HEAD_DIM:(c + 1) * HEAD_DIM] = val.astype(o_ref.dtype)

    def emit(pattern):
        gain = gain_ref[...] if any(pattern) else None
        for c in range(nh):
            yc = y[:, c * HEAD_DIM:(c + 1) * HEAD_DIM]
            if pattern[c]:
                ms = jnp.mean(yc * yc, axis=-1, keepdims=True)
                yc = yc * lax.rsqrt(ms + EPS) * gain[:, c * HEAD_DIM:(c + 1) * HEAD_DIM]
            store(c, yc)

    distinct = sorted(set(patterns))
    if len(distinct) == 1:
        emit(distinct[0])
    else:
        for pattern in distinct:
            tiles = [jj for jj, p in enumerate(patterns) if p == pattern]
            pred = functools.reduce(jnp.logical_or, [j == jj for jj in tiles])
            pl.when(pred)(functools.partial(emit, pattern))


def _proj(xn, w, gain, *, col0=0, ncols, norm_cols=(0, 0), tm, tn, heads_out=None, bf16_copy=False):
    r, d = xn.shape
    assert r % tm == 0 and col0 % tn == 0 and ncols % tn == 0 and tn % HEAD_DIM == 0
    assert norm_cols[0] % HEAD_DIM == 0 and norm_cols[1] % HEAD_DIM == 0
    col_tile0 = col0 // tn
    patterns = tuple(tuple(norm_cols[0] <= jj * tn + c * HEAD_DIM < norm_cols[1] for c in range(tn // HEAD_DIM))
                     for jj in range(ncols // tn))
    if gain is None:
        gain = jnp.ones((ncols,), F32)
    if heads_out is None:
        out_shape = jax.ShapeDtypeStruct((r, ncols), F32)
        out_spec = pl.BlockSpec((tm, tn), lambda i, j: (i, j))
    else:
        bsz, t = heads_out
        assert bsz * t == r and t % tm == 0
        nt = t // tm
        hpt = tn // HEAD_DIM
        out_shape = jax.ShapeDtypeStruct((bsz, ncols // HEAD_DIM, t, HEAD_DIM), F32)
        out_spec = pl.BlockSpec((None, hpt, tm, HEAD_DIM), lambda i, j: (i // nt, j, i % nt, 0))
    kern = functools.partial(_proj_kernel, patterns=patterns, heads_out=heads_out is not None)
    if bf16_copy:
        out_shape = (out_shape, jax.ShapeDtypeStruct(out_shape.shape, BF16))
        out_spec = (out_spec, out_spec)
    return pl.pallas_call(
        kern,
        out_shape=out_shape,
        grid=(r // tm, ncols // tn),
        in_specs=[
            pl.BlockSpec((tm, d), lambda i, j: (i, 0)),
            pl.BlockSpec((d, tn), lambda i, j: (0, j + col_tile0)),
            pl.BlockSpec((1, tn), lambda i, j: (0, j)),
        ],
        out_specs=out_spec,
        compiler_params=_cparams(("parallel", "parallel"), 40),
        name="proj",
    )(xn, w, gain.reshape(1, ncols))


def _sb_prompt_kernel(bias_ref, q_ref, kb_ref, vb_ref, o_ref):
    hg = pl.program_id(1)
    qi = pl.program_id(2)
    blk = q_ref.shape[0]
    nh = kb_ref.shape[0]

    qs = [q_ref[:, j * HEAD_DIM:(j + 1) * HEAD_DIM] for j in range(nh)]
    biases = [bias_ref[hg * nh + j] for j in range(nh)]
    row = lax.broadcasted_iota(jnp.int32, (blk, blk), 0)
    col = lax.broadcasted_iota(jnp.int32, (blk, blk), 1)
    tri = jnp.where(row >= col, 1.0, 0.0).astype(BF16)
    causal = col < row

    def block(kb, state, masked):
        start = pl.multiple_of(kb * blk, blk)
        heads = range(nh)
        zs = [_dot_nt(qs[j], kb_ref[j, pl.ds(start, blk), :]) * QK_SCALE + biases[j] for j in heads]
        drops = [_softplus_fast(z) for z in zs]
        if masked:
            drops = [jnp.where(causal, dr, 0.0) for dr in drops]
        splits = [_split_bf16(dr) for dr in drops]
        suffixes = [_dot(hi, tri) + _dot(lo, tri) for hi, lo in splits]
        ws = [jnp.exp(zs[j] - suffixes[j] - state[j][0]) for j in heads]
        if masked:
            ws = [jnp.where(causal, w, 0.0) for w in ws]
        pvs = [_dot(ws[j].astype(BF16), vb_ref[j, pl.ds(start, blk), :]) for j in heads]
        return tuple((state[j][0] + suffixes[j][:, 0:1], state[j][1] + pvs[j]) for j in heads)

    zero = (jnp.zeros((blk, 1), F32), jnp.zeros((blk, HEAD_DIM), F32))
    state = block(qi, (zero,) * nh, True)
    state = lax.fori_loop(0, qi, lambda it, st: block(qi - 1 - it, st, False), state)
    for j in range(nh):
        o_ref[:, j * HEAD_DIM:(j + 1) * HEAD_DIM] = state[j][1].astype(o_ref.dtype)


def _sb_prompt(q, k, v, sb_bias):
    bsz, heads, t, _ = k.shape
    blk = SB_BLK
    nh = SB_HEADS_PER_STEP
    assert t % blk == 0 and heads % nh == 0
    assert q.dtype == BF16 and k.dtype == BF16 and v.dtype == BF16
    nq = t // blk
    kv_spec = pl.BlockSpec((None, nh, t, HEAD_DIM), lambda b, h, i: (b, h, 0, 0))
    return pl.pallas_call(
        _sb_prompt_kernel,
        out_shape=jax.ShapeDtypeStruct((bsz * t, SB_W), BF16),
        grid=(bsz, heads // nh, nq),
        in_specs=[
            pl.BlockSpec(memory_space=pltpu.SMEM),
            pl.BlockSpec((blk, nh * HEAD_DIM), lambda b, h, i: (b * nq + i, h)),
            kv_spec, kv_spec,
        ],
        out_specs=pl.BlockSpec((blk, nh * HEAD_DIM), lambda b, h, i: (b * nq + i, h)),
        compiler_params=_cparams(("parallel", "parallel", "parallel"), 40),
        name="sb_prompt",
    )(sb_bias, q, k, v)


def _sb_sample_kernel(pt_ref, bias_ref, q_ref, *refs, n_pages):
    del pt_ref
    k_refs = refs[:n_pages]
    v_refs = refs[n_pages:2 * n_pages]
    o_ref = refs[2 * n_pages]
    heads, page = k_refs[0].shape[:2]
    rep = 8
    n = n_pages * rep

    row = lax.broadcasted_iota(jnp.int32, (page, page), 0)
    col = lax.broadcasted_iota(jnp.int32, (page, page), 1)
    tri = jnp.where(row >= col, 1.0, 0.0).astype(BF16)
    rn = lax.broadcasted_iota(jnp.int32, (n, n), 0)
    cn = lax.broadcasted_iota(jnp.int32, (n, n), 1)
    later = jnp.where(((rn % rep) == (cn % rep)) & (cn // rep > rn // rep), 1.0, 0.0).astype(BF16)
    q = q_ref[0]
    zs = []
    for h in range(heads):
        q8 = jnp.broadcast_to(q[h:h + 1, :], (rep, HEAD_DIM)).astype(BF16)
        z = jnp.concatenate([_dot_nt(q8, k_refs[p][h].astype(BF16)) for p in range(n_pages)], axis=0)
        zs.append(z * QK_SCALE + bias_ref[h])
    z = jnp.concatenate(zs, axis=0)
    hi, lo = _split_bf16(-_softplus_fast(z))
    suffix = _dot(hi, tri) + _dot(lo, tri)
    shi, slo = _split_bf16(suffix)
    carry = jnp.concatenate(
        [(_dot(later, shi[h * n:(h + 1) * n]) + _dot(later, slo[h * n:(h + 1) * n]))[:, 0:1] for h in range(heads)],
        axis=0)
    w = jnp.exp(z + suffix + carry).astype(BF16)
    outs = []
    for h in range(heads):
        acc = jnp.zeros((rep, HEAD_DIM), F32)
        for p in range(n_pages):
            r0 = h * n + p * rep
            acc = acc + _dot(w[r0:r0 + rep, :], v_refs[p][h].astype(BF16))
        outs.append(acc[0:1, :])
    o_ref[0] = jnp.concatenate(outs, axis=0).astype(o_ref.dtype)


def _sb_sample(q, sb_bias, cache_k, cache_v, page_table):
    bs, n_pages = page_table.shape
    heads, page = cache_k.shape[1:3]
    assert heads == SB_HEADS and cache_k.shape[3] == HEAD_DIM

    def page_spec(p):
        return pl.BlockSpec((None, heads, page, HEAD_DIM), lambda b, pt: (pt[b, p], 0, 0, 0))

    grid_spec = pltpu.PrefetchScalarGridSpec(
        num_scalar_prefetch=1,
        grid=(bs,),
        in_specs=[pl.BlockSpec(memory_space=pltpu.SMEM),
                  pl.BlockSpec((1, heads, HEAD_DIM), lambda b, pt: (b, 0, 0))]
                 + [page_spec(p) for p in range(n_pages)] * 2,
        out_specs=pl.BlockSpec((1, heads, HEAD_DIM), lambda b, pt: (b, 0, 0)),
    )
    out = pl.pallas_call(
        functools.partial(_sb_sample_kernel, n_pages=n_pages),
        out_shape=jax.ShapeDtypeStruct((bs, heads, HEAD_DIM), BF16),
        grid_spec=grid_spec,
        compiler_params=_cparams(("arbitrary",), 40),
        name="sb_sample",
    )(page_table, sb_bias, q.reshape(bs, heads, HEAD_DIM), *([cache_k] * n_pages), *([cache_v] * n_pages))
    return out.reshape(bs, SB_W)


def _lru_gates(xc, wa_ref, ba, wx_ref, bx, nsp_lambda):
    xb = xc.astype(BF16)
    r_parts, i_parts = [], []
    for h in range(LRU_HEADS):
        xh = xb[:, h * HEAD_DIM:(h + 1) * HEAD_DIM]
        r_parts.append(_dot(xh, wa_ref[h].astype(BF16)))
        i_parts.append(_dot(xh, wx_ref[h].astype(BF16)))
    r = _sigmoid(jnp.concatenate(r_parts, axis=1) + ba)
    i = _sigmoid(jnp.concatenate(i_parts, axis=1) + bx)
    log_a = -LRU_C * r * nsp_lambda
    a = jnp.exp(log_a)
    u = jnp.sqrt(1.0 - jnp.exp(2.0 * log_a)) * (i * xc)
    return a, u


def _gelu_tanh(x):
    return 0.5 * x * (1.0 + jnp.tanh(0.7978845608028654 * (x + 0.044715 * (x * x * x))))


def _lru_prompt_kernel(x_ref, gate_ref, prev_ref, h0_ref, cw_ref, cb_ref, wa_ref, ba_ref, wx_ref, bx_ref,
                       lam_ref, o_ref, hlast_ref, xp_ref, h_ref):
    ti = pl.program_id(1)
    tt = x_ref.shape[0]
    pad = 8

    @pl.when(ti == 0)
    def _():
        xp_ref[pad - (CONV_W - 1):pad, :] = prev_ref[0]
        h_ref[...] = h0_ref[0]

    x = x_ref[...]
    xp_ref[pad:pad + tt, :] = x
    cw = cw_ref[...]
    xc = cb_ref[...] + cw[CONV_W - 1:CONV_W, :] * x
    for j in range(CONV_W - 1):
        xc = xc + cw[j:j + 1, :] * xp_ref[pad - (CONV_W - 1) + j:pad - (CONV_W - 1) + j + tt, :]
    xp_ref[pad - (CONV_W - 1):pad, :] = x[tt - (CONV_W - 1):tt, :]

    nsp = _softplus(-lam_ref[...])
    a, u = _lru_gates(xc, wa_ref, ba_ref[...], wx_ref, bx_ref[...], nsp)
    rows = lax.broadcasted_iota(jnp.int32, (tt, LRU_W), 0)
    b = jnp.where(rows == 0, u + a * h_ref[...], u)
    s = 1
    while s < tt:
        keep = rows >= s
        b = jnp.where(keep, a * pltpu.roll(b, s, 0) + b, b)
        if 2 * s < tt:
            a = jnp.where(keep, a * pltpu.roll(a, s, 0), a)
        s *= 2
    h_ref[...] = b[tt - 1:tt, :]
    o_ref[...] = (b * _gelu_tanh(gate_ref[...])).astype(o_ref.dtype)

    @pl.when(ti == pl.num_programs(1) - 1)
    def _():
        hlast_ref[0] = b[tt - 1:tt, :]


def _lru_prompt(proj, conv_prev, h0, conv_w, conv_b, lru_wa, lru_ba, lru_wx, lru_bx, lru_lambda, bsz, t):
    tt = LRU_TT
    assert t % tt == 0
    nt = t // tt
    xl, gl = 0, 1
    full = lambda shape: pl.BlockSpec(shape, lambda b, i: (0,) * len(shape))
    o_lru, h_last = pl.pallas_call(
        _lru_prompt_kernel,
        out_shape=(jax.ShapeDtypeStruct((bsz * t, LRU_W), BF16),
                   jax.ShapeDtypeStruct((bsz, 1, LRU_W), F32)),
        grid=(bsz, nt),
        in_specs=[
            pl.BlockSpec((tt, LRU_W), lambda b, i: (b * nt + i, xl)),
            pl.BlockSpec((tt, LRU_W), lambda b, i: (b * nt + i, gl)),
            pl.BlockSpec((1, CONV_W - 1, LRU_W), lambda b, i: (b, 0, 0)),
            pl.BlockSpec((1, 1, LRU_W), lambda b, i: (b, 0, 0)),
            full((CONV_W, LRU_W)), full((1, LRU_W)),
            full((LRU_HEADS, HEAD_DIM, HEAD_DIM)), full((1, LRU_W)),
            full((LRU_HEADS, HEAD_DIM, HEAD_DIM)), full((1, LRU_W)),
            full((1, LRU_W)),
        ],
        out_specs=(pl.BlockSpec((tt, LRU_W), lambda b, i: (b * nt + i, 0)),
                   pl.BlockSpec((1, 1, LRU_W), lambda b, i: (b, 0, 0))),
        scratch_shapes=[pltpu.VMEM((8 + tt, LRU_W), F32), pltpu.VMEM((1, LRU_W), F32)],
        compiler_params=_cparams(("parallel", "arbitrary"), 32),
        name="lru_prompt",
    )(proj, proj, conv_prev, h0.reshape(bsz, 1, LRU_W), conv_w, conv_b.reshape(1, LRU_W),
      lru_wa, lru_ba.reshape(1, LRU_W), lru_wx, lru_bx.reshape(1, LRU_W), lru_lambda.reshape(1, LRU_W))
    return o_lru, h_last.reshape(bsz, LRU_W)


def _lru_step_kernel(x_ref, gate_ref, p0_ref, p1_ref, p2_ref, h0_ref, cw_ref, cb_ref, wa_ref, ba_ref, wx_ref,
                     bx_ref, lam_ref, o_ref, h_ref):
    cw = cw_ref[...]
    xc = (cb_ref[...] + cw[0:1, :] * p0_ref[...] + cw[1:2, :] * p1_ref[...] + cw[2:3, :] * p2_ref[...]
          + cw[3:4, :] * x_ref[...])
    nsp = _softplus(-lam_ref[...])
    a, u = _lru_gates(xc, wa_ref, ba_ref[...], wx_ref, bx_ref[...], nsp)
    h = u + a * h0_ref[...]
    h_ref[...] = h
    o_ref[...] = (h * _gelu_tanh(gate_ref[...])).astype(o_ref.dtype)


def _lru_step(x_lru, gate_lru, state_conv, h0, conv_w, conv_b, lru_wa, lru_ba, lru_wx, lru_bx, lru_lambda):
    bs = x_lru.shape[0]
    assert CONV_W == 4
    prevs = [state_conv[:, j, :] for j in range(CONV_W - 1)]
    o_lru, h_new = pl.pallas_call(
        _lru_step_kernel,
        out_shape=(jax.ShapeDtypeStruct((bs, LRU_W), BF16), jax.ShapeDtypeStruct((bs, LRU_W), F32)),
        name="lru_step",
    )(x_lru, gate_lru, *prevs, h0, conv_w, conv_b.reshape(1, LRU_W), lru_wa, lru_ba.reshape(1, LRU_W),
      lru_wx, lru_bx.reshape(1, LRU_W), lru_lambda.reshape(1, LRU_W))
    return o_lru, h_new


def _mem_prompt_kernel(q_ref, k_ref, v_ref, o_ref):
    s = _dot_nt(q_ref[...].astype(BF16), k_ref[...].astype(BF16)) * QK_SCALE
    m = jnp.max(s, axis=-1, keepdims=True)
    e = jnp.exp(s - m)
    p = e / jnp.sum(e, axis=-1, keepdims=True)
    o_ref[...] = _dot(p.astype(BF16), v_ref[...].astype(BF16)).astype(o_ref.dtype)


def _mem_prompt(proj, qcol0, mem_k, mem_v, t):
    bsz, _, n_mem, _ = mem_k.shape
    tq = min(t, 512)
    assert t % tq == 0 and qcol0 % HEAD_DIM == 0
    nt = t // tq
    qc = qcol0 // HEAD_DIM
    kv_spec = pl.BlockSpec((None, None, n_mem, HEAD_DIM), lambda b, h, i: (b, h, 0, 0))
    return pl.pallas_call(
        _mem_prompt_kernel,
        out_shape=jax.ShapeDtypeStruct((bsz * t, MEM_W), BF16),
        grid=(bsz, MEM_HEADS, nt),
        in_specs=[pl.BlockSpec((tq, HEAD_DIM), lambda b, h, i: (b * nt + i, qc + h)), kv_spec, kv_spec],
        out_specs=pl.BlockSpec((tq, HEAD_DIM), lambda b, h, i: (b * nt + i, h)),
        compiler_params=_cparams(("parallel", "parallel", "parallel"), 32),
        name="mem_prompt",
    )(proj, mem_k, mem_v)


def _mem_sample_kernel(q_ref, k_ref, v_ref, o_ref):
    group = q_ref.shape[0]
    for g in range(group):
        q = q_ref[g]
        s = jnp.sum(k_ref[g] * q[None], axis=-1, keepdims=True) * QK_SCALE
        m = jnp.max(s, axis=0, keepdims=True)
        e = jnp.exp(s - m)
        p = e / jnp.sum(e, axis=0, keepdims=True)
        o_ref[g] = jnp.sum(p * v_ref[g], axis=0).astype(o_ref.dtype)


def _mem_sample(q, cache_k, cache_v):
    bs, n_mem, heads, _ = cache_k.shape
    assert heads == MEM_HEADS and cache_k.shape[3] == HEAD_DIM
    group = 8
    assert bs % group == 0
    kv_spec = pl.BlockSpec((group, n_mem, heads, HEAD_DIM), lambda i: (i, 0, 0, 0))
    out = pl.pallas_call(
        _mem_sample_kernel,
        out_shape=jax.ShapeDtypeStruct((bs, heads, HEAD_DIM), BF16),
        grid=(bs // group,),
        in_specs=[pl.BlockSpec((group, heads, HEAD_DIM), lambda i: (i, 0, 0)), kv_spec, kv_spec],
        out_specs=pl.BlockSpec((group, heads, HEAD_DIM), lambda i: (i, 0, 0)),
        compiler_params=_cparams(("parallel",), 32),
        name="mem_sample",
    )(q.reshape(bs, heads, HEAD_DIM), cache_k, cache_v)
    return out.reshape(bs, MEM_W)


def _merge_kernel(oa_ref, ol_ref, om_ref, pa_ref, pl_ref, pm_ref, ga_ref, gl_ref, gm_ref, o_ref):
    ya = _dot(oa_ref[...], pa_ref[...].astype(BF16))
    yl = _dot(ol_ref[...], pl_ref[...].astype(BF16))
    ym = _dot(om_ref[...], pm_ref[...].astype(BF16))
    merged = _sigmoid(ga_ref[...]) * ya + _sigmoid(gl_ref[...]) * yl + _sigmoid(gm_ref[...]) * ym
    o_ref[...] = merged.astype(o_ref.dtype)


def _merge(o_sb, o_lru, o_mem, p_attn, p_lru, p_mem, gates, tm):
    r = o_sb.shape[0]
    d = p_attn.shape[1]
    tn = 512
    assert r % tm == 0 and d % tn == 0
    nj = d // tn
    return pl.pallas_call(
        _merge_kernel,
        out_shape=jax.ShapeDtypeStruct((r, d), BF16),
        grid=(r // tm, nj),
        in_specs=[
            pl.BlockSpec((tm, SB_W), lambda i, j: (i, 0)),
            pl.BlockSpec((tm, LRU_W), lambda i, j: (i, 0)),
            pl.BlockSpec((tm, MEM_W), lambda i, j: (i, 0)),
            pl.BlockSpec((SB_W, tn), lambda i, j: (0, j)),
            pl.BlockSpec((LRU_W, tn), lambda i, j: (0, j)),
            pl.BlockSpec((MEM_W, tn), lambda i, j: (0, j)),
            pl.BlockSpec((tm, tn), lambda i, j: (i, j)),
            pl.BlockSpec((tm, tn), lambda i, j: (i, nj + j)),
            pl.BlockSpec((tm, tn), lambda i, j: (i, 2 * nj + j)),
        ],
        out_specs=pl.BlockSpec((tm, tn), lambda i, j: (i, j)),
        compiler_params=_cparams(("parallel", "parallel"), 44),
        name="merge",
    )(o_sb, o_lru, o_mem, p_attn, p_lru, p_mem, gates, gates, gates)


def _post_kernel(x_ref, m_ref, wo_ref, g_ref, wr_ref, br_ref, cnt0_ref,
                 hres_ref, hnp_ref, idx_ref, gate_ref, rank_ref, cnt_ref):
    i = pl.program_id(0)
    tm, d = x_ref.shape

    @pl.when(i == 0)
    def _():
        cnt_ref[...] = cnt0_ref[...]

    hres = x_ref[...] + _dot(m_ref[...], wo_ref[...])
    hres_ref[...] = hres
    ms = jnp.mean(hres * hres, axis=-1, keepdims=True)
    hn = hres * lax.rsqrt(ms + EPS) * g_ref[...]
    hb = hn.astype(BF16)
    wa = lax.bitcast_convert_type(hb[:, :d // 2].astype(F32), jnp.uint32)
    wb = lax.bitcast_convert_type(hb[:, d // 2:].astype(F32), jnp.uint32)
    hnp_ref[...] = wa | (wb >> 16)

    logits = _dot_nt(wr_ref[...], hb) + br_ref[...]
    ne = logits.shape[0]
    eid = lax.broadcasted_iota(jnp.int32, (ne, tm), 0)
    work = logits
    vals, idxs, onehots = [], [], []
    for _ in range(TOP_K):
        mx = jnp.max(work, axis=0, keepdims=True)
        sel = jnp.min(jnp.where(work == mx, eid, ne), axis=0, keepdims=True)
        oh = eid == sel
        vals.append(mx)
        idxs.append(sel)
        onehots.append(oh)
        work = jnp.where(oh, -jnp.inf, work)
    es = [jnp.exp(v - vals[0]) for v in vals]
    den = functools.reduce(jnp.add, es)
    gate_ref[...] = jnp.concatenate([e / den for e in es], axis=0)
    idx_ref[...] = jnp.concatenate(idxs, axis=0)

    chosen = functools.reduce(jnp.logical_or, onehots)
    r_ = lax.broadcasted_iota(jnp.int32, (tm, tm), 0)
    c_ = lax.broadcasted_iota(jnp.int32, (tm, tm), 1)
    before = jnp.where(r_ < c_, 1.0, 0.0).astype(BF16)
    chosen_f = jnp.where(chosen, 1.0, 0.0)
    prior = _dot(chosen_f.astype(BF16), before) + cnt_ref[...]
    rank_ref[...] = jnp.concatenate(
        [jnp.sum(jnp.where(oh, prior, 0.0), axis=0, keepdims=True) for oh in onehots], axis=0).astype(jnp.int32)
    cnt_ref[...] = cnt_ref[...] + jnp.sum(chosen_f, axis=1, keepdims=True)


def _post(x, merged, wo_b, g, wr_t, b_router, cnt0, tm):
    r, d = x.shape
    ne = wr_t.shape[0]
    assert r % tm == 0
    full = lambda shape: pl.BlockSpec(shape, lambda i: (0,) * len(shape))
    return pl.pallas_call(
        _post_kernel,
        out_shape=(jax.ShapeDtypeStruct((r, d), F32),
                   jax.ShapeDtypeStruct((r, d // 2), jnp.uint32),
                   jax.ShapeDtypeStruct((TOP_K, r), jnp.int32),
                   jax.ShapeDtypeStruct((TOP_K, r), F32),
                   jax.ShapeDtypeStruct((TOP_K, r), jnp.int32),
                   jax.ShapeDtypeStruct((ne, 1), F32)),
        grid=(r // tm,),
        in_specs=[pl.BlockSpec((tm, d), lambda i: (i, 0)),
                  pl.BlockSpec((tm, d), lambda i: (i, 0)),
                  full((d, d)), full((1, d)), full((ne, d)), full((ne, 1)), full((ne, 1))],
        out_specs=(pl.BlockSpec((tm, d), lambda i: (i, 0)),
                   pl.BlockSpec((tm, d // 2), lambda i: (i, 0)),
                   pl.BlockSpec((TOP_K, tm), lambda i: (0, i)),
                   pl.BlockSpec((TOP_K, tm), lambda i: (0, i)),
                   pl.BlockSpec((TOP_K, tm), lambda i: (0, i)),
                   full((ne, 1))),
        compiler_params=_cparams(("arbitrary",), 48),
        name="post",
    )(x, merged, wo_b, g.reshape(1, d), wr_t, b_router.reshape(ne, 1), cnt0)


def _dispatch_kernel(tail_ref, dest_ref, hn_ref, xs_ref, zero_ref, sem, zsem):
    i = pl.program_id(0)
    tm = dest_ref.shape[2] // TOP_K

    @pl.when(i == 0)
    def _():
        zero_ref[...] = jnp.zeros_like(zero_ref)

        def tail_copy(e):
            start = pl.multiple_of(tail_ref[e], MOE_SUB)
            return pltpu.make_async_copy(zero_ref, xs_ref.at[pl.ds(start, MOE_SUB)], zsem)

        def zstart(e, c):
            pl.when(tail_ref[e] >= 0)(lambda: tail_copy(e).start())
            return c

        def zwait(e, c):
            pl.when(tail_ref[e] >= 0)(lambda: tail_copy(e).wait())
            return c

        lax.fori_loop(0, tail_ref.shape[0], zstart, 0)
        lax.fori_loop(0, tail_ref.shape[0], zwait, 0)

    def row_copy(t, k):
        return pltpu.make_async_copy(hn_ref.at[pl.ds(i * tm + t, 1)],
                                     xs_ref.at[pl.ds(dest_ref[0, 0, k * tm + t], 1)], sem)

    def start(t, c):
        for k in range(TOP_K):
            row_copy(t, k).start()
        return c

    def wait(t, c):
        for k in range(TOP_K):
            row_copy(t, k).wait()
        return c

    def wait_tile():
        lax.fori_loop(0, tm, wait, 0)

    lax.fori_loop(0, tm, start, 0)
    pl.when(i > 0)(wait_tile)
    pl.when(i == pl.num_programs(0) - 1)(wait_tile)


def _dispatch(hnp, dest, tail, n_rows, tm):
    r, w = hnp.shape
    assert r % tm == 0
    nt = r // tm
    dest_t = dest.reshape(TOP_K, nt, tm).transpose(1, 0, 2).reshape(nt, 1, TOP_K * tm)
    grid_spec = pltpu.PrefetchScalarGridSpec(
        num_scalar_prefetch=1,
        grid=(nt,),
        in_specs=[pl.BlockSpec((1, 1, TOP_K * tm), lambda i, tl: (i, 0, 0), memory_space=pltpu.SMEM),
                  pl.BlockSpec(memory_space=pl.ANY)],
        out_specs=pl.BlockSpec(memory_space=pl.ANY),
        scratch_shapes=[pltpu.VMEM((MOE_SUB, w), hnp.dtype), pltpu.SemaphoreType.DMA, pltpu.SemaphoreType.DMA],
    )
    return pl.pallas_call(
        _dispatch_kernel,
        out_shape=jax.ShapeDtypeStruct((n_rows, w), hnp.dtype),
        grid_spec=grid_spec,
        compiler_params=_cparams(("arbitrary",), 16),
        name="dispatch",
    )(tail, dest_t, hnp)


def _moe_kernel(ce_ref, cb_ref, ns_ref, xs_ref, wg_ref, wu_ref, bg_ref, bu_ref, wd_ref, bd_ref, o_ref,
                xa_ref, xb_ref, wgb_ref, wub_ref, wdb_ref):
    del ce_ref, cb_ref
    c = pl.program_id(0)
    s = pl.program_id(1)
    nsub = ns_ref[c]
    half = xa_ref.shape[1]
    d = o_ref.shape[1]

    def for_row_blocks(fn):
        done = 0
        for size in MOE_BLOCK_SUBS:
            count = (nsub - done) // size
            rows = size * MOE_SUB

            def body(i, carry, done=done, rows=rows):
                fn(pl.ds(pl.multiple_of(done * MOE_SUB + i * rows, MOE_SUB), rows))
                return carry

            lax.fori_loop(0, count, body, 0)
            done = done + count * size

    @pl.when(s == 0)
    def _():
        def unpack(rows):
            word = xs_ref[rows, :]
            xa_ref[rows, :] = lax.bitcast_convert_type(word & jnp.uint32(0xFFFF0000), F32).astype(BF16)
            xb_ref[rows, :] = lax.bitcast_convert_type(word << 16, F32).astype(BF16)
            o_ref[rows, :] = jnp.broadcast_to(bd_ref[...], (rows.size, d))
        for_row_blocks(unpack)

    wgb_ref[...] = wg_ref[...].astype(BF16)
    wub_ref[...] = wu_ref[...].astype(BF16)
    wdb_ref[...] = wd_ref[...].astype(BF16)
    bg = bg_ref[...]
    bu = bu_ref[...]

    def step(rows):
        xa = xa_ref[rows, :]
        xb = xb_ref[rows, :]
        gate = _dot(xa, wgb_ref[:half, :]) + _dot(xb, wgb_ref[half:, :]) + bg
        upv = _dot(xa, wub_ref[:half, :]) + _dot(xb, wub_ref[half:, :]) + bu
        gate = jnp.minimum(gate, SWIGLU_LIMIT)
        upv = jnp.clip(upv, -SWIGLU_LIMIT, SWIGLU_LIMIT)
        act = (gate * _sigmoid(SWIGLU_ALPHA * gate) * (upv + 1.0)).astype(BF16)
        for n0 in range(0, d, MOE_TN):
            o_ref[rows, n0:n0 + MOE_TN] += _dot(act, wdb_ref[:, n0:n0 + MOE_TN])
    for_row_blocks(step)


def _moe(xs, n_used, chunk_expert, chunk_block, chunk_nsub, w_up, b_up, w_down, b_down):
    ne, d, dff2 = w_up.shape
    dff = dff2 // 2
    rb = MOE_SUB * MOE_CHUNK_SUBS
    n_chunks = xs.shape[0] // rb
    tf = MOE_TF
    nf = dff // tf
    assert dff % tf == 0 and d % MOE_TN == 0

    grid_spec = pltpu.PrefetchScalarGridSpec(
        num_scalar_prefetch=3,
        grid=(n_used, nf),
        in_specs=[
            pl.BlockSpec((rb, d // 2), lambda c, s, ce, cb, ns: (cb[c], 0)),
            pl.BlockSpec((None, d, tf), lambda c, s, ce, cb, ns: (ce[c], 0, s)),
            pl.BlockSpec((None, d, tf), lambda c, s, ce, cb, ns: (ce[c], 0, nf + s)),
            pl.BlockSpec((None, 1, tf), lambda c, s, ce, cb, ns: (ce[c], 0, s)),
            pl.BlockSpec((None, 1, tf), lambda c, s, ce, cb, ns: (ce[c], 0, nf + s)),
            pl.BlockSpec((None, tf, d), lambda c, s, ce, cb, ns: (ce[c], s, 0)),
            pl.BlockSpec((None, 1, d), lambda c, s, ce, cb, ns: (ce[c], 0, 0)),
        ],
        out_specs=pl.BlockSpec((rb, d), lambda c, s, ce, cb, ns: (cb[c], 0)),
        scratch_shapes=[pltpu.VMEM((rb, d // 2), BF16), pltpu.VMEM((rb, d // 2), BF16),
                        pltpu.VMEM((d, tf), BF16), pltpu.VMEM((d, tf), BF16), pltpu.VMEM((tf, d), BF16)],
    )
    return pl.pallas_call(
        _moe_kernel,
        out_shape=jax.ShapeDtypeStruct((n_chunks * rb, d), F32),
        grid_spec=grid_spec,
        compiler_params=_cparams(("arbitrary", "arbitrary"), 56),
        name="moe",
    )(chunk_expert, chunk_block, chunk_nsub, xs, w_up, w_up, b_up.reshape(ne, 1, dff2), b_up.reshape(ne, 1, dff2),
      w_down, b_down.reshape(ne, 1, d))


def _combine_kernel(dest_ref, dest_next_ref, hres_ref, gate_ref, yb_ref, o_ref, buf_ref, sem):
    i = pl.program_id(0)
    tm = hres_ref.shape[0]
    slot = i % 2

    def row_copy(dref, s, t, k):
        return pltpu.make_async_copy(yb_ref.at[pl.ds(dref[0, 0, k * tm + t], 1)],
                                     buf_ref.at[s, k, pl.ds(t, 1)], sem.at[s])

    def issue(dref, s):
        def body(t, c):
            for k in range(TOP_K):
                row_copy(dref, s, t, k).start()
            return c
        lax.fori_loop(0, tm, body, 0)

    def drain(dref, s):
        def body(t, c):
            for k in range(TOP_K):
                row_copy(dref, s, t, k).wait()
            return c
        lax.fori_loop(0, tm, body, 0)

    pl.when(i == 0)(lambda: issue(dest_ref, slot))
    pl.when(i + 1 < pl.num_programs(0))(lambda: issue(dest_next_ref, 1 - slot))
    drain(dest_ref, slot)
    g = gate_ref[...]
    y = hres_ref[...]
    for k in range(TOP_K):
        y = y + g[:, k:k + 1] * buf_ref[slot, k]
    o_ref[...] = y


def _combine(hres, gates, dest, yb, tm):
    r, d = hres.shape
    assert r % tm == 0
    nt = r // tm
    dest_t = dest.reshape(TOP_K, nt, tm).transpose(1, 0, 2).reshape(nt, 1, TOP_K * tm)
    return pl.pallas_call(
        _combine_kernel,
        out_shape=jax.ShapeDtypeStruct((r, d), F32),
        grid=(nt,),
        in_specs=[pl.BlockSpec((1, 1, TOP_K * tm), lambda i: (i, 0, 0), memory_space=pltpu.SMEM),
                  pl.BlockSpec((1, 1, TOP_K * tm), lambda i: (jnp.minimum(i + 1, nt - 1), 0, 0),
                               memory_space=pltpu.SMEM),
                  pl.BlockSpec((tm, d), lambda i: (i, 0)),
                  pl.BlockSpec((tm, TOP_K), lambda i: (i, 0)),
                  pl.BlockSpec(memory_space=pl.ANY)],
        out_specs=pl.BlockSpec((tm, d), lambda i: (i, 0)),
        scratch_shapes=[pltpu.VMEM((2, TOP_K, tm, d), F32), pltpu.SemaphoreType.DMA((2,))],
        compiler_params=_cparams(("arbitrary",), 40),
        name="combine",
    )(dest_t, dest_t, hres, gates.T, yb)


def _row_tile(r, cap):
    tm = min(r, cap)
    assert r % tm == 0
    return tm


def kernel(x_prompt, x_sample, mem_prompt, cache_sb_k, cache_sb_v, page_table, cache_mem_k, cache_mem_v, state_conv, state_lru, norm_mix_g, norm_mem_g, w_in, q_sb_g, k_sb_g, sb_bias, q_mem_g, k_mem_g, w_mem_kv, conv_w, conv_b, lru_wa, lru_ba, lru_wx, lru_bx, lru_lambda, p_attn, p_lru, p_mem, w_o, norm_ffn_g, w_router, b_router, w_up, b_up, w_down, b_down):
    bp, t, d = x_prompt.shape
    bs, ts, _ = x_sample.shape
    assert ts == 1, "sample group decodes one token per sequence"
    n_mem = mem_prompt.shape[1]
    q_gain = jnp.tile(q_sb_g, SB_HEADS)
    k_gain = jnp.tile(k_sb_g, SB_HEADS)
    qm_gain = jnp.tile(q_mem_g, MEM_HEADS)
    heads_major = lambda a: jnp.transpose(a, (0, 2, 1, 3))

    memn = _rmsnorm(mem_prompt.reshape(bp * n_mem, d), norm_mem_g, _row_tile(bp * n_mem, 256))
    mem_tm = _row_tile(n_mem, 256)
    mem_k_hm = _proj(memn, w_mem_kv, jnp.tile(k_mem_g, MEM_HEADS), col0=0, ncols=MEM_W, norm_cols=(0, MEM_W),
                     tm=mem_tm, tn=MEM_W, heads_out=(bp, n_mem))
    mem_v_hm = _proj(memn, w_mem_kv, None, col0=MEM_W, ncols=MEM_W, tm=mem_tm, tn=MEM_W, heads_out=(bp, n_mem))

    rest_w = COL_GATE - COL_XL
    qm_cols = (COL_QM - COL_XL, rest_w)
    w_q = _cast_bf16(w_in, COL_Q, SB_W)
    w_k = _cast_bf16(w_in, COL_K, SB_W)
    w_v = _cast_bf16(w_in, COL_V, SB_W)
    w_rest = _cast_bf16(w_in, COL_XL, rest_w)
    w_gates = _cast_bf16(w_in, COL_GATE, N_BRANCH * d)
    rest_gain = jnp.concatenate([jnp.ones((2 * LRU_W,), F32), qm_gain])

    xp2 = x_prompt.reshape(bp * t, d)
    tm_p = _row_tile(t, 1024)
    xn_p = _rmsnorm(xp2, norm_mix_g, _row_tile(bp * t, 512))
    _, q_pb = _proj(xn_p, w_q, q_gain, ncols=SB_W, norm_cols=(0, SB_W), tm=tm_p, tn=SB_W, bf16_copy=True)
    k_p, k_pb = _proj(xn_p, w_k, k_gain, ncols=SB_W, norm_cols=(0, SB_W), tm=tm_p, tn=SB_W, heads_out=(bp, t),
                      bf16_copy=True)
    v_p, v_pb = _proj(xn_p, w_v, None, ncols=SB_W, tm=tm_p, tn=SB_W, heads_out=(bp, t), bf16_copy=True)
    rest_p = _proj(xn_p, w_rest, rest_gain, ncols=rest_w, norm_cols=qm_cols, tm=tm_p, tn=1024)
    gates_p = _proj(xn_p, w_gates, None, ncols=N_BRANCH * d, tm=tm_p, tn=1024)
    o_sb_p = _sb_prompt(q_pb, k_pb, v_pb, sb_bias)
    conv0 = jnp.zeros((bp, CONV_W - 1, LRU_W), x_prompt.dtype)
    lru0 = jnp.zeros((bp, LRU_W), state_lru.dtype)
    o_lru_p, lru_prompt = _lru_prompt(rest_p, conv0, lru0, conv_w, conv_b, lru_wa, lru_ba, lru_wx, lru_bx,
                                      lru_lambda, bp, t)
    o_mem_p = _mem_prompt(rest_p, COL_QM - COL_XL, mem_k_hm, mem_v_hm, t)
    xl_tail = rest_p.reshape(bp, t, -1)[:, t - (CONV_W - 1):, :LRU_W]
    conv_prompt = jnp.concatenate([conv0, xl_tail], axis=1)[:, -(CONV_W - 1):]

    xs2 = x_sample.reshape(bs, d)
    xn_s = _rmsnorm(xs2, norm_mix_g, bs)
    q_s = _proj(xn_s, w_q, q_gain, ncols=SB_W, norm_cols=(0, SB_W), tm=bs, tn=SB_W)
    k_s = _proj(xn_s, w_k, k_gain, ncols=SB_W, norm_cols=(0, SB_W), tm=bs, tn=SB_W)
    v_s = _proj(xn_s, w_v, None, ncols=SB_W, tm=bs, tn=SB_W)
    rest_s = _proj(xn_s, w_rest, rest_gain, ncols=rest_w, norm_cols=qm_cols, tm=bs, tn=1024)
    gates_s = _proj(xn_s, w_gates, None, ncols=N_BRANCH * d, tm=bs, tn=1024)
    o_sb_s = _sb_sample(q_s, sb_bias, heads_major(cache_sb_k), heads_major(cache_sb_v), page_table)
    xl_s = rest_s[:, :LRU_W]
    o_lru_s, lru_sample = _lru_step(xl_s, rest_s[:, LRU_W:2 * LRU_W], state_conv, state_lru, conv_w, conv_b,
                                    lru_wa, lru_ba, lru_wx, lru_bx, lru_lambda)
    conv_sample = jnp.concatenate([state_conv, xl_s[:, None, :]], axis=1)[:, -(CONV_W - 1):]
    o_mem_s = _mem_sample(rest_s[:, 2 * LRU_W:], cache_mem_k, cache_mem_v)

    wo_b = _cast_bf16(w_o)
    wr_t = w_router.T.astype(BF16)
    merged_p = _merge(o_sb_p, o_lru_p, o_mem_p, p_attn, p_lru, p_mem, gates_p, _row_tile(bp * t, 1024))
    merged_s = _merge(o_sb_s, o_lru_s, o_mem_s, p_attn, p_lru, p_mem, gates_s, _row_tile(bs, 1024))
    cnt0 = jnp.zeros((N_EXPERTS, 1), F32)
    hres_p, hnp_p, idx_p, gate_p, rank_p, cnt_p = _post(xp2, merged_p, wo_b, norm_ffn_g, wr_t, b_router,
                                                        cnt0, _row_tile(bp * t, 256))
    hres_s, hnp_s, idx_s, gate_s, rank_s, cnt_all = _post(xs2, merged_s, wo_b, norm_ffn_g, wr_t, b_router,
                                                          cnt_p, _row_tile(bs, 256))

    rb = MOE_SUB * MOE_CHUNK_SUBS
    n_assign = (bp * t + bs) * TOP_K
    n_chunks = -(-n_assign // rb) + N_EXPERTS
    counts = cnt_all[:, 0].astype(jnp.int32)
    chunks_e = (counts + rb - 1) // rb
    chunk_end = jnp.cumsum(chunks_e)
    chunk_start = chunk_end - chunks_e
    used = chunk_end[-1]
    cidx = jnp.arange(n_chunks, dtype=jnp.int32)
    last = jnp.minimum(cidx, used - 1)
    chunk_expert = jnp.clip(jnp.searchsorted(chunk_end, last, side='right'), 0, N_EXPERTS - 1).astype(jnp.int32)
    rows_left = counts[chunk_expert] - (last - chunk_start[chunk_expert]) * rb
    chunk_nsub = jnp.where(cidx < used, (jnp.clip(rows_left, 0, rb) + MOE_SUB - 1) // MOE_SUB, 0).astype(jnp.int32)
    chunk_block = last.astype(jnp.int32)
    row_start = chunk_start * rb
    experts = jnp.arange(N_EXPERTS, dtype=jnp.int32)

    def slots(idx, rank):
        return jnp.sum(jnp.where(idx[..., None] == experts, row_start, 0), axis=-1) + rank

    dest_p = slots(idx_p, rank_p)
    dest_s = slots(idx_s, rank_s)
    tail = jnp.where(counts % MOE_SUB != 0, row_start + counts // MOE_SUB * MOE_SUB, -1).astype(jnp.int32)

    hnp = jnp.concatenate([hnp_p, hnp_s], axis=0)
    dest = jnp.concatenate([dest_p, dest_s], axis=1)
    xs = _dispatch(hnp, dest, tail, n_chunks * rb, _row_tile(bp * t + bs, 128))
    yb = _moe(xs, used.astype(jnp.int32), chunk_expert, chunk_block, chunk_nsub, w_up, b_up, w_down, b_down)
    y_prompt = _combine(hres_p, gate_p, dest_p, yb, _row_tile(bp * t, 256)).reshape(bp, t, d)
    y_sample = _combine(hres_s, gate_s, dest_s, yb, _row_tile(bs, 256)).reshape(bs, 1, d)

    sb_k_sample = k_s.reshape(bs, 1, SB_HEADS, HEAD_DIM)
    sb_v_sample = v_s.reshape(bs, 1, SB_HEADS, HEAD_DIM)
    return (y_prompt, y_sample, heads_major(k_p), heads_major(v_p), heads_major(mem_k_hm), heads_major(mem_v_hm),
            conv_prompt, lru_prompt, sb_k_sample, sb_v_sample, conv_sample, lru_sample)
```

```python
import functools

import jax
import jax.numpy as jnp
from jax import lax
from jax.experimental import pallas as pl
from jax.experimental.pallas import tpu as pltpu

F32 = jnp.float32
BF16 = jnp.bfloat16

HEAD_DIM = 128
SB_HEADS = 6
SB_W = SB_HEADS * HEAD_DIM
LRU_HEADS = 6
LRU_W = LRU_HEADS * HEAD_DIM
LRU_C = 8.0
CONV_W = 4
MEM_HEADS = 4
MEM_W = MEM_HEADS * HEAD_DIM
N_BRANCH = 3
N_EXPERTS = 32
TOP_K = 4
SWIGLU_LIMIT = 7.0
SWIGLU_ALPHA = 1.702
EPS = 1e-6
QK_SCALE = HEAD_DIM ** -0.5

COL_Q, COL_K, COL_V = 0, SB_W, 2 * SB_W
COL_XL, COL_GL = 3 * SB_W, 3 * SB_W + LRU_W
COL_QM = 3 * SB_W + 2 * LRU_W
COL_GATE = COL_QM + MEM_W

MIB = 1024 * 1024
PROJ_TN = 256
SB_BLK = 256
SB_HEADS_PER_STEP = 6
LRU_TT = 256
MOE_SUB = 128
MOE_BLOCK_SUBS = (9, 8, 1)
MOE_CHUNK_SUBS = 9
MOE_TF = 256
MOE_TN = 512
DMA_QUEUES = 2


def _cparams(sem, vmem_mib):
    return pltpu.CompilerParams(dimension_semantics=sem, vmem_limit_bytes=vmem_mib * MIB)


def _softplus(z):
    return jnp.maximum(z, 0.0) + jnp.log1p(jnp.exp(-jnp.abs(z)))


def _softplus_fast(z):
    return jnp.maximum(z, 0.0) + jnp.log(1.0 + jnp.exp(-jnp.abs(z)))


def _sigmoid(z):
    return 1.0 / (1.0 + jnp.exp(-z))


def _split_bf16(x):
    hi = x.astype(BF16)
    lo = (x - hi.astype(F32)).astype(BF16)
    return hi, lo


def _dot(a, b):
    return jnp.dot(a, b, preferred_element_type=F32)


def _dot_nt(a, b):
    return lax.dot_general(a, b, (((1,), (1,)), ((), ())), preferred_element_type=F32)


def _rmsnorm_kernel(x_ref, g_ref, o_ref):
    x = x_ref[...]
    ms = jnp.mean(x * x, axis=-1, keepdims=True)
    o_ref[...] = (x * lax.rsqrt(ms + EPS) * g_ref[...]).astype(o_ref.dtype)


def _rmsnorm(x, g, tm):
    r, d = x.shape
    assert r % tm == 0
    return pl.pallas_call(
        _rmsnorm_kernel,
        out_shape=jax.ShapeDtypeStruct((r, d), BF16),
        grid=(r // tm,),
        in_specs=[pl.BlockSpec((tm, d), lambda i: (i, 0)), pl.BlockSpec((1, d), lambda i: (0, 0))],
        out_specs=pl.BlockSpec((tm, d), lambda i: (i, 0)),
        compiler_params=_cparams(("parallel",), 32),
        name="rmsnorm",
    )(x, g.reshape(1, d))


def _cast_kernel(x_ref, o_ref):
    o_ref[...] = x_ref[...].astype(o_ref.dtype)


def _cast_bf16(w, col0=0, ncols=None):
    r, c = w.shape
    ncols = c - col0 if ncols is None else ncols
    tc = PROJ_TN
    assert col0 % tc == 0 and ncols % tc == 0
    c0 = col0 // tc
    return pl.pallas_call(
        _cast_kernel,
        out_shape=jax.ShapeDtypeStruct((r, ncols), BF16),
        grid=(ncols // tc,),
        in_specs=[pl.BlockSpec((r, tc), lambda j: (0, j + c0))],
        out_specs=pl.BlockSpec((r, tc), lambda j: (0, j)),
        compiler_params=_cparams(("parallel",), 32),
        name="cast_bf16",
    )(w)


def _proj_kernel(xn_ref, w_ref, gain_ref, *o_refs, patterns, heads_out):
    j = pl.program_id(1)
    y = _dot(xn_ref[...], w_ref[...].astype(BF16))
    nh = y.shape[1] // HEAD_DIM

    def store(c, val):
        for o_ref in o_refs:
            if heads_out:
                o_ref[c] = val.astype(o_ref.dtype)
            else:
                o_ref[:, c * HEAD_DIM:(c + 1) * HEAD_DIM] = val.astype(o_ref.dtype)

    def emit(pattern):
        gain = gain_ref[...] if any(pattern) else None
        for c in range(nh):
            yc = y[:, c * HEAD_DIM:(c + 1) * HEAD_DIM]
            if pattern[c]:
                ms = jnp.mean(yc * yc, axis=-1, keepdims=True)
                yc = yc * lax.rsqrt(ms + EPS) * gain[:, c * HEAD_DIM:(c + 1) * HEAD_DIM]
            store(c, yc)

    distinct = sorted(set(patterns))
    if len(distinct) == 1:
        emit(distinct[0])
    else:
        for pattern in distinct:
            tiles = [jj for jj, p in enumerate(patterns) if p == pattern]
            pred = functools.reduce(jnp.logical_or, [j == jj for jj in tiles])
            pl.when(pred)(functools.partial(emit, pattern))


def _proj(xn, w, gain, *, col0=0, ncols, norm_cols=(0, 0), tm, tn, heads_out=None, bf16_copy=False):
    r, d = xn.shape
    assert r % tm == 0 and col0 % tn == 0 and ncols % tn == 0 and tn % HEAD_DIM == 0
    assert norm_cols[0] % HEAD_DIM == 0 and norm_cols[1] % HEAD_DIM == 0
    col_tile0 = col0 // tn
    patterns = tuple(tuple(norm_cols[0] <= jj * tn + c * HEAD_DIM < norm_cols[1] for c in range(tn // HEAD_DIM))
                     for jj in range(ncols // tn))
    if gain is None:
        gain = jnp.ones((ncols,), F32)
    if heads_out is None:
        out_shape = jax.ShapeDtypeStruct((r, ncols), F32)
        out_spec = pl.BlockSpec((tm, tn), lambda i, j: (i, j))
    else:
        bsz, t = heads_out
        assert bsz * t == r and t % tm == 0
        nt = t // tm
        hpt = tn // HEAD_DIM
        out_shape = jax.ShapeDtypeStruct((bsz, ncols // HEAD_DIM, t, HEAD_DIM), F32)
        out_spec = pl.BlockSpec((None, hpt, tm, HEAD_DIM), lambda i, j: (i // nt, j, i % nt, 0))
    kern = functools.partial(_proj_kernel, patterns=patterns, heads_out=heads_out is not None)
    if bf16_copy:
        out_shape = (out_shape, jax.ShapeDtypeStruct(out_shape.shape, BF16))
        out_spec = (out_spec, out_spec)
    return pl.pallas_call(
        kern,
        out_shape=out_shape,
        grid=(r // tm, ncols // tn),
        in_specs=[
            pl.BlockSpec((tm, d), lambda i, j: (i, 0)),
            pl.BlockSpec((d, tn), lambda i, j: (0, j + col_tile0)),
            pl.BlockSpec((1, tn), lambda i, j: (0, j)),
        ],
        out_specs=out_spec,
        compiler_params=_cparams(("parallel", "parallel"), 40),
        name="proj",
    )(xn, w, gain.reshape(1, ncols))


def _sb_prompt_kernel(bias_ref, q_ref, kb_ref, vb_ref, o_ref):
    hg = pl.program_id(1)
    qi = pl.program_id(2)
    blk = q_ref.shape[0]
    nh = kb_ref.shape[0]

    qs = [q_ref[:, j * HEAD_DIM:(j + 1) * HEAD_DIM] for j in range(nh)]
    biases = [bias_ref[hg * nh + j] for j in range(nh)]
    row = lax.broadcasted_iota(jnp.int32, (blk, blk), 0)
    col = lax.broadcasted_iota(jnp.int32, (blk, blk), 1)
    tri = jnp.where(row >= col, 1.0, 0.0).astype(BF16)
    causal = col < row

    def block(kb, state, masked):
        start = pl.multiple_of(kb * blk, blk)
        heads = range(nh)
        zs = [_dot_nt(qs[j], kb_ref[j, pl.ds(start, blk), :]) * QK_SCALE + biases[j] for j in heads]
        drops = [_softplus_fast(z) for z in zs]
        if masked:
            drops = [jnp.where(causal, dr, 0.0) for dr in drops]
        splits = [_split_bf16(dr) for dr in drops]
        suffixes = [_dot(hi, tri) + _dot(lo, tri) for hi, lo in splits]
        ws = [jnp.exp(zs[j] - suffixes[j] - state[j][0]) for j in heads]
        if masked:
            ws = [jnp.where(causal, w, 0.0) for w in ws]
        pvs = [_dot(ws[j].astype(BF16), vb_ref[j, pl.ds(start, blk), :]) for j in heads]
        return tuple((state[j][0] + suffixes[j][:, 0:1], state[j][1] + pvs[j]) for j in heads)

    zero = (jnp.zeros((blk, 1), F32), jnp.zeros((blk, HEAD_DIM), F32))
    state = block(qi, (zero,) * nh, True)
    state = lax.fori_loop(0, qi, lambda it, st: block(qi - 1 - it, st, False), state)
    for j in range(nh):
        o_ref[:, j * HEAD_DIM:(j + 1) * HEAD_DIM] = state[j][1].astype(o_ref.dtype)


def _sb_prompt(q, k, v, sb_bias):
    bsz, heads, t, _ = k.shape
    blk = SB_BLK
    nh = SB_HEADS_PER_STEP
    assert t % blk == 0 and heads % nh == 0
    assert q.dtype == BF16 and k.dtype == BF16 and v.dtype == BF16
    nq = t // blk
    kv_spec = pl.BlockSpec((None, nh, t, HEAD_DIM), lambda b, h, i: (b, h, 0, 0))
    return pl.pallas_call(
        _sb_prompt_kernel,
        out_shape=jax.ShapeDtypeStruct((bsz * t, SB_W), BF16),
        grid=(bsz, heads // nh, nq),
        in_specs=[
            pl.BlockSpec(memory_space=pltpu.SMEM),
            pl.BlockSpec((blk, nh * HEAD_DIM), lambda b, h, i: (b * nq + i, h)),
            kv_spec, kv_spec,
        ],
        out_specs=pl.BlockSpec((blk, nh * HEAD_DIM), lambda b, h, i: (b * nq + i, h)),
        compiler_params=_cparams(("parallel", "parallel", "parallel"), 40),
        name="sb_prompt",
    )(sb_bias, q, k, v)


def _sb_sample_kernel(pt_ref, bias_ref, q_ref, *refs, n_pages):
    del pt_ref
    k_refs = refs[:n_pages]
    v_refs = refs[n_pages:2 * n_pages]
    o_ref = refs[2 * n_pages]
    heads, page = k_refs[0].shape[:2]
    rep = 8
    n = n_pages * rep

    row = lax.broadcasted_iota(jnp.int32, (page, page), 0)
    col = lax.broadcasted_iota(jnp.int32, (page, page), 1)
    tri = jnp.where(row >= col, 1.0, 0.0).astype(BF16)
    rn = lax.broadcasted_iota(jnp.int32, (n, n), 0)
    cn = lax.broadcasted_iota(jnp.int32, (n, n), 1)
    later = jnp.where(((rn % rep) == (cn % rep)) & (cn // rep > rn // rep), 1.0, 0.0).astype(BF16)
    q = q_ref[0]
    zs = []
    for h in range(heads):
        q8 = jnp.broadcast_to(q[h:h + 1, :], (rep, HEAD_DIM)).astype(BF16)
        z = jnp.concatenate([_dot_nt(q8, k_refs[p][h].astype(BF16)) for p in range(n_pages)], axis=0)
        zs.append(z * QK_SCALE + bias_ref[h])
    z = jnp.concatenate(zs, axis=0)
    hi, lo = _split_bf16(-_softplus_fast(z))
    suffix = _dot(hi, tri) + _dot(lo, tri)
    shi, slo = _split_bf16(suffix)
    carry = jnp.concatenate(
        [(_dot(later, shi[h * n:(h + 1) * n]) + _dot(later, slo[h * n:(h + 1) * n]))[:, 0:1] for h in range(heads)],
        axis=0)
    w = jnp.exp(z + suffix + carry).astype(BF16)
    outs = []
    for h in range(heads):
        acc = jnp.zeros((rep, HEAD_DIM), F32)
        for p in range(n_pages):
            r0 = h * n + p * rep
            acc = acc + _dot(w[r0:r0 + rep, :], v_refs[p][h].astype(BF16))
        outs.append(acc[0:1, :])
    o_ref[0] = jnp.concatenate(outs, axis=0).astype(o_ref.dtype)


def _sb_sample(q, sb_bias, cache_k, cache_v, page_table):
    bs, n_pages = page_table.shape
    heads, page = cache_k.shape[1:3]
    assert heads == SB_HEADS and cache_k.shape[3] == HEAD_DIM

    def page_spec(p):
        return pl.BlockSpec((None, heads, page, HEAD_DIM), lambda b, pt: (pt[b, p], 0, 0, 0))

    grid_spec = pltpu.PrefetchScalarGridSpec(
        num_scalar_prefetch=1,
        grid=(bs,),
        in_specs=[pl.BlockSpec(memory_space=pltpu.SMEM),
                  pl.BlockSpec((1, heads, HEAD_DIM), lambda b, pt: (b, 0, 0))]
                 + [page_spec(p) for p in range(n_pages)] * 2,
        out_specs=pl.BlockSpec((1, heads, HEAD_DIM), lambda b, pt: (b, 0, 0)),
    )
    out = pl.pallas_call(
        functools.partial(_sb_sample_kernel, n_pages=n_pages),
        out_shape=jax.ShapeDtypeStruct((bs, heads, HEAD_DIM), BF16),
        grid_spec=grid_spec,
        compiler_params=_cparams(("arbitrary",), 40),
        name="sb_sample",
    )(page_table, sb_bias, q.reshape(bs, heads, HEAD_DIM), *([cache_k] * n_pages), *([cache_v] * n_pages))
    return out.reshape(bs, SB_W)


def _lru_gates(xc, wa_ref, ba, wx_ref, bx, nsp_lambda):
    xb = xc.astype(BF16)
    r_parts, i_parts = [], []
    for h in range(LRU_HEADS):
        xh = xb[:, h * HEAD_DIM:(h + 1) * HEAD_DIM]
        r_parts.append(_dot(xh, wa_ref[h].astype(BF16)))
        i_parts.append(_dot(xh, wx_ref[h].astype(BF16)))
    r = _sigmoid(jnp.concatenate(r_parts, axis=1) + ba)
    i = _sigmoid(jnp.concatenate(i_parts, axis=1) + bx)
    log_a = -LRU_C * r * nsp_lambda
    a = jnp.exp(log_a)
    u = jnp.sqrt(1.0 - jnp.exp(2.0 * log_a)) * (i * xc)
    return a, u


def _gelu_tanh(x):
    return 0.5 * x * (1.0 + jnp.tanh(0.7978845608028654 * (x + 0.044715 * (x * x * x))))


def _lru_prompt_kernel(x_ref, gate_ref, prev_ref, h0_ref, cw_ref, cb_ref, wa_ref, ba_ref, wx_ref, bx_ref,
                       lam_ref, o_ref, hlast_ref, xp_ref, h_ref):
    ti = pl.program_id(1)
    tt = x_ref.shape[0]
    pad = 8

    @pl.when(ti == 0)
    def _():
        xp_ref[pad - (CONV_W - 1):pad, :] = prev_ref[0]
        h_ref[...] = h0_ref[0]

    x = x_ref[...]
    xp_ref[pad:pad + tt, :] = x
    cw = cw_ref[...]
    xc = cb_ref[...] + cw[CONV_W - 1:CONV_W, :] * x
    for j in range(CONV_W - 1):
        xc = xc + cw[j:j + 1, :] * xp_ref[pad - (CONV_W - 1) + j:pad - (CONV_W - 1) + j + tt, :]
    xp_ref[pad - (CONV_W - 1):pad, :] = x[tt - (CONV_W - 1):tt, :]

    nsp = _softplus(-lam_ref[...])
    a, u = _lru_gates(xc, wa_ref, ba_ref[...], wx_ref, bx_ref[...], nsp)
    rows = lax.broadcasted_iota(jnp.int32, (tt, LRU_W), 0)
    b = jnp.where(rows == 0, u + a * h_ref[...], u)
    s = 1
    while s < tt:
        keep = rows >= s
        b = jnp.where(keep, a * pltpu.roll(b, s, 0) + b, b)
        if 2 * s < tt:
            a = jnp.where(keep, a * pltpu.roll(a, s, 0), a)
        s *= 2
    h_ref[...] = b[tt - 1:tt, :]
    o_ref[...] = (b * _gelu_tanh(gate_ref[...])).astype(o_ref.dtype)

    @pl.when(ti == pl.num_programs(1) - 1)
    def _():
        hlast_ref[0] = b[tt - 1:tt, :]


def _lru_prompt(proj, conv_prev, h0, conv_w, conv_b, lru_wa, lru_ba, lru_wx, lru_bx, lru_lambda, bsz, t):
    tt = LRU_TT
    assert t % tt == 0
    nt = t // tt
    xl, gl = 0, 1
    full = lambda shape: pl.BlockSpec(shape, lambda b, i: (0,) * len(shape))
    o_lru, h_last = pl.pallas_call(
        _lru_prompt_kernel,
        out_shape=(jax.ShapeDtypeStruct((bsz * t, LRU_W), BF16),
                   jax.ShapeDtypeStruct((bsz, 1, LRU_W), F32)),
        grid=(bsz, nt),
        in_specs=[
            pl.BlockSpec((tt, LRU_W), lambda b, i: (b * nt + i, xl)),
            pl.BlockSpec((tt, LRU_W), lambda b, i: (b * nt + i, gl)),
            pl.BlockSpec((1, CONV_W - 1, LRU_W), lambda b, i: (b, 0, 0)),
            pl.BlockSpec((1, 1, LRU_W), lambda b, i: (b, 0, 0)),
            full((CONV_W, LRU_W)), full((1, LRU_W)),
            full((LRU_HEADS, HEAD_DIM, HEAD_DIM)), full((1, LRU_W)),
            full((LRU_HEADS, HEAD_DIM, HEAD_DIM)), full((1, LRU_W)),
            full((1, LRU_W)),
        ],
        out_specs=(pl.BlockSpec((tt, LRU_W), lambda b, i: (b * nt + i, 0)),
                   pl.BlockSpec((1, 1, LRU_W), lambda b, i: (b, 0, 0))),
        scratch_shapes=[pltpu.VMEM((8 + tt, LRU_W), F32), pltpu.VMEM((1, LRU_W), F32)],
        compiler_params=_cparams(("parallel", "arbitrary"), 32),
        name="lru_prompt",
    )(proj, proj, conv_prev, h0.reshape(bsz, 1, LRU_W), conv_w, conv_b.reshape(1, LRU_W),
      lru_wa, lru_ba.reshape(1, LRU_W), lru_wx, lru_bx.reshape(1, LRU_W), lru_lambda.reshape(1, LRU_W))
    return o_lru, h_last.reshape(bsz, LRU_W)


def _lru_step_kernel(x_ref, gate_ref, p0_ref, p1_ref, p2_ref, h0_ref, cw_ref, cb_ref, wa_ref, ba_ref, wx_ref,
                     bx_ref, lam_ref, o_ref, h_ref):
    cw = cw_ref[...]
    xc = (cb_ref[...] + cw[0:1, :] * p0_ref[...] + cw[1:2, :] * p1_ref[...] + cw[2:3, :] * p2_ref[...]
          + cw[3:4, :] * x_ref[...])
    nsp = _softplus(-lam_ref[...])
    a, u = _lru_gates(xc, wa_ref, ba_ref[...], wx_ref, bx_ref[...], nsp)
    h = u + a * h0_ref[...]
    h_ref[...] = h
    o_ref[...] = (h * _gelu_tanh(gate_ref[...])).astype(o_ref.dtype)


def _lru_step(x_lru, gate_lru, state_conv, h0, conv_w, conv_b, lru_wa, lru_ba, lru_wx, lru_bx, lru_lambda):
    bs = x_lru.shape[0]
    assert CONV_W == 4
    prevs = [state_conv[:, j, :] for j in range(CONV_W - 1)]
    o_lru, h_new = pl.pallas_call(
        _lru_step_kernel,
        out_shape=(jax.ShapeDtypeStruct((bs, LRU_W), BF16), jax.ShapeDtypeStruct((bs, LRU_W), F32)),
        name="lru_step",
    )(x_lru, gate_lru, *prevs, h0, conv_w, conv_b.reshape(1, LRU_W), lru_wa, lru_ba.reshape(1, LRU_W),
      lru_wx, lru_bx.reshape(1, LRU_W), lru_lambda.reshape(1, LRU_W))
    return o_lru, h_new


def _mem_prompt_kernel(q_ref, k_ref, v_ref, o_ref):
    s = _dot_nt(q_ref[...].astype(BF16), k_ref[...].astype(BF16)) * QK_SCALE
    m = jnp.max(s, axis=-1, keepdims=True)
    e = jnp.exp(s - m)
    p = e / jnp.sum(e, axis=-1, keepdims=True)
    o_ref[...] = _dot(p.astype(BF16), v_ref[...].astype(BF16)).astype(o_ref.dtype)


def _mem_prompt(proj, qcol0, mem_k, mem_v, t):
    bsz, _, n_mem, _ = mem_k.shape
    tq = min(t, 512)
    assert t % tq == 0 and qcol0 % HEAD_DIM == 0
    nt = t // tq
    qc = qcol0 // HEAD_DIM
    kv_spec = pl.BlockSpec((None, None, n_mem, HEAD_DIM), lambda b, h, i: (b, h, 0, 0))
    return pl.pallas_call(
        _mem_prompt_kernel,
        out_shape=jax.ShapeDtypeStruct((bsz * t, MEM_W), BF16),
        grid=(bsz, MEM_HEADS, nt),
        in_specs=[pl.BlockSpec((tq, HEAD_DIM), lambda b, h, i: (b * nt + i, qc + h)), kv_spec, kv_spec],
        out_specs=pl.BlockSpec((tq, HEAD_DIM), lambda b, h, i: (b * nt + i, h)),
        compiler_params=_cparams(("parallel", "parallel", "parallel"), 32),
        name="mem_prompt",
    )(proj, mem_k, mem_v)


def _mem_sample_kernel(q_ref, k_ref, v_ref, o_ref):
    group = q_ref.shape[0]
    for g in range(group):
        q = q_ref[g]
        s = jnp.sum(k_ref[g] * q[None], axis=-1, keepdims=True) * QK_SCALE
        m = jnp.max(s, axis=0, keepdims=True)
        e = jnp.exp(s - m)
        p = e / jnp.sum(e, axis=0, keepdims=True)
        o_ref[g] = jnp.sum(p * v_ref[g], axis=0).astype(o_ref.dtype)


def _mem_sample(q, cache_k, cache_v):
    bs, n_mem, heads, _ = cache_k.shape
    assert heads == MEM_HEADS and cache_k.shape[3] == HEAD_DIM
    group = 8
    assert bs % group == 0
    kv_spec = pl.BlockSpec((group, n_mem, heads, HEAD_DIM), lambda i: (i, 0, 0, 0))
    out = pl.pallas_call(
        _mem_sample_kernel,
        out_shape=jax.ShapeDtypeStruct((bs, heads, HEAD_DIM), BF16),
        grid=(bs // group,),
        in_specs=[pl.BlockSpec((group, heads, HEAD_DIM), lambda i: (i, 0, 0)), kv_spec, kv_spec],
        out_specs=pl.BlockSpec((group, heads, HEAD_DIM), lambda i: (i, 0, 0)),
        compiler_params=_cparams(("parallel",), 32),
        name="mem_sample",
    )(q.reshape(bs, heads, HEAD_DIM), cache_k, cache_v)
    return out.reshape(bs, MEM_W)


def _merge_kernel(oa_ref, ol_ref, om_ref, pa_ref, pl_ref, pm_ref, ga_ref, gl_ref, gm_ref, o_ref):
    ya = _dot(oa_ref[...], pa_ref[...].astype(BF16))
    yl = _dot(ol_ref[...], pl_ref[...].astype(BF16))
    ym = _dot(om_ref[...], pm_ref[...].astype(BF16))
    merged = _sigmoid(ga_ref[...]) * ya + _sigmoid(gl_ref[...]) * yl + _sigmoid(gm_ref[...]) * ym
    o_ref[...] = merged.astype(o_ref.dtype)


def _merge(o_sb, o_lru, o_mem, p_attn, p_lru, p_mem, gates, tm):
    r = o_sb.shape[0]
    d = p_attn.shape[1]
    tn = 512
    assert r % tm == 0 and d % tn == 0
    nj = d // tn
    return pl.pallas_call(
        _merge_kernel,
        out_shape=jax.ShapeDtypeStruct((r, d), BF16),
        grid=(r // tm, nj),
        in_specs=[
            pl.BlockSpec((tm, SB_W), lambda i, j: (i, 0)),
            pl.BlockSpec((tm, LRU_W), lambda i, j: (i, 0)),
            pl.BlockSpec((tm, MEM_W), lambda i, j: (i, 0)),
            pl.BlockSpec((SB_W, tn), lambda i, j: (0, j)),
            pl.BlockSpec((LRU_W, tn), lambda i, j: (0, j)),
            pl.BlockSpec((MEM_W, tn), lambda i, j: (0, j)),
            pl.BlockSpec((tm, tn), lambda i, j: (i, j)),
            pl.BlockSpec((tm, tn), lambda i, j: (i, nj + j)),
            pl.BlockSpec((tm, tn), lambda i, j: (i, 2 * nj + j)),
        ],
        out_specs=pl.BlockSpec((tm, tn), lambda i, j: (i, j)),
        compiler_params=_cparams(("parallel", "parallel"), 44),
        name="merge",
    )(o_sb, o_lru, o_mem, p_attn, p_lru, p_mem, gates, gates, gates)


def _post_kernel(x_ref, m_ref, wo_ref, g_ref, wr_ref, br_ref, cnt0_ref,
                 hres_ref, hnp_ref, idx_ref, gate_ref, rank_ref, cnt_ref):
    i = pl.program_id(0)
    tm, d = x_ref.shape

    @pl.when(i == 0)
    def _():
        cnt_ref[...] = cnt0_ref[...]

    hres = x_ref[...] + _dot(m_ref[...], wo_ref[...])
    hres_ref[...] = hres
    ms = jnp.mean(hres * hres, axis=-1, keepdims=True)
    hn = hres * lax.rsqrt(ms + EPS) * g_ref[...]
    hb = hn.astype(BF16)
    wa = lax.bitcast_convert_type(hb[:, :d // 2].astype(F32), jnp.uint32)
    wb = lax.bitcast_convert_type(hb[:, d // 2:].astype(F32), jnp.uint32)
    hnp_ref[...] = wa | (wb >> 16)

    logits = _dot_nt(wr_ref[...], hb) + br_ref[...]
    ne = logits.shape[0]
    eid = lax.broadcasted_iota(jnp.int32, (ne, tm), 0)
    work = logits
    vals, idxs, onehots = [], [], []
    for _ in range(TOP_K):
        mx = jnp.max(work, axis=0, keepdims=True)
        sel = jnp.min(jnp.where(work == mx, eid, ne), axis=0, keepdims=True)
        oh = eid == sel
        vals.append(mx)
        idxs.append(sel)
        onehots.append(oh)
        work = jnp.where(oh, -jnp.inf, work)
    es = [jnp.exp(v - vals[0]) for v in vals]
    den = functools.reduce(jnp.add, es)
    gate_ref[...] = jnp.concatenate([e / den for e in es], axis=0)
    idx_ref[...] = jnp.concatenate(idxs, axis=0)

    chosen = functools.reduce(jnp.logical_or, onehots)
    r_ = lax.broadcasted_iota(jnp.int32, (tm, tm), 0)
    c_ = lax.broadcasted_iota(jnp.int32, (tm, tm), 1)
    before = jnp.where(r_ < c_, 1.0, 0.0).astype(BF16)
    chosen_f = jnp.where(chosen, 1.0, 0.0)
    prior = _dot(chosen_f.astype(BF16), before) + cnt_ref[...]
    rank_ref[...] = jnp.concatenate(
        [jnp.sum(jnp.where(oh, prior, 0.0), axis=0, keepdims=True) for oh in onehots], axis=0).astype(jnp.int32)
    cnt_ref[...] = cnt_ref[...] + jnp.sum(chosen_f, axis=1, keepdims=True)


def _post(x, merged, wo_b, g, wr_t, b_router, cnt0, tm):
    r, d = x.shape
    ne = wr_t.shape[0]
    assert r % tm == 0
    full = lambda shape: pl.BlockSpec(shape, lambda i: (0,) * len(shape))
    return pl.pallas_call(
        _post_kernel,
        out_shape=(jax.ShapeDtypeStruct((r, d), F32),
                   jax.ShapeDtypeStruct((r, d // 2), jnp.uint32),
                   jax.ShapeDtypeStruct((TOP_K, r), jnp.int32),
                   jax.ShapeDtypeStruct((TOP_K, r), F32),
                   jax.ShapeDtypeStruct((TOP_K, r), jnp.int32),
                   jax.ShapeDtypeStruct((ne, 1), F32)),
        grid=(r // tm,),
        in_specs=[pl.BlockSpec((tm, d), lambda i: (i, 0)),
                  pl.BlockSpec((tm, d), lambda i: (i, 0)),
                  full((d, d)), full((1, d)), full((ne, d)), full((ne, 1)), full((ne, 1))],
        out_specs=(pl.BlockSpec((tm, d), lambda i: (i, 0)),
                   pl.BlockSpec((tm, d // 2), lambda i: (i, 0)),
                   pl.BlockSpec((TOP_K, tm), lambda i: (0, i)),
                   pl.BlockSpec((TOP_K, tm), lambda i: (0, i)),
                   pl.BlockSpec((TOP_K, tm), lambda i: (0, i)),
                   full((ne, 1))),
        compiler_params=_cparams(("arbitrary",), 48),
        name="post",
    )(x, merged, wo_b, g.reshape(1, d), wr_t, b_router.reshape(ne, 1), cnt0)


def _dispatch_kernel(tail_ref, dest_ref, hn_ref, xs_ref, zero_ref, sem, zsem):
    i = pl.program_id(0)
    tm = dest_ref.shape[2] // TOP_K

    @pl.when(i == 0)
    def _():
        zero_ref[...] = jnp.zeros_like(zero_ref)

        def tail_copy(e):
            start = pl.multiple_of(tail_ref[e], MOE_SUB)
            return pltpu.make_async_copy(zero_ref, xs_ref.at[pl.ds(start, MOE_SUB)], zsem)

        def zstart(e, c):
            pl.when(tail_ref[e] >= 0)(lambda: tail_copy(e).start())
            return c

        def zwait(e, c):
            pl.when(tail_ref[e] >= 0)(lambda: tail_copy(e).wait())
            return c

        lax.fori_loop(0, tail_ref.shape[0], zstart, 0)
        lax.fori_loop(0, tail_ref.shape[0], zwait, 0)

    def row_copy(t, k):
        return pltpu.make_async_copy(hn_ref.at[pl.ds(i * tm + t, 1)],
                                     xs_ref.at[pl.ds(dest_ref[0, 0, k * tm + t], 1)], sem)

    def start(t, c):
        for k in range(TOP_K):
            row_copy(t, k).start(priority=k % DMA_QUEUES)
        return c

    def wait(t, c):
        for k in range(TOP_K):
            row_copy(t, k).wait()
        return c

    def wait_tile():
        lax.fori_loop(0, tm, wait, 0)

    lax.fori_loop(0, tm, start, 0)
    pl.when(i > 0)(wait_tile)
    pl.when(i == pl.num_programs(0) - 1)(wait_tile)


def _dispatch(hnp, dest, tail, n_rows, tm):
    r, w = hnp.shape
    assert r % tm == 0
    nt = r // tm
    dest_t = dest.reshape(TOP_K, nt, tm).transpose(1, 0, 2).reshape(nt, 1, TOP_K * tm)
    grid_spec = pltpu.PrefetchScalarGridSpec(
        num_scalar_prefetch=1,
        grid=(nt,),
        in_specs=[pl.BlockSpec((1, 1, TOP_K * tm), lambda i, tl: (i, 0, 0), memory_space=pltpu.SMEM),
                  pl.BlockSpec(memory_space=pl.ANY)],
        out_specs=pl.BlockSpec(memory_space=pl.ANY),
        scratch_shapes=[pltpu.VMEM((MOE_SUB, w), hnp.dtype), pltpu.SemaphoreType.DMA, pltpu.SemaphoreType.DMA],
    )
    return pl.pallas_call(
        _dispatch_kernel,
        out_shape=jax.ShapeDtypeStruct((n_rows, w), hnp.dtype),
        grid_spec=grid_spec,
        compiler_params=_cparams(("arbitrary",), 16),
        name="dispatch",
    )(tail, dest_t, hnp)


def _moe_kernel(ce_ref, cb_ref, ns_ref, xs_ref, wg_ref, wu_ref, bg_ref, bu_ref, wd_ref, bd_ref, o_ref,
                xa_ref, xb_ref):
    del ce_ref, cb_ref
    c = pl.program_id(0)
    s = pl.program_id(1)
    nsub = ns_ref[c]
    half = xa_ref.shape[1]
    d = o_ref.shape[1]

    def for_row_blocks(fn):
        done = 0
        for size in MOE_BLOCK_SUBS:
            count = (nsub - done) // size
            rows = size * MOE_SUB

            def body(i, carry, done=done, rows=rows):
                fn(pl.ds(pl.multiple_of(done * MOE_SUB + i * rows, MOE_SUB), rows))
                return carry

            lax.fori_loop(0, count, body, 0)
            done = done + count * size

    @pl.when(s == 0)
    def _():
        def unpack(rows):
            word = xs_ref[rows, :]
            xa_ref[rows, :] = lax.bitcast_convert_type(word & jnp.uint32(0xFFFF0000), F32).astype(BF16)
            xb_ref[rows, :] = lax.bitcast_convert_type(word << 16, F32).astype(BF16)
            o_ref[rows, :] = jnp.broadcast_to(bd_ref[...], (rows.size, d))
        for_row_blocks(unpack)

    bg = bg_ref[...]
    bu = bu_ref[...]

    def step(rows):
        xa = xa_ref[rows, :]
        xb = xb_ref[rows, :]
        gate = (_dot(xa, wg_ref[:half, :].astype(BF16)) + _dot(xb, wg_ref[half:, :].astype(BF16)) + bg)
        upv = (_dot(xa, wu_ref[:half, :].astype(BF16)) + _dot(xb, wu_ref[half:, :].astype(BF16)) + bu)
        gate = jnp.minimum(gate, SWIGLU_LIMIT)
        upv = jnp.clip(upv, -SWIGLU_LIMIT, SWIGLU_LIMIT)
        act = (gate * _sigmoid(SWIGLU_ALPHA * gate) * (upv + 1.0)).astype(BF16)
        for n0 in range(0, d, MOE_TN):
            o_ref[rows, n0:n0 + MOE_TN] += _dot(act, wd_ref[:, n0:n0 + MOE_TN].astype(BF16))
    for_row_blocks(step)


def _moe(xs, n_used, chunk_expert, chunk_block, chunk_nsub, w_up, b_up, w_down, b_down):
    ne, d, dff2 = w_up.shape
    dff = dff2 // 2
    rb = MOE_SUB * MOE_CHUNK_SUBS
    n_chunks = xs.shape[0] // rb
    tf = MOE_TF
    nf = dff // tf
    assert dff % tf == 0 and d % MOE_TN == 0

    grid_spec = pltpu.PrefetchScalarGridSpec(
        num_scalar_prefetch=3,
        grid=(n_used, nf),
        in_specs=[
            pl.BlockSpec((rb, d // 2), lambda c, s, ce, cb, ns: (cb[c], 0)),
            pl.BlockSpec((None, d, tf), lambda c, s, ce, cb, ns: (ce[c], 0, s)),
            pl.BlockSpec((None, d, tf), lambda c, s, ce, cb, ns: (ce[c], 0, nf + s)),
            pl.BlockSpec((None, 1, tf), lambda c, s, ce, cb, ns: (ce[c], 0, s)),
            pl.BlockSpec((None, 1, tf), lambda c, s, ce, cb, ns: (ce[c], 0, nf + s)),
            pl.BlockSpec((None, tf, d), lambda c, s, ce, cb, ns: (ce[c], s, 0)),
            pl.BlockSpec((None, 1, d), lambda c, s, ce, cb, ns: (ce[c], 0, 0)),
        ],
        out_specs=pl.BlockSpec((rb, d), lambda c, s, ce, cb, ns: (cb[c], 0)),
        scratch_shapes=[pltpu.VMEM((rb, d // 2), BF16), pltpu.VMEM((rb, d // 2), BF16)],
    )
    return pl.pallas_call(
        _moe_kernel,
        out_shape=jax.ShapeDtypeStruct((n_chunks * rb, d), F32),
        grid_spec=grid_spec,
        compiler_params=_cparams(("arbitrary", "arbitrary"), 56),
        name="moe",
    )(chunk_expert, chunk_block, chunk_nsub, xs, w_up, w_up, b_up.reshape(ne, 1, dff2), b_up.reshape(ne, 1, dff2),
      w_down, b_down.reshape(ne, 1, d))


def _combine_kernel(dest_ref, dest_next_ref, hres_ref, gate_ref, yb_ref, o_ref, buf_ref, sem):
    i = pl.program_id(0)
    tm = hres_ref.shape[0]
    slot = i % 2

    def row_copy(dref, s, t, k):
        return pltpu.make_async_copy(yb_ref.at[pl.ds(dref[0, 0, k * tm + t], 1)],
                                     buf_ref.at[s, k, pl.ds(t, 1)], sem.at[s])

    def issue(dref, s):
        def body(t, c):
            for k in range(TOP_K):
                row_copy(dref, s, t, k).start(priority=k % DMA_QUEUES)
            return c
        lax.fori_loop(0, tm, body, 0)

    def drain(dref, s):
        def body(t, c):
            for k in range(TOP_K):
                row_copy(dref, s, t, k).wait()
            return c
        lax.fori_loop(0, tm, body, 0)

    pl.when(i == 0)(lambda: issue(dest_ref, slot))
    pl.when(i + 1 < pl.num_programs(0))(lambda: issue(dest_next_ref, 1 - slot))
    drain(dest_ref, slot)
    g = gate_ref[...]
    y = hres_ref[...]
    for k in range(TOP_K):
        y = y + g[:, k:k + 1] * buf_ref[slot, k]
    o_ref[...] = y


def _combine(hres, gates, dest, yb, tm):
    r, d = hres.shape
    assert r % tm == 0
    nt = r // tm
    dest_t = dest.reshape(TOP_K, nt, tm).transpose(1, 0, 2).reshape(nt, 1, TOP_K * tm)
    return pl.pallas_call(
        _combine_kernel,
        out_shape=jax.ShapeDtypeStruct((r, d), F32),
        grid=(nt,),
        in_specs=[pl.BlockSpec((1, 1, TOP_K * tm), lambda i: (i, 0, 0), memory_space=pltpu.SMEM),
                  pl.BlockSpec((1, 1, TOP_K * tm), lambda i: (jnp.minimum(i + 1, nt - 1), 0, 0),
                               memory_space=pltpu.SMEM),
                  pl.BlockSpec((tm, d), lambda i: (i, 0)),
                  pl.BlockSpec((tm, TOP_K), lambda i: (i, 0)),
                  pl.BlockSpec(memory_space=pl.ANY)],
        out_specs=pl.BlockSpec((tm, d), lambda i: (i, 0)),
        scratch_shapes=[pltpu.VMEM((2, TOP_K, tm, d), F32), pltpu.SemaphoreType.DMA((2,))],
        compiler_params=_cparams(("arbitrary",), 40),
        name="combine",
    )(dest_t, dest_t, hres, gates.T, yb)


def _row_tile(r, cap):
    tm = min(r, cap)
    assert r % tm == 0
    return tm


def kernel(x_prompt, x_sample, mem_prompt, cache_sb_k, cache_sb_v, page_table, cache_mem_k, cache_mem_v, state_conv, state_lru, norm_mix_g, norm_mem_g, w_in, q_sb_g, k_sb_g, sb_bias, q_mem_g, k_mem_g, w_mem_kv, conv_w, conv_b, lru_wa, lru_ba, lru_wx, lru_bx, lru_lambda, p_attn, p_lru, p_mem, w_o, norm_ffn_g, w_router, b_router, w_up, b_up, w_down, b_down):
    bp, t, d = x_prompt.shape
    bs, ts, _ = x_sample.shape
    assert ts == 1, "sample group decodes one token per sequence"
    n_mem = mem_prompt.shape[1]
    q_gain = jnp.tile(q_sb_g, SB_HEADS)
    k_gain = jnp.tile(k_sb_g, SB_HEADS)
    qm_gain = jnp.tile(q_mem_g, MEM_HEADS)
    heads_major = lambda a: jnp.transpose(a, (0, 2, 1, 3))

    memn = _rmsnorm(mem_prompt.reshape(bp * n_mem, d), norm_mem_g, _row_tile(bp * n_mem, 256))
    mem_tm = _row_tile(n_mem, 256)
    mem_k_hm = _proj(memn, w_mem_kv, jnp.tile(k_mem_g, MEM_HEADS), col0=0, ncols=MEM_W, norm_cols=(0, MEM_W),
                     tm=mem_tm, tn=MEM_W, heads_out=(bp, n_mem))
    mem_v_hm = _proj(memn, w_mem_kv, None, col0=MEM_W, ncols=MEM_W, tm=mem_tm, tn=MEM_W, heads_out=(bp, n_mem))

    rest_w = COL_GATE - COL_XL
    qm_cols = (COL_QM - COL_XL, rest_w)
    w_q = _cast_bf16(w_in, COL_Q, SB_W)
    w_k = _cast_bf16(w_in, COL_K, SB_W)
    w_v = _cast_bf16(w_in, COL_V, SB_W)
    w_rest = _cast_bf16(w_in, COL_XL, rest_w)
    w_gates = _cast_bf16(w_in, COL_GATE, N_BRANCH * d)
    rest_gain = jnp.concatenate([jnp.ones((2 * LRU_W,), F32), qm_gain])

    xp2 = x_prompt.reshape(bp * t, d)
    tm_p = _row_tile(t, 1024)
    xn_p = _rmsnorm(xp2, norm_mix_g, _row_tile(bp * t, 512))
    _, q_pb = _proj(xn_p, w_q, q_gain, ncols=SB_W, norm_cols=(0, SB_W), tm=tm_p, tn=SB_W, bf16_copy=True)
    k_p, k_pb = _proj(xn_p, w_k, k_gain, ncols=SB_W, norm_cols=(0, SB_W), tm=tm_p, tn=SB_W, heads_out=(bp, t),
                      bf16_copy=True)
    v_p, v_pb = _proj(xn_p, w_v, None, ncols=SB_W, tm=tm_p, tn=SB_W, heads_out=(bp, t), bf16_copy=True)
    rest_p = _proj(xn_p, w_rest, rest_gain, ncols=rest_w, norm_cols=qm_cols, tm=tm_p, tn=1024)
    gates_p = _proj(xn_p, w_gates, None, ncols=N_BRANCH * d, tm=tm_p, tn=1024)
    o_sb_p = _sb_prompt(q_pb, k_pb, v_pb, sb_bias)
    conv0 = jnp.zeros((bp, CONV_W - 1, LRU_W), x_prompt.dtype)
    lru0 = jnp.zeros((bp, LRU_W), state_lru.dtype)
    o_lru_p, lru_prompt = _lru_prompt(rest_p, conv0, lru0, conv_w, conv_b, lru_wa, lru_ba, lru_wx, lru_bx,
                                      lru_lambda, bp, t)
    o_mem_p = _mem_prompt(rest_p, COL_QM - COL_XL, mem_k_hm, mem_v_hm, t)
    xl_tail = rest_p.reshape(bp, t, -1)[:, t - (CONV_W - 1):, :LRU_W]
    conv_prompt = jnp.concatenate([conv0, xl_tail], axis=1)[:, -(CONV_W - 1):]

    xs2 = x_sample.reshape(bs, d)
    xn_s = _rmsnorm(xs2, norm_mix_g, bs)
    q_s = _proj(xn_s, w_q, q_gain, ncols=SB_W, norm_cols=(0, SB_W), tm=bs, tn=SB_W)
    k_s = _proj(xn_s, w_k, k_gain, ncols=SB_W, norm_cols=(0, SB_W), tm=bs, tn=SB_W)
    v_s = _proj(xn_s, w_v, None, ncols=SB_W, tm=bs, tn=SB_W)
    rest_s = _proj(xn_s, w_rest, rest_gain, ncols=rest_w, norm_cols=qm_cols, tm=bs, tn=1024)
    gates_s = _proj(xn_s, w_gates, None, ncols=N_BRANCH * d, tm=bs, tn=1024)
    o_sb_s = _sb_sample(q_s, sb_bias, heads_major(cache_sb_k), heads_major(cache_sb_v), page_table)
    xl_s = rest_s[:, :LRU_W]
    o_lru_s, lru_sample = _lru_step(xl_s, rest_s[:, LRU_W:2 * LRU_W], state_conv, state_lru, conv_w, conv_b,
                                    lru_wa, lru_ba, lru_wx, lru_bx, lru_lambda)
    conv_sample = jnp.concatenate([state_conv, xl_s[:, None, :]], axis=1)[:, -(CONV_W - 1):]
    o_mem_s = _mem_sample(rest_s[:, 2 * LRU_W:], cache_mem_k, cache_mem_v)

    wo_b = _cast_bf16(w_o)
    wr_t = w_router.T.astype(BF16)
    merged_p = _merge(o_sb_p, o_lru_p, o_mem_p, p_attn, p_lru, p_mem, gates_p, _row_tile(bp * t, 1024))
    merged_s = _merge(o_sb_s, o_lru_s, o_mem_s, p_attn, p_lru, p_mem, gates_s, _row_tile(bs, 1024))
    cnt0 = jnp.zeros((N_EXPERTS, 1), F32)
    hres_p, hnp_p, idx_p, gate_p, rank_p, cnt_p = _post(xp2, merged_p, wo_b, norm_ffn_g, wr_t, b_router,
                                                        cnt0, _row_tile(bp * t, 256))
    hres_s, hnp_s, idx_s, gate_s, rank_s, cnt_all = _post(xs2, merged_s, wo_b, norm_ffn_g, wr_t, b_router,
                                                          cnt_p, _row_tile(bs, 256))

    rb = MOE_SUB * MOE_CHUNK_SUBS
    n_assign = (bp * t + bs) * TOP_K
    n_chunks = -(-n_assign // rb) + N_EXPERTS
    counts = cnt_all[:, 0].astype(jnp.int32)
    chunks_e = (counts + rb - 1) // rb
    chunk_end = jnp.cumsum(chunks_e)
    chunk_start = chunk_end - chunks_e
    used = chunk_end[-1]
    cidx = jnp.arange(n_chunks, dtype=jnp.int32)
    last = jnp.minimum(cidx, used - 1)
    chunk_expert = jnp.clip(jnp.searchsorted(chunk_end, last, side='right'), 0, N_EXPERTS - 1).astype(jnp.int32)
    rows_left = counts[chunk_expert] - (last - chunk_start[chunk_expert]) * rb
    chunk_nsub = jnp.where(cidx < used, (jnp.clip(rows_left, 0, rb) + MOE_SUB - 1) // MOE_SUB, 0).astype(jnp.int32)
    chunk_block = last.astype(jnp.int32)
    row_start = chunk_start * rb
    experts = jnp.arange(N_EXPERTS, dtype=jnp.int32)

    def slots(idx, rank):
        return jnp.sum(jnp.where(idx[..., None] == experts, row_start, 0), axis=-1) + rank

    dest_p = slots(idx_p, rank_p)
    dest_s = slots(idx_s, rank_s)
    tail = jnp.where(counts % MOE_SUB != 0, row_start + counts // MOE_SUB * MOE_SUB, -1).astype(jnp.int32)

    hnp = jnp.concatenate([hnp_p, hnp_s], axis=0)
    dest = jnp.concatenate([dest_p, dest_s], axis=1)
    xs = _dispatch(hnp, dest, tail, n_chunks * rb, _row_tile(bp * t + bs, 128))
    yb = _moe(xs, used.astype(jnp.int32), chunk_expert, chunk_block, chunk_nsub, w_up, b_up, w_down, b_down)
    y_prompt = _combine(hres_p, gate_p, dest_p, yb, _row_tile(bp * t, 256)).reshape(bp, t, d)
    y_sample = _combine(hres_s, gate_s, dest_s, yb, _row_tile(bs, 256)).reshape(bs, 1, d)

    sb_k_sample = k_s.reshape(bs, 1, SB_HEADS, HEAD_DIM)
    sb_v_sample = v_s.reshape(bs, 1, SB_HEADS, HEAD_DIM)
    return (y_prompt, y_sample, heads_major(k_p), heads_major(v_p), heads_major(mem_k_hm), heads_major(mem_v_hm),
            conv_prompt, lru_prompt, sb_k_sample, sb_v_sample, conv_sample, lru_sample)
```

```python
import functools

import jax
import jax.numpy as jnp
from jax import lax
from jax.experimental import pallas as pl
from jax.experimental.pallas import tpu as pltpu

F32 = jnp.float32
BF16 = jnp.bfloat16

HEAD_DIM = 128
SB_HEADS = 6
SB_W = SB_HEADS * HEAD_DIM
LRU_HEADS = 6
LRU_W = LRU_HEADS * HEAD_DIM
LRU_C = 8.0
CONV_W = 4
MEM_HEADS = 4
MEM_W = MEM_HEADS * HEAD_DIM
N_BRANCH = 3
N_EXPERTS = 32
TOP_K = 4
SWIGLU_LIMIT = 7.0
SWIGLU_ALPHA = 1.702
EPS = 1e-6
QK_SCALE = HEAD_DIM ** -0.5

COL_Q, COL_K, COL_V = 0, SB_W, 2 * SB_W
COL_XL, COL_GL = 3 * SB_W, 3 * SB_W + LRU_W
COL_QM = 3 * SB_W + 2 * LRU_W
COL_GATE = COL_QM + MEM_W

MIB = 1024 * 1024
PROJ_TN = 256
SB_BLK = 256
SB_HEADS_PER_STEP = 6
LRU_TT = 256
MOE_SUB = 128
MOE_BLOCK_SUBS = (9, 8, 1)
MOE_CHUNK_SUBS = 9
MOE_TF = 256
MOE_TN = 512
PROJ_SUB = 256
MERGE_SUB = 256
POST_TM = 512
POST_SUB = 256

def _cparams(sem, vmem_mib):
    return pltpu.CompilerParams(dimension_semantics=sem, vmem_limit_bytes=vmem_mib * MIB)


def _softplus(z):
    return jnp.maximum(z, 0.0) + jnp.log1p(jnp.exp(-jnp.abs(z)))


def _softplus_fast(z):
    return jnp.maximum(z, 0.0) + jnp.log(1.0 + jnp.exp(-jnp.abs(z)))


def _sigmoid(z):
    return 1.0 / (1.0 + jnp.exp(-z))


def _split_bf16(x):
    hi = x.astype(BF16)
    lo = (x - hi.astype(F32)).astype(BF16)
    return hi, lo


def _dot(a, b):
    return jnp.dot(a, b, preferred_element_type=F32)


def _dot_nt(a, b):
    return lax.dot_general(a, b, (((1,), (1,)), ((), ())), preferred_element_type=F32)


def _rmsnorm_kernel(x_ref, g_ref, o_ref):
    x = x_ref[...]
    ms = jnp.mean(x * x, axis=-1, keepdims=True)
    o_ref[...] = (x * lax.rsqrt(ms + EPS) * g_ref[...]).astype(o_ref.dtype)


def _rmsnorm(x, g, tm):
    r, d = x.shape
    assert r % tm == 0
    return pl.pallas_call(
        _rmsnorm_kernel,
        out_shape=jax.ShapeDtypeStruct((r, d), BF16),
        grid=(r // tm,),
        in_specs=[pl.BlockSpec((tm, d), lambda i: (i, 0)), pl.BlockSpec((1, d), lambda i: (0, 0))],
        out_specs=pl.BlockSpec((tm, d), lambda i: (i, 0)),
        compiler_params=_cparams(("parallel",), 32),
        name="rmsnorm",
    )(x, g.reshape(1, d))


def _cast_kernel(x_ref, o_ref, *, gap_lo, gap_hi):
    j = pl.program_id(0)
    is_gap = (j >= gap_lo) & (j < gap_hi)

    @pl.when(is_gap)
    def _():
        o_ref[...] = jnp.zeros_like(o_ref)

    @pl.when(jnp.logical_not(is_gap))
    def _():
        o_ref[...] = x_ref[...].astype(o_ref.dtype)


def _cast_bf16(w, gap_at=0, gap=0):
    r, c = w.shape
    tc = PROJ_TN
    assert c % tc == 0 and gap_at % tc == 0 and gap % tc == 0
    g0, g1 = gap_at // tc, (gap_at + gap) // tc

    def src_tile(j):
        return jnp.where(j < g1, jnp.minimum(j, max(g0 - 1, 0)), j - (g1 - g0))

    return pl.pallas_call(
        functools.partial(_cast_kernel, gap_lo=g0, gap_hi=g1),
        out_shape=jax.ShapeDtypeStruct((r, c + gap), BF16),
        grid=((c + gap) // tc,),
        in_specs=[pl.BlockSpec((r, tc), lambda j: (0, src_tile(j)))],
        out_specs=pl.BlockSpec((r, tc), lambda j: (0, j)),
        compiler_params=_cparams(("parallel",), 32),
        name="cast_bf16",
    )(w)


def _proj_kernel(xn_ref, w_ref, gain_ref, *o_refs, patterns, heads_out):
    j = pl.program_id(1)
    tm = xn_ref.shape[0]
    nh = w_ref.shape[1] // HEAD_DIM
    w = w_ref[...].astype(BF16)
    sub = min(tm, PROJ_SUB) if (heads_out or any(any(p) for p in patterns)) else tm
    ys = [(r0, _dot(xn_ref[r0:r0 + sub, :], w)) for r0 in range(0, tm, sub)]

    def store(r0, c, val):
        for o_ref in o_refs:
            if heads_out:
                o_ref[c, r0:r0 + sub, :] = val.astype(o_ref.dtype)
            else:
                o_ref[r0:r0 + sub, c * HEAD_DIM:(c + 1) * HEAD_DIM] = val.astype(o_ref.dtype)

    def emit(pattern):
        gain = gain_ref[...] if any(pattern) else None
        for r0, y in ys:
            for c in range(nh):
                yc = y[:, c * HEAD_DIM:(c + 1) * HEAD_DIM]
                if pattern[c]:
                    ms = jnp.mean(yc * yc, axis=-1, keepdims=True)
                    yc = yc * lax.rsqrt(ms + EPS) * gain[:, c * HEAD_DIM:(c + 1) * HEAD_DIM]
                store(r0, c, yc)

    distinct = sorted(set(patterns))
    if len(distinct) == 1:
        emit(distinct[0])
    else:
        for pattern in distinct:
            tiles = [jj for jj, p in enumerate(patterns) if p == pattern]
            pred = functools.reduce(jnp.logical_or, [j == jj for jj in tiles])
            pl.when(pred)(functools.partial(emit, pattern))


def _proj(xn, w, gain, *, col0=0, ncols, norm_cols=(0, 0), tm, tn, heads_out=None, bf16_copy=False):
    r, d = xn.shape
    assert r % tm == 0 and col0 % tn == 0 and ncols % tn == 0 and tn % HEAD_DIM == 0
    assert norm_cols[0] % HEAD_DIM == 0 and norm_cols[1] % HEAD_DIM == 0
    col_tile0 = col0 // tn
    patterns = tuple(tuple(norm_cols[0] <= jj * tn + c * HEAD_DIM < norm_cols[1] for c in range(tn // HEAD_DIM))
                     for jj in range(ncols // tn))
    if gain is None:
        gain = jnp.ones((ncols,), F32)
    if heads_out is None:
        out_shape = jax.ShapeDtypeStruct((r, ncols), F32)
        out_spec = pl.BlockSpec((tm, tn), lambda i, j: (i, j))
    else:
        bsz, t = heads_out
        assert bsz * t == r and t % tm == 0
        nt = t // tm
        hpt = tn // HEAD_DIM
        out_shape = jax.ShapeDtypeStruct((bsz, ncols // HEAD_DIM, t, HEAD_DIM), F32)
        out_spec = pl.BlockSpec((None, hpt, tm, HEAD_DIM), lambda i, j: (i // nt, j, i % nt, 0))
    kern = functools.partial(_proj_kernel, patterns=patterns, heads_out=heads_out is not None)
    if bf16_copy:
        out_shape = (out_shape, jax.ShapeDtypeStruct(out_shape.shape, BF16))
        out_spec = (out_spec, out_spec)
    return pl.pallas_call(
        kern,
        out_shape=out_shape,
        grid=(r // tm, ncols // tn),
        in_specs=[
            pl.BlockSpec((tm, d), lambda i, j: (i, 0)),
            pl.BlockSpec((d, tn), lambda i, j: (0, j + col_tile0)),
            pl.BlockSpec((1, tn), lambda i, j: (0, j)),
        ],
        out_specs=out_spec,
        compiler_params=_cparams(("parallel", "parallel"), 40),
        name="proj",
    )(xn, w, gain.reshape(1, ncols))


def _sb_prompt_kernel(bias_ref, q_ref, kb_ref, vb_ref, o_ref):
    hg = pl.program_id(1)
    qi = pl.program_id(2)
    blk = q_ref.shape[0]
    nh = kb_ref.shape[0]

    qs = [q_ref[:, j * HEAD_DIM:(j + 1) * HEAD_DIM] for j in range(nh)]
    biases = [bias_ref[hg * nh + j] for j in range(nh)]
    row = lax.broadcasted_iota(jnp.int32, (blk, blk), 0)
    col = lax.broadcasted_iota(jnp.int32, (blk, blk), 1)
    tri = jnp.where(row >= col, 1.0, 0.0).astype(BF16)
    causal = col < row

    def block(kb, state, masked):
        start = pl.multiple_of(kb * blk, blk)
        heads = range(nh)
        zs = [_dot_nt(qs[j], kb_ref[j, pl.ds(start, blk), :]) * QK_SCALE + biases[j] for j in heads]
        drops = [_softplus_fast(z) for z in zs]
        if masked:
            drops = [jnp.where(causal, dr, 0.0) for dr in drops]
        splits = [_split_bf16(dr) for dr in drops]
        suffixes = [_dot(hi, tri) + _dot(lo, tri) for hi, lo in splits]
        ws = [jnp.exp(zs[j] - suffixes[j] - state[j][0]) for j in heads]
        if masked:
            ws = [jnp.where(causal, w, 0.0) for w in ws]
        pvs = [_dot(ws[j].astype(BF16), vb_ref[j, pl.ds(start, blk), :]) for j in heads]
        return tuple((state[j][0] + suffixes[j][:, 0:1], state[j][1] + pvs[j]) for j in heads)

    zero = (jnp.zeros((blk, 1), F32), jnp.zeros((blk, HEAD_DIM), F32))
    state = block(qi, (zero,) * nh, True)
    state = lax.fori_loop(0, qi, lambda it, st: block(qi - 1 - it, st, False), state)
    for j in range(nh):
        o_ref[:, j * HEAD_DIM:(j + 1) * HEAD_DIM] = state[j][1].astype(o_ref.dtype)


def _sb_prompt(q, k, v, sb_bias):
    bsz, heads, t, _ = k.shape
    blk = SB_BLK
    nh = SB_HEADS_PER_STEP
    assert t % blk == 0 and heads % nh == 0
    assert q.dtype == BF16 and k.dtype == BF16 and v.dtype == BF16
    nq = t // blk
    kv_spec = pl.BlockSpec((None, nh, t, HEAD_DIM), lambda b, h, i: (b, h, 0, 0))
    return pl.pallas_call(
        _sb_prompt_kernel,
        out_shape=jax.ShapeDtypeStruct((bsz * t, SB_W), BF16),
        grid=(bsz, heads // nh, nq),
        in_specs=[
            pl.BlockSpec(memory_space=pltpu.SMEM),
            pl.BlockSpec((blk, nh * HEAD_DIM), lambda b, h, i: (b * nq + i, h)),
            kv_spec, kv_spec,
        ],
        out_specs=pl.BlockSpec((blk, nh * HEAD_DIM), lambda b, h, i: (b * nq + i, h)),
        compiler_params=_cparams(("parallel", "parallel", "parallel"), 40),
        name="sb_prompt",
    )(sb_bias, q, k, v)


def _sb_sample_kernel(pt_ref, bias_ref, q_ref, *refs, n_pages):
    del pt_ref
    k_refs = refs[:n_pages]
    v_refs = refs[n_pages:2 * n_pages]
    o_ref = refs[2 * n_pages]
    heads, page = k_refs[0].shape[:2]
    rep = 8
    n = n_pages * rep

    row = lax.broadcasted_iota(jnp.int32, (page, page), 0)
    col = lax.broadcasted_iota(jnp.int32, (page, page), 1)
    tri = jnp.where(row >= col, 1.0, 0.0).astype(BF16)
    rn = lax.broadcasted_iota(jnp.int32, (n, n), 0)
    cn = lax.broadcasted_iota(jnp.int32, (n, n), 1)
    later = jnp.where(((rn % rep) == (cn % rep)) & (cn // rep > rn // rep), 1.0, 0.0).astype(BF16)
    q = q_ref[0]
    zs = []
    for h in range(heads):
        q8 = jnp.broadcast_to(q[h:h + 1, :], (rep, HEAD_DIM)).astype(BF16)
        z = jnp.concatenate([_dot_nt(q8, k_refs[p][h].astype(BF16)) for p in range(n_pages)], axis=0)
        zs.append(z * QK_SCALE + bias_ref[h])
    z = jnp.concatenate(zs, axis=0)
    hi, lo = _split_bf16(-_softplus_fast(z))
    suffix = _dot(hi, tri) + _dot(lo, tri)
    shi, slo = _split_bf16(suffix)
    carry = jnp.concatenate(
        [(_dot(later, shi[h * n:(h + 1) * n]) + _dot(later, slo[h * n:(h + 1) * n]))[:, 0:1] for h in range(heads)],
        axis=0)
    w = jnp.exp(z + suffix + carry).astype(BF16)
    outs = []
    for h in range(heads):
        acc = jnp.zeros((rep, HEAD_DIM), F32)
        for p in range(n_pages):
            r0 = h * n + p * rep
            acc = acc + _dot(w[r0:r0 + rep, :], v_refs[p][h].astype(BF16))
        outs.append(acc[0:1, :])
    o_ref[0] = jnp.concatenate(outs, axis=0).astype(o_ref.dtype)


def _sb_sample(q, sb_bias, cache_k, cache_v, page_table):
    bs, n_pages = page_table.shape
    heads, page = cache_k.shape[1:3]
    assert heads == SB_HEADS and cache_k.shape[3] == HEAD_DIM

    def page_spec(p):
        return pl.BlockSpec((None, heads, page, HEAD_DIM), lambda b, pt: (pt[b, p], 0, 0, 0))

    grid_spec = pltpu.PrefetchScalarGridSpec(
        num_scalar_prefetch=1,
        grid=(bs,),
        in_specs=[pl.BlockSpec(memory_space=pltpu.SMEM),
                  pl.BlockSpec((1, heads, HEAD_DIM), lambda b, pt: (b, 0, 0))]
                 + [page_spec(p) for p in range(n_pages)] * 2,
        out_specs=pl.BlockSpec((1, heads, HEAD_DIM), lambda b, pt: (b, 0, 0)),
    )
    out = pl.pallas_call(
        functools.partial(_sb_sample_kernel, n_pages=n_pages),
        out_shape=jax.ShapeDtypeStruct((bs, heads, HEAD_DIM), BF16),
        grid_spec=grid_spec,
        compiler_params=_cparams(("arbitrary",), 40),
        name="sb_sample",
    )(page_table, sb_bias, q.reshape(bs, heads, HEAD_DIM), *([cache_k] * n_pages), *([cache_v] * n_pages))
    return out.reshape(bs, SB_W)


def _lru_gates(xc, wa_ref, ba, wx_ref, bx, nsp_lambda):
    xb = xc.astype(BF16)
    r_parts, i_parts = [], []
    for h in range(LRU_HEADS):
        xh = xb[:, h * HEAD_DIM:(h + 1) * HEAD_DIM]
        r_parts.append(_dot(xh, wa_ref[h].astype(BF16)))
        i_parts.append(_dot(xh, wx_ref[h].astype(BF16)))
    r = _sigmoid(jnp.concatenate(r_parts, axis=1) + ba)
    i = _sigmoid(jnp.concatenate(i_parts, axis=1) + bx)
    log_a = -LRU_C * r * nsp_lambda
    a = jnp.exp(log_a)
    u = jnp.sqrt(1.0 - jnp.exp(2.0 * log_a)) * (i * xc)
    return a, u


def _gelu_tanh(x):
    return 0.5 * x * (1.0 + jnp.tanh(0.7978845608028654 * (x + 0.044715 * (x * x * x))))


def _lru_prompt_kernel(x_ref, gate_ref, prev_ref, h0_ref, cw_ref, cb_ref, wa_ref, ba_ref, wx_ref, bx_ref,
                       lam_ref, o_ref, hlast_ref, xp_ref, h_ref):
    ti = pl.program_id(1)
    tt = x_ref.shape[0]
    pad = 8

    @pl.when(ti == 0)
    def _():
        xp_ref[pad - (CONV_W - 1):pad, :] = prev_ref[0]
        h_ref[...] = h0_ref[0]

    x = x_ref[...]
    xp_ref[pad:pad + tt, :] = x
    cw = cw_ref[...]
    xc = cb_ref[...] + cw[CONV_W - 1:CONV_W, :] * x
    for j in range(CONV_W - 1):
        xc = xc + cw[j:j + 1, :] * xp_ref[pad - (CONV_W - 1) + j:pad - (CONV_W - 1) + j + tt, :]
    xp_ref[pad - (CONV_W - 1):pad, :] = x[tt - (CONV_W - 1):tt, :]

    nsp = _softplus(-lam_ref[...])
    a, u = _lru_gates(xc, wa_ref, ba_ref[...], wx_ref, bx_ref[...], nsp)
    rows = lax.broadcasted_iota(jnp.int32, (tt, LRU_W), 0)
    b = jnp.where(rows == 0, u + a * h_ref[...], u)
    s = 1
    while s < tt:
        keep = rows >= s
        b = jnp.where(keep, a * pltpu.roll(b, s, 0) + b, b)
        if 2 * s < tt:
            a = jnp.where(keep, a * pltpu.roll(a, s, 0), a)
        s *= 2
    h_ref[...] = b[tt - 1:tt, :]
    o_ref[...] = (b * _gelu_tanh(gate_ref[...])).astype(o_ref.dtype)

    @pl.when(ti == pl.num_programs(1) - 1)
    def _():
        hlast_ref[0] = b[tt - 1:tt, :]


def _lru_prompt(proj, conv_prev, h0, conv_w, conv_b, lru_wa, lru_ba, lru_wx, lru_bx, lru_lambda, bsz, t):
    tt = LRU_TT
    assert t % tt == 0
    nt = t // tt
    xl, gl = 0, 1
    full = lambda shape: pl.BlockSpec(shape, lambda b, i: (0,) * len(shape))
    o_lru, h_last = pl.pallas_call(
        _lru_prompt_kernel,
        out_shape=(jax.ShapeDtypeStruct((bsz * t, LRU_W), BF16),
                   jax.ShapeDtypeStruct((bsz, 1, LRU_W), F32)),
        grid=(bsz, nt),
        in_specs=[
            pl.BlockSpec((tt, LRU_W), lambda b, i: (b * nt + i, xl)),
            pl.BlockSpec((tt, LRU_W), lambda b, i: (b * nt + i, gl)),
            pl.BlockSpec((1, CONV_W - 1, LRU_W), lambda b, i: (b, 0, 0)),
            pl.BlockSpec((1, 1, LRU_W), lambda b, i: (b, 0, 0)),
            full((CONV_W, LRU_W)), full((1, LRU_W)),
            full((LRU_HEADS, HEAD_DIM, HEAD_DIM)), full((1, LRU_W)),
            full((LRU_HEADS, HEAD_DIM, HEAD_DIM)), full((1, LRU_W)),
            full((1, LRU_W)),
        ],
        out_specs=(pl.BlockSpec((tt, LRU_W), lambda b, i: (b * nt + i, 0)),
                   pl.BlockSpec((1, 1, LRU_W), lambda b, i: (b, 0, 0))),
        scratch_shapes=[pltpu.VMEM((8 + tt, LRU_W), F32), pltpu.VMEM((1, LRU_W), F32)],
        compiler_params=_cparams(("parallel", "arbitrary"), 32),
        name="lru_prompt",
    )(proj, proj, conv_prev, h0.reshape(bsz, 1, LRU_W), conv_w, conv_b.reshape(1, LRU_W),
      lru_wa, lru_ba.reshape(1, LRU_W), lru_wx, lru_bx.reshape(1, LRU_W), lru_lambda.reshape(1, LRU_W))
    return o_lru, h_last.reshape(bsz, LRU_W)


def _lru_step_kernel(x_ref, gate_ref, p0_ref, p1_ref, p2_ref, h0_ref, cw_ref, cb_ref, wa_ref, ba_ref, wx_ref,
                     bx_ref, lam_ref, o_ref, h_ref):
    cw = cw_ref[...]
    xc = (cb_ref[...] + cw[0:1, :] * p0_ref[...] + cw[1:2, :] * p1_ref[...] + cw[2:3, :] * p2_ref[...]
          + cw[3:4, :] * x_ref[...])
    nsp = _softplus(-lam_ref[...])
    a, u = _lru_gates(xc, wa_ref, ba_ref[...], wx_ref, bx_ref[...], nsp)
    h = u + a * h0_ref[...]
    h_ref[...] = h
    o_ref[...] = (h * _gelu_tanh(gate_ref[...])).astype(o_ref.dtype)


def _lru_step(x_lru, gate_lru, state_conv, h0, conv_w, conv_b, lru_wa, lru_ba, lru_wx, lru_bx, lru_lambda):
    bs = x_lru.shape[0]
    assert CONV_W == 4
    prevs = [state_conv[:, j, :] for j in range(CONV_W - 1)]
    o_lru, h_new = pl.pallas_call(
        _lru_step_kernel,
        out_shape=(jax.ShapeDtypeStruct((bs, LRU_W), BF16), jax.ShapeDtypeStruct((bs, LRU_W), F32)),
        name="lru_step",
    )(x_lru, gate_lru, *prevs, h0, conv_w, conv_b.reshape(1, LRU_W), lru_wa, lru_ba.reshape(1, LRU_W),
      lru_wx, lru_bx.reshape(1, LRU_W), lru_lambda.reshape(1, LRU_W))
    return o_lru, h_new


def _mem_prompt_kernel(q_ref, k_ref, v_ref, o_ref):
    s = _dot_nt(q_ref[...].astype(BF16), k_ref[...].astype(BF16)) * QK_SCALE
    m = jnp.max(s, axis=-1, keepdims=True)
    e = jnp.exp(s - m)
    p = e / jnp.sum(e, axis=-1, keepdims=True)
    o_ref[...] = _dot(p.astype(BF16), v_ref[...].astype(BF16)).astype(o_ref.dtype)


def _mem_prompt(proj, qcol0, mem_k, mem_v, t):
    bsz, _, n_mem, _ = mem_k.shape
    tq = min(t, 512)
    assert t % tq == 0 and qcol0 % HEAD_DIM == 0
    nt = t // tq
    qc = qcol0 // HEAD_DIM
    kv_spec = pl.BlockSpec((None, None, n_mem, HEAD_DIM), lambda b, h, i: (b, h, 0, 0))
    return pl.pallas_call(
        _mem_prompt_kernel,
        out_shape=jax.ShapeDtypeStruct((bsz * t, MEM_W), BF16),
        grid=(bsz, MEM_HEADS, nt),
        in_specs=[pl.BlockSpec((tq, HEAD_DIM), lambda b, h, i: (b * nt + i, qc + h)), kv_spec, kv_spec],
        out_specs=pl.BlockSpec((tq, HEAD_DIM), lambda b, h, i: (b * nt + i, h)),
        compiler_params=_cparams(("parallel", "parallel", "parallel"), 32),
        name="mem_prompt",
    )(proj, mem_k, mem_v)


def _mem_sample_kernel(q_ref, k_ref, v_ref, o_ref):
    group = q_ref.shape[0]
    for g in range(group):
        q = q_ref[g]
        s = jnp.sum(k_ref[g] * q[None], axis=-1, keepdims=True) * QK_SCALE
        m = jnp.max(s, axis=0, keepdims=True)
        e = jnp.exp(s - m)
        p = e / jnp.sum(e, axis=0, keepdims=True)
        o_ref[g] = jnp.sum(p * v_ref[g], axis=0).astype(o_ref.dtype)


def _mem_sample(q, cache_k, cache_v):
    bs, n_mem, heads, _ = cache_k.shape
    assert heads == MEM_HEADS and cache_k.shape[3] == HEAD_DIM
    group = 8
    assert bs % group == 0
    kv_spec = pl.BlockSpec((group, n_mem, heads, HEAD_DIM), lambda i: (i, 0, 0, 0))
    out = pl.pallas_call(
        _mem_sample_kernel,
        out_shape=jax.ShapeDtypeStruct((bs, heads, HEAD_DIM), BF16),
        grid=(bs // group,),
        in_specs=[pl.BlockSpec((group, heads, HEAD_DIM), lambda i: (i, 0, 0)), kv_spec, kv_spec],
        out_specs=pl.BlockSpec((group, heads, HEAD_DIM), lambda i: (i, 0, 0)),
        compiler_params=_cparams(("parallel",), 32),
        name="mem_sample",
    )(q.reshape(bs, heads, HEAD_DIM), cache_k, cache_v)
    return out.reshape(bs, MEM_W)


def _merge_kernel(oa_ref, ol_ref, om_ref, pa_ref, pl_ref, pm_ref, ga_ref, gl_ref, gm_ref, o_ref):
    tm = o_ref.shape[0]
    sub = min(tm, MERGE_SUB)
    pa = pa_ref[...].astype(BF16)
    pl_ = pl_ref[...].astype(BF16)
    pm = pm_ref[...].astype(BF16)
    for r0 in range(0, tm, sub):
        rows = slice(r0, r0 + sub)
        ya = _dot(oa_ref[rows, :], pa)
        yl = _dot(ol_ref[rows, :], pl_)
        ym = _dot(om_ref[rows, :], pm)
        merged = (_sigmoid(ga_ref[rows, :]) * ya + _sigmoid(gl_ref[rows, :]) * yl
                  + _sigmoid(gm_ref[rows, :]) * ym)
        o_ref[rows, :] = merged.astype(o_ref.dtype)


def _merge(o_sb, o_lru, o_mem, p_attn, p_lru, p_mem, gates, tm):
    r = o_sb.shape[0]
    d = p_attn.shape[1]
    tn = 512
    assert r % tm == 0 and d % tn == 0
    nj = d // tn
    return pl.pallas_call(
        _merge_kernel,
        out_shape=jax.ShapeDtypeStruct((r, d), BF16),
        grid=(r // tm, nj),
        in_specs=[
            pl.BlockSpec((tm, SB_W), lambda i, j: (i, 0)),
            pl.BlockSpec((tm, LRU_W), lambda i, j: (i, 0)),
            pl.BlockSpec((tm, MEM_W), lambda i, j: (i, 0)),
            pl.BlockSpec((SB_W, tn), lambda i, j: (0, j)),
            pl.BlockSpec((LRU_W, tn), lambda i, j: (0, j)),
            pl.BlockSpec((MEM_W, tn), lambda i, j: (0, j)),
            pl.BlockSpec((tm, tn), lambda i, j: (i, j)),
            pl.BlockSpec((tm, tn), lambda i, j: (i, nj + j)),
            pl.BlockSpec((tm, tn), lambda i, j: (i, 2 * nj + j)),
        ],
        out_specs=pl.BlockSpec((tm, tn), lambda i, j: (i, j)),
        compiler_params=_cparams(("parallel", "parallel"), 44),
        name="merge",
    )(o_sb, o_lru, o_mem, p_attn, p_lru, p_mem, gates, gates, gates)


def _post_kernel(x_ref, m_ref, wo_ref, g_ref, wr_ref, br_ref, cnt0_ref,
                 hres_ref, hnp_ref, idx_ref, gate_ref, rank_ref, cnt_ref):
    i = pl.program_id(0)
    tm, d = x_ref.shape
    sub = min(tm, POST_SUB)
    starts = range(0, tm, sub)

    @pl.when(i == 0)
    def _():
        cnt_ref[...] = cnt0_ref[...]

    hres_all = [x_ref[r0:r0 + sub, :] + _dot(m_ref[r0:r0 + sub, :], wo_ref[...]) for r0 in starts]
    eid = lax.broadcasted_iota(jnp.int32, (wr_ref.shape[0], sub), 0)
    ne = eid.shape[0]
    r_ = lax.broadcasted_iota(jnp.int32, (sub, sub), 0)
    c_ = lax.broadcasted_iota(jnp.int32, (sub, sub), 1)
    before = jnp.where(r_ < c_, 1.0, 0.0).astype(BF16)
    cnt = cnt_ref[...]
    for r0, hres in zip(starts, hres_all):
        hres_ref[r0:r0 + sub, :] = hres
        ms = jnp.mean(hres * hres, axis=-1, keepdims=True)
        hn = hres * lax.rsqrt(ms + EPS) * g_ref[...]
        hb = hn.astype(BF16)
        wa = lax.bitcast_convert_type(hb[:, :d // 2].astype(F32), jnp.uint32)
        wb = lax.bitcast_convert_type(hb[:, d // 2:].astype(F32), jnp.uint32)
        hnp_ref[r0:r0 + sub, :] = wa | (wb >> 16)

        work = _dot_nt(wr_ref[...], hb) + br_ref[...]
        vals, idxs, onehots = [], [], []
        for _ in range(TOP_K):
            mx = jnp.max(work, axis=0, keepdims=True)
            sel = jnp.min(jnp.where(work == mx, eid, ne), axis=0, keepdims=True)
            oh = eid == sel
            vals.append(mx)
            idxs.append(sel)
            onehots.append(oh)
            work = jnp.where(oh, -jnp.inf, work)
        es = [jnp.exp(v - vals[0]) for v in vals]
        den = functools.reduce(jnp.add, es)
        gate_ref[:, r0:r0 + sub] = jnp.concatenate([e / den for e in es], axis=0)
        idx_ref[:, r0:r0 + sub] = jnp.concatenate(idxs, axis=0)

        chosen_f = jnp.where(functools.reduce(jnp.logical_or, onehots), 1.0, 0.0)
        prior = _dot(chosen_f.astype(BF16), before) + cnt
        rank_ref[:, r0:r0 + sub] = jnp.concatenate(
            [jnp.sum(jnp.where(oh, prior, 0.0), axis=0, keepdims=True) for oh in onehots], axis=0).astype(jnp.int32)
        cnt = cnt + jnp.sum(chosen_f, axis=1, keepdims=True)
    cnt_ref[...] = cnt


def _post(x, merged, wo_b, g, wr_t, b_router, cnt0, tm):
    r, d = x.shape
    ne = wr_t.shape[0]
    assert r % tm == 0
    full = lambda shape: pl.BlockSpec(shape, lambda i: (0,) * len(shape))
    return pl.pallas_call(
        _post_kernel,
        out_shape=(jax.ShapeDtypeStruct((r, d), F32),
                   jax.ShapeDtypeStruct((r, d // 2), jnp.uint32),
                   jax.ShapeDtypeStruct((TOP_K, r), jnp.int32),
                   jax.ShapeDtypeStruct((TOP_K, r), F32),
                   jax.ShapeDtypeStruct((TOP_K, r), jnp.int32),
                   jax.ShapeDtypeStruct((ne, 1), F32)),
        grid=(r // tm,),
        in_specs=[pl.BlockSpec((tm, d), lambda i: (i, 0)),
                  pl.BlockSpec((tm, d), lambda i: (i, 0)),
                  full((d, d)), full((1, d)), full((ne, d)), full((ne, 1)), full((ne, 1))],
        out_specs=(pl.BlockSpec((tm, d), lambda i: (i, 0)),
                   pl.BlockSpec((tm, d // 2), lambda i: (i, 0)),
                   pl.BlockSpec((TOP_K, tm), lambda i: (0, i)),
                   pl.BlockSpec((TOP_K, tm), lambda i: (0, i)),
                   pl.BlockSpec((TOP_K, tm), lambda i: (0, i)),
                   full((ne, 1))),
        compiler_params=_cparams(("arbitrary",), 56),
        name="post",
    )(x, merged, wo_b, g.reshape(1, d), wr_t, b_router.reshape(ne, 1), cnt0)


def _dispatch_kernel(tail_ref, dest_ref, hn_ref, xs_ref, zero_ref, sem, zsem):
    i = pl.program_id(0)
    tm = dest_ref.shape[2] // TOP_K

    @pl.when(i == 0)
    def _():
        zero_ref[...] = jnp.zeros_like(zero_ref)

        def tail_copy(e):
            start = pl.multiple_of(tail_ref[e], MOE_SUB)
            return pltpu.make_async_copy(zero_ref, xs_ref.at[pl.ds(start, MOE_SUB)], zsem)

        def zstart(e, c):
            pl.when(tail_ref[e] >= 0)(lambda: tail_copy(e).start())
            return c

        def zwait(e, c):
            pl.when(tail_ref[e] >= 0)(lambda: tail_copy(e).wait())
            return c

        lax.fori_loop(0, tail_ref.shape[0], zstart, 0)
        lax.fori_loop(0, tail_ref.shape[0], zwait, 0)

    def row_copy(t, k):
        return pltpu.make_async_copy(hn_ref.at[pl.ds(i * tm + t, 1)],
                                     xs_ref.at[pl.ds(dest_ref[0, 0, k * tm + t], 1)], sem)

    def start(t, c):
        for k in range(TOP_K):
            row_copy(t, k).start()
        return c

    def wait(t, c):
        for k in range(TOP_K):
            row_copy(t, k).wait()
        return c

    def wait_tile():
        lax.fori_loop(0, tm, wait, 0)

    lax.fori_loop(0, tm, start, 0)
    pl.when(i > 0)(wait_tile)
    pl.when(i == pl.num_programs(0) - 1)(wait_tile)


def _dispatch(hnp, dest, tail, n_rows, tm):
    r, w = hnp.shape
    assert r % tm == 0
    nt = r // tm
    dest_t = dest.reshape(TOP_K, nt, tm).transpose(1, 0, 2).reshape(nt, 1, TOP_K * tm)
    grid_spec = pltpu.PrefetchScalarGridSpec(
        num_scalar_prefetch=1,
        grid=(nt,),
        in_specs=[pl.BlockSpec((1, 1, TOP_K * tm), lambda i, tl: (i, 0, 0), memory_space=pltpu.SMEM),
                  pl.BlockSpec(memory_space=pl.ANY)],
        out_specs=pl.BlockSpec(memory_space=pl.ANY),
        scratch_shapes=[pltpu.VMEM((MOE_SUB, w), hnp.dtype), pltpu.SemaphoreType.DMA, pltpu.SemaphoreType.DMA],
    )
    return pl.pallas_call(
        _dispatch_kernel,
        out_shape=jax.ShapeDtypeStruct((n_rows, w), hnp.dtype),
        grid_spec=grid_spec,
        compiler_params=_cparams(("arbitrary",), 16),
        name="dispatch",
    )(tail, dest_t, hnp)


def _moe_kernel(ce_ref, cb_ref, ns_ref, xs_ref, wg_ref, wu_ref, bg_ref, bu_ref, wd_ref, bd_ref, o_ref,
                xa_ref, xb_ref):
    del ce_ref, cb_ref
    c = pl.program_id(0)
    s = pl.program_id(1)
    nsub = ns_ref[c]
    half = xa_ref.shape[1]
    d = o_ref.shape[1]

    def for_row_blocks(fn):
        done = 0
        for size in MOE_BLOCK_SUBS:
            count = (nsub - done) // size
            rows = size * MOE_SUB

            def body(i, carry, done=done, rows=rows):
                fn(pl.ds(pl.multiple_of(done * MOE_SUB + i * rows, MOE_SUB), rows))
                return carry

            lax.fori_loop(0, count, body, 0)
            done = done + count * size

    @pl.when(s == 0)
    def _():
        def unpack(rows):
            word = xs_ref[rows, :]
            xa_ref[rows, :] = lax.bitcast_convert_type(word & jnp.uint32(0xFFFF0000), F32).astype(BF16)
            xb_ref[rows, :] = lax.bitcast_convert_type(word << 16, F32).astype(BF16)
            o_ref[rows, :] = jnp.broadcast_to(bd_ref[...], (rows.size, d))
        for_row_blocks(unpack)

    bg = bg_ref[...]
    bu = bu_ref[...]

    def step(rows):
        xa = xa_ref[rows, :]
        xb = xb_ref[rows, :]
        gate = (_dot(xa, wg_ref[:half, :].astype(BF16)) + _dot(xb, wg_ref[half:, :].astype(BF16)) + bg)
        upv = (_dot(xa, wu_ref[:half, :].astype(BF16)) + _dot(xb, wu_ref[half:, :].astype(BF16)) + bu)
        gate = jnp.minimum(gate, SWIGLU_LIMIT)
        upv = jnp.clip(upv, -SWIGLU_LIMIT, SWIGLU_LIMIT)
        act = (gate * _sigmoid(SWIGLU_ALPHA * gate) * (upv + 1.0)).astype(BF16)
        for n0 in range(0, d, MOE_TN):
            o_ref[rows, n0:n0 + MOE_TN] += _dot(act, wd_ref[:, n0:n0 + MOE_TN].astype(BF16))
    for_row_blocks(step)


def _moe(xs, n_used, chunk_expert, chunk_block, chunk_nsub, w_up, b_up, w_down, b_down):
    ne, d, dff2 = w_up.shape
    dff = dff2 // 2
    rb = MOE_SUB * MOE_CHUNK_SUBS
    n_chunks = xs.shape[0] // rb
    tf = MOE_TF
    nf = dff // tf
    assert dff % tf == 0 and d % MOE_TN == 0

    grid_spec = pltpu.PrefetchScalarGridSpec(
        num_scalar_prefetch=3,
        grid=(n_used, nf),
        in_specs=[
            pl.BlockSpec((rb, d // 2), lambda c, s, ce, cb, ns: (cb[c], 0)),
            pl.BlockSpec((None, d, tf), lambda c, s, ce, cb, ns: (ce[c], 0, s)),
            pl.BlockSpec((None, d, tf), lambda c, s, ce, cb, ns: (ce[c], 0, nf + s)),
            pl.BlockSpec((None, 1, tf), lambda c, s, ce, cb, ns: (ce[c], 0, s)),
            pl.BlockSpec((None, 1, tf), lambda c, s, ce, cb, ns: (ce[c], 0, nf + s)),
            pl.BlockSpec((None, tf, d), lambda c, s, ce, cb, ns: (ce[c], s, 0)),
            pl.BlockSpec((None, 1, d), lambda c, s, ce, cb, ns: (ce[c], 0, 0)),
        ],
        out_specs=pl.BlockSpec((rb, d), lambda c, s, ce, cb, ns: (cb[c], 0)),
        scratch_shapes=[pltpu.VMEM((rb, d // 2), BF16), pltpu.VMEM((rb, d // 2), BF16)],
    )
    return pl.pallas_call(
        _moe_kernel,
        out_shape=jax.ShapeDtypeStruct((n_chunks * rb, d), F32),
        grid_spec=grid_spec,
        compiler_params=_cparams(("arbitrary", "arbitrary"), 56),
        name="moe",
    )(chunk_expert, chunk_block, chunk_nsub, xs, w_up, w_up, b_up.reshape(ne, 1, dff2), b_up.reshape(ne, 1, dff2),
      w_down, b_down.reshape(ne, 1, d))


def _combine_kernel(dest_ref, dest_next_ref, hres_ref, gate_ref, yb_ref, o_ref, buf_ref, sem):
    i = pl.program_id(0)
    tm = hres_ref.shape[0]
    slot = i % 2

    def row_copy(dref, s, t, k):
        return pltpu.make_async_copy(yb_ref.at[pl.ds(dref[0, 0, k * tm + t], 1)],
                                     buf_ref.at[s, k, pl.ds(t, 1)], sem.at[s])

    def issue(dref, s):
        def body(t, c):
            for k in range(TOP_K):
                row_copy(dref, s, t, k).start()
            return c
        lax.fori_loop(0, tm, body, 0)

    def drain(dref, s):
        def body(t, c):
            for k in range(TOP_K):
                row_copy(dref, s, t, k).wait()
            return c
        lax.fori_loop(0, tm, body, 0)

    pl.when(i == 0)(lambda: issue(dest_ref, slot))
    pl.when(i + 1 < pl.num_programs(0))(lambda: issue(dest_next_ref, 1 - slot))
    drain(dest_ref, slot)
    g = gate_ref[...]
    y = hres_ref[...]
    for k in range(TOP_K):
        y = y + g[:, k:k + 1] * buf_ref[slot, k]
    o_ref[...] = y


def _combine(hres, gates, dest, yb, tm):
    r, d = hres.shape
    assert r % tm == 0
    nt = r // tm
    dest_t = dest.reshape(TOP_K, nt, tm).transpose(1, 0, 2).reshape(nt, 1, TOP_K * tm)
    return pl.pallas_call(
        _combine_kernel,
        out_shape=jax.ShapeDtypeStruct((r, d), F32),
        grid=(nt,),
        in_specs=[pl.BlockSpec((1, 1, TOP_K * tm), lambda i: (i, 0, 0), memory_space=pltpu.SMEM),
                  pl.BlockSpec((1, 1, TOP_K * tm), lambda i: (jnp.minimum(i + 1, nt - 1), 0, 0),
                               memory_space=pltpu.SMEM),
                  pl.BlockSpec((tm, d), lambda i: (i, 0)),
                  pl.BlockSpec((tm, TOP_K), lambda i: (i, 0)),
                  pl.BlockSpec(memory_space=pl.ANY)],
        out_specs=pl.BlockSpec((tm, d), lambda i: (i, 0)),
        scratch_shapes=[pltpu.VMEM((2, TOP_K, tm, d), F32), pltpu.SemaphoreType.DMA((2,))],
        compiler_params=_cparams(("arbitrary",), 40),
        name="combine",
    )(dest_t, dest_t, hres, gates.T, yb)


def _row_tile(r, cap):
    tm = min(r, cap)
    assert r % tm == 0
    return tm


def kernel(x_prompt, x_sample, mem_prompt, cache_sb_k, cache_sb_v, page_table, cache_mem_k, cache_mem_v, state_conv, state_lru, norm_mix_g, norm_mem_g, w_in, q_sb_g, k_sb_g, sb_bias, q_mem_g, k_mem_g, w_mem_kv, conv_w, conv_b, lru_wa, lru_ba, lru_wx, lru_bx, lru_lambda, p_attn, p_lru, p_mem, w_o, norm_ffn_g, w_router, b_router, w_up, b_up, w_down, b_down):
    bp, t, d = x_prompt.shape
    bs, ts, _ = x_sample.shape
    assert ts == 1, "sample group decodes one token per sequence"
    n_mem = mem_prompt.shape[1]
    q_gain = jnp.tile(q_sb_g, SB_HEADS)
    k_gain = jnp.tile(k_sb_g, SB_HEADS)
    qm_gain = jnp.tile(q_mem_g, MEM_HEADS)
    heads_major = lambda a: jnp.transpose(a, (0, 2, 1, 3))

    memn = _rmsnorm(mem_prompt.reshape(bp * n_mem, d), norm_mem_g, _row_tile(bp * n_mem, 256))
    mem_tm = _row_tile(n_mem, 256)
    mem_k_hm = _proj(memn, w_mem_kv, jnp.tile(k_mem_g, MEM_HEADS), col0=0, ncols=MEM_W, norm_cols=(0, MEM_W),
                     tm=mem_tm, tn=MEM_W, heads_out=(bp, n_mem))
    mem_v_hm = _proj(memn, w_mem_kv, None, col0=MEM_W, ncols=MEM_W, tm=mem_tm, tn=MEM_W, heads_out=(bp, n_mem))

    rest_w = COL_GATE - COL_XL
    qm_cols = (COL_QM - COL_XL, rest_w)
    gap = -COL_XL % 1024
    w_b = _cast_bf16(w_in, gap_at=COL_XL, gap=gap)
    q_at, k_at, v_at, rest_at, gates_at = COL_Q, COL_K, COL_V, COL_XL + gap, COL_GATE + gap
    rest_gain = jnp.concatenate([jnp.ones((2 * LRU_W,), F32), qm_gain])

    xp2 = x_prompt.reshape(bp * t, d)
    tm_p = _row_tile(t, 1024)
    xn_p = _rmsnorm(xp2, norm_mix_g, _row_tile(bp * t, 512))
    _, q_pb = _proj(xn_p, w_b, q_gain, col0=q_at, ncols=SB_W, norm_cols=(0, SB_W), tm=tm_p, tn=SB_W,
                    bf16_copy=True)
    k_p, k_pb = _proj(xn_p, w_b, k_gain, col0=k_at, ncols=SB_W, norm_cols=(0, SB_W), tm=tm_p, tn=SB_W,
                      heads_out=(bp, t), bf16_copy=True)
    v_p, v_pb = _proj(xn_p, w_b, None, col0=v_at, ncols=SB_W, tm=tm_p, tn=SB_W, heads_out=(bp, t), bf16_copy=True)
    rest_p = _proj(xn_p, w_b, rest_gain, col0=rest_at, ncols=rest_w, norm_cols=qm_cols, tm=tm_p, tn=1024)
    gates_p = _proj(xn_p, w_b, None, col0=gates_at, ncols=N_BRANCH * d, tm=tm_p, tn=1024)
    o_sb_p = _sb_prompt(q_pb, k_pb, v_pb, sb_bias)
    conv0 = jnp.zeros((bp, CONV_W - 1, LRU_W), x_prompt.dtype)
    lru0 = jnp.zeros((bp, LRU_W), state_lru.dtype)
    o_lru_p, lru_prompt = _lru_prompt(rest_p, conv0, lru0, conv_w, conv_b, lru_wa, lru_ba, lru_wx, lru_bx,
                                      lru_lambda, bp, t)
    o_mem_p = _mem_prompt(rest_p, COL_QM - COL_XL, mem_k_hm, mem_v_hm, t)
    xl_tail = rest_p.reshape(bp, t, -1)[:, t - (CONV_W - 1):, :LRU_W]
    conv_prompt = jnp.concatenate([conv0, xl_tail], axis=1)[:, -(CONV_W - 1):]

    xs2 = x_sample.reshape(bs, d)
    xn_s = _rmsnorm(xs2, norm_mix_g, bs)
    qkv_gain = jnp.concatenate([q_gain, k_gain, jnp.ones((SB_W,), F32)])
    qkv_s = _proj(xn_s, w_b, qkv_gain, col0=q_at, ncols=3 * SB_W, norm_cols=(0, 2 * SB_W), tm=bs, tn=SB_W)
    q_s, k_s, v_s = qkv_s[:, :SB_W], qkv_s[:, SB_W:2 * SB_W], qkv_s[:, 2 * SB_W:]
    rest_s = _proj(xn_s, w_b, rest_gain, col0=rest_at, ncols=rest_w, norm_cols=qm_cols, tm=bs, tn=1024)
    gates_s = _proj(xn_s, w_b, None, col0=gates_at, ncols=N_BRANCH * d, tm=bs, tn=1024)
    o_sb_s = _sb_sample(q_s, sb_bias, heads_major(cache_sb_k), heads_major(cache_sb_v), page_table)
    xl_s = rest_s[:, :LRU_W]
    o_lru_s, lru_sample = _lru_step(xl_s, rest_s[:, LRU_W:2 * LRU_W], state_conv, state_lru, conv_w, conv_b,
                                    lru_wa, lru_ba, lru_wx, lru_bx, lru_lambda)
    conv_sample = jnp.concatenate([state_conv, xl_s[:, None, :]], axis=1)[:, -(CONV_W - 1):]
    o_mem_s = _mem_sample(rest_s[:, 2 * LRU_W:], cache_mem_k, cache_mem_v)

    wo_b = _cast_bf16(w_o)
    wr_t = w_router.T.astype(BF16)
    merged_p = _merge(o_sb_p, o_lru_p, o_mem_p, p_attn, p_lru, p_mem, gates_p, _row_tile(bp * t, 1024))
    merged_s = _merge(o_sb_s, o_lru_s, o_mem_s, p_attn, p_lru, p_mem, gates_s, _row_tile(bs, 1024))
    cnt0 = jnp.zeros((N_EXPERTS, 1), F32)
    hres_p, hnp_p, idx_p, gate_p, rank_p, cnt_p = _post(xp2, merged_p, wo_b, norm_ffn_g, wr_t, b_router,
                                                        cnt0, _row_tile(bp * t, POST_TM))
    hres_s, hnp_s, idx_s, gate_s, rank_s, cnt_all = _post(xs2, merged_s, wo_b, norm_ffn_g, wr_t, b_router,
                                                          cnt_p, _row_tile(bs, POST_TM))

    rb = MOE_SUB * MOE_CHUNK_SUBS
    n_assign = (bp * t + bs) * TOP_K
    n_chunks = -(-n_assign // rb) + N_EXPERTS
    counts = cnt_all[:, 0].astype(jnp.int32)
    chunks_e = (counts + rb - 1) // rb
    chunk_end = jnp.cumsum(chunks_e)
    chunk_start = chunk_end - chunks_e
    used = chunk_end[-1]
    cidx = jnp.arange(n_chunks, dtype=jnp.int32)
    last = jnp.minimum(cidx, used - 1)
    chunk_expert = jnp.clip(jnp.searchsorted(chunk_end, last, side='right'), 0, N_EXPERTS - 1).astype(jnp.int32)
    rows_left = counts[chunk_expert] - (last - chunk_start[chunk_expert]) * rb
    chunk_nsub = jnp.where(cidx < used, (jnp.clip(rows_left, 0, rb) + MOE_SUB - 1) // MOE_SUB, 0).astype(jnp.int32)
    chunk_block = last.astype(jnp.int32)
    row_start = chunk_start * rb
    experts = jnp.arange(N_EXPERTS, dtype=jnp.int32)

    def slots(idx, rank):
        return jnp.sum(jnp.where(idx[..., None] == experts, row_start, 0), axis=-1) + rank

    dest_p = slots(idx_p, rank_p)
    dest_s = slots(idx_s, rank_s)
    tail = jnp.where(counts % MOE_SUB != 0, row_start + counts // MOE_SUB * MOE_SUB, -1).astype(jnp.int32)

    hnp = jnp.concatenate([hnp_p, hnp_s], axis=0)
    dest = jnp.concatenate([dest_p, dest_s], axis=1)
    xs = _dispatch(hnp, dest, tail, n_chunks * rb, _row_tile(bp * t + bs, 128))
    yb = _moe(xs, used.astype(jnp.int32), chunk_expert, chunk_block, chunk_nsub, w_up, b_up, w_down, b_down)
    y_prompt = _combine(hres_p, gate_p, dest_p, yb, _row_tile(bp * t, 256)).reshape(bp, t, d)
    y_sample = _combine(hres_s, gate_s, dest_s, yb, _row_tile(bs, 256)).reshape(bs, 1, d)

    sb_k_sample = k_s.reshape(bs, 1, SB_HEADS, HEAD_DIM)
    sb_v_sample = v_s.reshape(bs, 1, SB_HEADS, HEAD_DIM)
    return (y_prompt, y_sample, heads_major(k_p), heads_major(v_p), heads_major(mem_k_hm), heads_major(mem_v_hm),
            conv_prompt, lru_prompt, sb_k_sample, sb_v_sample, conv_sample, lru_sample)
```

```python
import functools
import math

import jax
import jax.numpy as jnp
from jax import lax
from jax.experimental import pallas as pl
from jax.experimental.pallas import tpu as pltpu

F32 = jnp.float32
BF16 = jnp.bfloat16

HEAD_DIM = 128
SB_HEADS = 6
SB_W = SB_HEADS * HEAD_DIM
LRU_HEADS = 6
LRU_W = LRU_HEADS * HEAD_DIM
LRU_C = 8.0
CONV_W = 4
MEM_HEADS = 4
MEM_W = MEM_HEADS * HEAD_DIM
N_BRANCH = 3
N_EXPERTS = 32
TOP_K = 4
SWIGLU_LIMIT = 7.0
SWIGLU_ALPHA = 1.702
EPS = 1e-6
QK_SCALE = HEAD_DIM ** -0.5

COL_Q, COL_K, COL_V = 0, SB_W, 2 * SB_W
COL_XL, COL_GL = 3 * SB_W, 3 * SB_W + LRU_W
COL_QM = 3 * SB_W + 2 * LRU_W
COL_GATE = COL_QM + MEM_W

MIB = 1024 * 1024
PROJ_TN = 256
ROW_TILE = 1024
SB_BLK = 256
SB_HEADS_PER_STEP = 6
LRU_TT = 256
MOE_SUB = 128
MOE_BLOCK_SUBS = (9, 8, 1)
MOE_CHUNK_SUBS = 9
MOE_TF = 256
MOE_TN = 512
PROJ_SUB = 256
MERGE_SUB = 256
POST_TM = 512
POST_SUB = 256
COMBINE_TM = 256
DISPATCH_TM = 128
WIDE_TN = 1024


def _cparams(sem, vmem_mib):
    return pltpu.CompilerParams(dimension_semantics=sem, vmem_limit_bytes=vmem_mib * MIB)


def _softplus(z):
    return jnp.maximum(z, 0.0) + jnp.log1p(jnp.exp(-jnp.abs(z)))


def _softplus_fast(z):
    return jnp.maximum(z, 0.0) + jnp.log(1.0 + jnp.exp(-jnp.abs(z)))


def _sigmoid(z):
    return 1.0 / (1.0 + jnp.exp(-z))


def _split_bf16(x):
    hi = x.astype(BF16)
    lo = (x - hi.astype(F32)).astype(BF16)
    return hi, lo


def _dot(a, b):
    return jnp.dot(a, b, preferred_element_type=F32)


def _dot_nt(a, b):
    return lax.dot_general(a, b, (((1,), (1,)), ((), ())), preferred_element_type=F32)


def _rmsnorm_kernel(x_ref, g_ref, o_ref):
    x = x_ref[...]
    ms = jnp.mean(x * x, axis=-1, keepdims=True)
    o_ref[...] = (x * lax.rsqrt(ms + EPS) * g_ref[...]).astype(o_ref.dtype)


def _rmsnorm(x, g, tm):
    r, d = x.shape
    assert r % tm == 0
    return pl.pallas_call(
        _rmsnorm_kernel,
        out_shape=jax.ShapeDtypeStruct((r, d), BF16),
        grid=(r // tm,),
        in_specs=[pl.BlockSpec((tm, d), lambda i: (i, 0)), pl.BlockSpec((1, d), lambda i: (0, 0))],
        out_specs=pl.BlockSpec((tm, d), lambda i: (i, 0)),
        compiler_params=_cparams(("parallel",), 32),
        name="rmsnorm",
    )(x, g.reshape(1, d))


def _cast_kernel(x_ref, o_ref, *, gap_lo, gap_hi):
    j = pl.program_id(0)
    is_gap = (j >= gap_lo) & (j < gap_hi)

    @pl.when(is_gap)
    def _():
        o_ref[...] = jnp.zeros_like(o_ref)

    @pl.when(jnp.logical_not(is_gap))
    def _():
        o_ref[...] = x_ref[...].astype(o_ref.dtype)


def _cast_bf16(w, gap_at=0, gap=0):
    r, c = w.shape
    tc = PROJ_TN
    assert c % tc == 0 and gap_at % tc == 0 and gap % tc == 0
    g0, g1 = gap_at // tc, (gap_at + gap) // tc

    def src_tile(j):
        return jnp.where(j < g1, jnp.minimum(j, max(g0 - 1, 0)), j - (g1 - g0))

    return pl.pallas_call(
        functools.partial(_cast_kernel, gap_lo=g0, gap_hi=g1),
        out_shape=jax.ShapeDtypeStruct((r, c + gap), BF16),
        grid=((c + gap) // tc,),
        in_specs=[pl.BlockSpec((r, tc), lambda j: (0, src_tile(j)))],
        out_specs=pl.BlockSpec((r, tc), lambda j: (0, j)),
        compiler_params=_cparams(("parallel",), 32),
        name="cast_bf16",
    )(w)


def _proj_kernel(xn_ref, w_ref, gain_ref, *o_refs, patterns, heads_out):
    j = pl.program_id(1)
    tm = xn_ref.shape[0]
    nh = w_ref.shape[1] // HEAD_DIM
    w = w_ref[...].astype(BF16)
    sub = min(tm, PROJ_SUB) if (heads_out or any(any(p) for p in patterns)) else tm
    ys = [(r0, _dot(xn_ref[r0:r0 + sub, :], w)) for r0 in range(0, tm, sub)]

    def store(r0, c, val):
        for o_ref in o_refs:
            if heads_out:
                o_ref[c, r0:r0 + sub, :] = val.astype(o_ref.dtype)
            else:
                o_ref[r0:r0 + sub, c * HEAD_DIM:(c + 1) * HEAD_DIM] = val.astype(o_ref.dtype)

    def emit(pattern):
        gain = gain_ref[...] if any(pattern) else None
        for r0, y in ys:
            for c in range(nh):
                yc = y[:, c * HEAD_DIM:(c + 1) * HEAD_DIM]
                if pattern[c]:
                    ms = jnp.mean(yc * yc, axis=-1, keepdims=True)
                    yc = yc * lax.rsqrt(ms + EPS) * gain[:, c * HEAD_DIM:(c + 1) * HEAD_DIM]
                store(r0, c, yc)

    distinct = sorted(set(patterns))
    if len(distinct) == 1:
        emit(distinct[0])
    else:
        for pattern in distinct:
            tiles = [jj for jj, p in enumerate(patterns) if p == pattern]
            pred = functools.reduce(jnp.logical_or, [j == jj for jj in tiles])
            pl.when(pred)(functools.partial(emit, pattern))


def _proj(xn, w, gain, *, col0=0, ncols, norm_cols=(0, 0), tm, tn, heads_out=None, bf16_copy=False):
    r, d = xn.shape
    assert r % tm == 0 and col0 % tn == 0 and ncols % tn == 0 and tn % HEAD_DIM == 0
    assert norm_cols[0] % HEAD_DIM == 0 and norm_cols[1] % HEAD_DIM == 0
    col_tile0 = col0 // tn
    patterns = tuple(tuple(norm_cols[0] <= jj * tn + c * HEAD_DIM < norm_cols[1] for c in range(tn // HEAD_DIM))
                     for jj in range(ncols // tn))
    if gain is None:
        gain = jnp.ones((ncols,), F32)
    if heads_out is None:
        out_shape = jax.ShapeDtypeStruct((r, ncols), F32)
        out_spec = pl.BlockSpec((tm, tn), lambda i, j: (i, j))
    else:
        bsz, t = heads_out
        assert bsz * t == r and t % tm == 0
        nt = t // tm
        hpt = tn // HEAD_DIM
        out_shape = jax.ShapeDtypeStruct((bsz, ncols // HEAD_DIM, t, HEAD_DIM), F32)
        out_spec = pl.BlockSpec((None, hpt, tm, HEAD_DIM), lambda i, j: (i // nt, j, i % nt, 0))
    kern = functools.partial(_proj_kernel, patterns=patterns, heads_out=heads_out is not None)
    if bf16_copy:
        out_shape = (out_shape, jax.ShapeDtypeStruct(out_shape.shape, BF16))
        out_spec = (out_spec, out_spec)
    return pl.pallas_call(
        kern,
        out_shape=out_shape,
        grid=(r // tm, ncols // tn),
        in_specs=[
            pl.BlockSpec((tm, d), lambda i, j: (i, 0)),
            pl.BlockSpec((d, tn), lambda i, j: (0, j + col_tile0)),
            pl.BlockSpec((1, tn), lambda i, j: (0, j)),
        ],
        out_specs=out_spec,
        compiler_params=_cparams(("parallel", "parallel"), 40),
        name="proj",
    )(xn, w, gain.reshape(1, ncols))


def _sb_prompt_kernel(bias_ref, q_ref, kb_ref, vb_ref, o_ref):
    hg = pl.program_id(1)
    qi = pl.program_id(2)
    blk = q_ref.shape[0]
    nh = kb_ref.shape[0]

    qs = [q_ref[:, j * HEAD_DIM:(j + 1) * HEAD_DIM] for j in range(nh)]
    biases = [bias_ref[hg * nh + j] for j in range(nh)]
    row = lax.broadcasted_iota(jnp.int32, (blk, blk), 0)
    col = lax.broadcasted_iota(jnp.int32, (blk, blk), 1)
    tri = jnp.where(row >= col, 1.0, 0.0).astype(BF16)
    causal = col < row

    def block(kb, state, masked):
        start = pl.multiple_of(kb * blk, blk)
        heads = range(nh)
        zs = [_dot_nt(qs[j], kb_ref[j, pl.ds(start, blk), :]) * QK_SCALE + biases[j] for j in heads]
        drops = [_softplus_fast(z) for z in zs]
        if masked:
            drops = [jnp.where(causal, dr, 0.0) for dr in drops]
        splits = [_split_bf16(dr) for dr in drops]
        suffixes = [_dot(hi, tri) + _dot(lo, tri) for hi, lo in splits]
        ws = [jnp.exp(zs[j] - suffixes[j] - state[j][0]) for j in heads]
        if masked:
            ws = [jnp.where(causal, w, 0.0) for w in ws]
        pvs = [_dot(ws[j].astype(BF16), vb_ref[j, pl.ds(start, blk), :]) for j in heads]
        return tuple((state[j][0] + suffixes[j][:, 0:1], state[j][1] + pvs[j]) for j in heads)

    zero = (jnp.zeros((blk, 1), F32), jnp.zeros((blk, HEAD_DIM), F32))
    state = block(qi, (zero,) * nh, True)
    state = lax.fori_loop(0, qi, lambda it, st: block(qi - 1 - it, st, False), state)
    for j in range(nh):
        o_ref[:, j * HEAD_DIM:(j + 1) * HEAD_DIM] = state[j][1].astype(o_ref.dtype)


def _sb_prompt(q, k, v, sb_bias):
    bsz, heads, t, _ = k.shape
    blk = SB_BLK
    nh = SB_HEADS_PER_STEP
    assert t % blk == 0 and heads % nh == 0
    assert q.dtype == BF16 and k.dtype == BF16 and v.dtype == BF16
    nq = t // blk
    kv_spec = pl.BlockSpec((None, nh, t, HEAD_DIM), lambda b, h, i: (b, h, 0, 0))
    return pl.pallas_call(
        _sb_prompt_kernel,
        out_shape=jax.ShapeDtypeStruct((bsz * t, SB_W), BF16),
        grid=(bsz, heads // nh, nq),
        in_specs=[
            pl.BlockSpec(memory_space=pltpu.SMEM),
            pl.BlockSpec((blk, nh * HEAD_DIM), lambda b, h, i: (b * nq + i, h)),
            kv_spec, kv_spec,
        ],
        out_specs=pl.BlockSpec((blk, nh * HEAD_DIM), lambda b, h, i: (b * nq + i, h)),
        compiler_params=_cparams(("parallel", "parallel", "parallel"), 40),
        name="sb_prompt",
    )(sb_bias, q, k, v)


def _sb_sample_kernel(pt_ref, bias_ref, q_ref, *refs, n_pages):
    del pt_ref
    k_refs = refs[:n_pages]
    v_refs = refs[n_pages:2 * n_pages]
    o_ref = refs[2 * n_pages]
    heads, page = k_refs[0].shape[:2]
    rep = 8
    n = n_pages * rep

    row = lax.broadcasted_iota(jnp.int32, (page, page), 0)
    col = lax.broadcasted_iota(jnp.int32, (page, page), 1)
    tri = jnp.where(row >= col, 1.0, 0.0).astype(BF16)
    rn = lax.broadcasted_iota(jnp.int32, (n, n), 0)
    cn = lax.broadcasted_iota(jnp.int32, (n, n), 1)
    later = jnp.where(((rn % rep) == (cn % rep)) & (cn // rep > rn // rep), 1.0, 0.0).astype(BF16)
    q = q_ref[0]
    zs = []
    for h in range(heads):
        q8 = jnp.broadcast_to(q[h:h + 1, :], (rep, HEAD_DIM)).astype(BF16)
        z = jnp.concatenate([_dot_nt(q8, k_refs[p][h].astype(BF16)) for p in range(n_pages)], axis=0)
        zs.append(z * QK_SCALE + bias_ref[h])
    z = jnp.concatenate(zs, axis=0)
    hi, lo = _split_bf16(-_softplus_fast(z))
    suffix = _dot(hi, tri) + _dot(lo, tri)
    shi, slo = _split_bf16(suffix)
    carry = jnp.concatenate(
        [(_dot(later, shi[h * n:(h + 1) * n]) + _dot(later, slo[h * n:(h + 1) * n]))[:, 0:1] for h in range(heads)],
        axis=0)
    w = jnp.exp(z + suffix + carry).astype(BF16)
    outs = []
    for h in range(heads):
        acc = jnp.zeros((rep, HEAD_DIM), F32)
        for p in range(n_pages):
            r0 = h * n + p * rep
            acc = acc + _dot(w[r0:r0 + rep, :], v_refs[p][h].astype(BF16))
        outs.append(acc[0:1, :])
    o_ref[0] = jnp.concatenate(outs, axis=0).astype(o_ref.dtype)


def _sb_sample(q, sb_bias, cache_k, cache_v, page_table):
    bs, n_pages = page_table.shape
    heads, page = cache_k.shape[1:3]
    assert heads == SB_HEADS and cache_k.shape[3] == HEAD_DIM

    def page_spec(p):
        return pl.BlockSpec((None, heads, page, HEAD_DIM), lambda b, pt: (pt[b, p], 0, 0, 0))

    grid_spec = pltpu.PrefetchScalarGridSpec(
        num_scalar_prefetch=1,
        grid=(bs,),
        in_specs=[pl.BlockSpec(memory_space=pltpu.SMEM),
                  pl.BlockSpec((1, heads, HEAD_DIM), lambda b, pt: (b, 0, 0))]
                 + [page_spec(p) for p in range(n_pages)] * 2,
        out_specs=pl.BlockSpec((1, heads, HEAD_DIM), lambda b, pt: (b, 0, 0)),
    )
    out = pl.pallas_call(
        functools.partial(_sb_sample_kernel, n_pages=n_pages),
        out_shape=jax.ShapeDtypeStruct((bs, heads, HEAD_DIM), BF16),
        grid_spec=grid_spec,
        compiler_params=_cparams(("arbitrary",), 40),
        name="sb_sample",
    )(page_table, sb_bias, q.reshape(bs, heads, HEAD_DIM), *([cache_k] * n_pages), *([cache_v] * n_pages))
    return out.reshape(bs, SB_W)


def _lru_gates(xc, wa_ref, ba, wx_ref, bx, nsp_lambda):
    xb = xc.astype(BF16)
    r_parts, i_parts = [], []
    for h in range(LRU_HEADS):
        xh = xb[:, h * HEAD_DIM:(h + 1) * HEAD_DIM]
        r_parts.append(_dot(xh, wa_ref[h].astype(BF16)))
        i_parts.append(_dot(xh, wx_ref[h].astype(BF16)))
    r = _sigmoid(jnp.concatenate(r_parts, axis=1) + ba)
    i = _sigmoid(jnp.concatenate(i_parts, axis=1) + bx)
    log_a = -LRU_C * r * nsp_lambda
    a = jnp.exp(log_a)
    u = jnp.sqrt(1.0 - jnp.exp(2.0 * log_a)) * (i * xc)
    return a, u


def _gelu_tanh(x):
    return 0.5 * x * (1.0 + jnp.tanh(0.7978845608028654 * (x + 0.044715 * (x * x * x))))


def _lru_prompt_kernel(x_ref, gate_ref, prev_ref, h0_ref, cw_ref, cb_ref, wa_ref, ba_ref, wx_ref, bx_ref,
                       lam_ref, o_ref, hlast_ref, xp_ref, h_ref):
    ti = pl.program_id(1)
    tt = x_ref.shape[0]
    pad = 8

    @pl.when(ti == 0)
    def _():
        xp_ref[pad - (CONV_W - 1):pad, :] = prev_ref[0]
        h_ref[...] = h0_ref[0]

    x = x_ref[...]
    xp_ref[pad:pad + tt, :] = x
    cw = cw_ref[...]
    xc = cb_ref[...] + cw[CONV_W - 1:CONV_W, :] * x
    for j in range(CONV_W - 1):
        xc = xc + cw[j:j + 1, :] * xp_ref[pad - (CONV_W - 1) + j:pad - (CONV_W - 1) + j + tt, :]
    xp_ref[pad - (CONV_W - 1):pad, :] = x[tt - (CONV_W - 1):tt, :]

    nsp = _softplus(-lam_ref[...])
    a, u = _lru_gates(xc, wa_ref, ba_ref[...], wx_ref, bx_ref[...], nsp)
    rows = lax.broadcasted_iota(jnp.int32, (tt, LRU_W), 0)
    b = jnp.where(rows == 0, u + a * h_ref[...], u)
    s = 1
    while s < tt:
        keep = rows >= s
        b = jnp.where(keep, a * pltpu.roll(b, s, 0) + b, b)
        if 2 * s < tt:
            a = jnp.where(keep, a * pltpu.roll(a, s, 0), a)
        s *= 2
    h_ref[...] = b[tt - 1:tt, :]
    o_ref[...] = (b * _gelu_tanh(gate_ref[...])).astype(o_ref.dtype)

    @pl.when(ti == pl.num_programs(1) - 1)
    def _():
        hlast_ref[0] = b[tt - 1:tt, :]


def _lru_prompt(proj, conv_prev, h0, conv_w, conv_b, lru_wa, lru_ba, lru_wx, lru_bx, lru_lambda, bsz, t):
    tt = LRU_TT
    assert t % tt == 0
    nt = t // tt
    xl, gl = 0, 1
    full = lambda shape: pl.BlockSpec(shape, lambda b, i: (0,) * len(shape))
    o_lru, h_last = pl.pallas_call(
        _lru_prompt_kernel,
        out_shape=(jax.ShapeDtypeStruct((bsz * t, LRU_W), BF16),
                   jax.ShapeDtypeStruct((bsz, 1, LRU_W), F32)),
        grid=(bsz, nt),
        in_specs=[
            pl.BlockSpec((tt, LRU_W), lambda b, i: (b * nt + i, xl)),
            pl.BlockSpec((tt, LRU_W), lambda b, i: (b * nt + i, gl)),
            pl.BlockSpec((1, CONV_W - 1, LRU_W), lambda b, i: (b, 0, 0)),
            pl.BlockSpec((1, 1, LRU_W), lambda b, i: (b, 0, 0)),
            full((CONV_W, LRU_W)), full((1, LRU_W)),
            full((LRU_HEADS, HEAD_DIM, HEAD_DIM)), full((1, LRU_W)),
            full((LRU_HEADS, HEAD_DIM, HEAD_DIM)), full((1, LRU_W)),
            full((1, LRU_W)),
        ],
        out_specs=(pl.BlockSpec((tt, LRU_W), lambda b, i: (b * nt + i, 0)),
                   pl.BlockSpec((1, 1, LRU_W), lambda b, i: (b, 0, 0))),
        scratch_shapes=[pltpu.VMEM((8 + tt, LRU_W), F32), pltpu.VMEM((1, LRU_W), F32)],
        compiler_params=_cparams(("parallel", "arbitrary"), 32),
        name="lru_prompt",
    )(proj, proj, conv_prev, h0.reshape(bsz, 1, LRU_W), conv_w, conv_b.reshape(1, LRU_W),
      lru_wa, lru_ba.reshape(1, LRU_W), lru_wx, lru_bx.reshape(1, LRU_W), lru_lambda.reshape(1, LRU_W))
    return o_lru, h_last.reshape(bsz, LRU_W)


def _lru_step_kernel(x_ref, gate_ref, p0_ref, p1_ref, p2_ref, h0_ref, cw_ref, cb_ref, wa_ref, ba_ref, wx_ref,
                     bx_ref, lam_ref, o_ref, h_ref):
    cw = cw_ref[...]
    xc = (cb_ref[...] + cw[0:1, :] * p0_ref[...] + cw[1:2, :] * p1_ref[...] + cw[2:3, :] * p2_ref[...]
          + cw[3:4, :] * x_ref[...])
    nsp = _softplus(-lam_ref[...])
    a, u = _lru_gates(xc, wa_ref, ba_ref[...], wx_ref, bx_ref[...], nsp)
    h = u + a * h0_ref[...]
    h_ref[...] = h
    o_ref[...] = (h * _gelu_tanh(gate_ref[...])).astype(o_ref.dtype)


def _lru_step(x_lru, gate_lru, state_conv, h0, conv_w, conv_b, lru_wa, lru_ba, lru_wx, lru_bx, lru_lambda):
    bs = x_lru.shape[0]
    assert CONV_W == 4
    prevs = [state_conv[:, j, :] for j in range(CONV_W - 1)]
    o_lru, h_new = pl.pallas_call(
        _lru_step_kernel,
        out_shape=(jax.ShapeDtypeStruct((bs, LRU_W), BF16), jax.ShapeDtypeStruct((bs, LRU_W), F32)),
        name="lru_step",
    )(x_lru, gate_lru, *prevs, h0, conv_w, conv_b.reshape(1, LRU_W), lru_wa, lru_ba.reshape(1, LRU_W),
      lru_wx, lru_bx.reshape(1, LRU_W), lru_lambda.reshape(1, LRU_W))
    return o_lru, h_new


def _mem_prompt_kernel(q_ref, k_ref, v_ref, o_ref):
    s = _dot_nt(q_ref[...].astype(BF16), k_ref[...].astype(BF16)) * QK_SCALE
    m = jnp.max(s, axis=-1, keepdims=True)
    e = jnp.exp(s - m)
    p = e / jnp.sum(e, axis=-1, keepdims=True)
    o_ref[...] = _dot(p.astype(BF16), v_ref[...].astype(BF16)).astype(o_ref.dtype)


def _mem_prompt(proj, qcol0, mem_k, mem_v, t):
    bsz, _, n_mem, _ = mem_k.shape
    tq = min(t, 512)
    assert t % tq == 0 and qcol0 % HEAD_DIM == 0
    nt = t // tq
    qc = qcol0 // HEAD_DIM
    kv_spec = pl.BlockSpec((None, None, n_mem, HEAD_DIM), lambda b, h, i: (b, h, 0, 0))
    return pl.pallas_call(
        _mem_prompt_kernel,
        out_shape=jax.ShapeDtypeStruct((bsz * t, MEM_W), BF16),
        grid=(bsz, MEM_HEADS, nt),
        in_specs=[pl.BlockSpec((tq, HEAD_DIM), lambda b, h, i: (b * nt + i, qc + h)), kv_spec, kv_spec],
        out_specs=pl.BlockSpec((tq, HEAD_DIM), lambda b, h, i: (b * nt + i, h)),
        compiler_params=_cparams(("parallel", "parallel", "parallel"), 32),
        name="mem_prompt",
    )(proj, mem_k, mem_v)


def _mem_sample_kernel(q_ref, k_ref, v_ref, o_ref):
    group = q_ref.shape[0]
    for g in range(group):
        q = q_ref[g]
        s = jnp.sum(k_ref[g] * q[None], axis=-1, keepdims=True) * QK_SCALE
        m = jnp.max(s, axis=0, keepdims=True)
        e = jnp.exp(s - m)
        p = e / jnp.sum(e, axis=0, keepdims=True)
        o_ref[g] = jnp.sum(p * v_ref[g], axis=0).astype(o_ref.dtype)


def _mem_sample(q, cache_k, cache_v):
    bs, n_mem, heads, _ = cache_k.shape
    assert heads == MEM_HEADS and cache_k.shape[3] == HEAD_DIM
    group = 8
    assert bs % group == 0
    kv_spec = pl.BlockSpec((group, n_mem, heads, HEAD_DIM), lambda i: (i, 0, 0, 0))
    out = pl.pallas_call(
        _mem_sample_kernel,
        out_shape=jax.ShapeDtypeStruct((bs, heads, HEAD_DIM), BF16),
        grid=(bs // group,),
        in_specs=[pl.BlockSpec((group, heads, HEAD_DIM), lambda i: (i, 0, 0)), kv_spec, kv_spec],
        out_specs=pl.BlockSpec((group, heads, HEAD_DIM), lambda i: (i, 0, 0)),
        compiler_params=_cparams(("parallel",), 32),
        name="mem_sample",
    )(q.reshape(bs, heads, HEAD_DIM), cache_k, cache_v)
    return out.reshape(bs, MEM_W)


def _merge_kernel(oa_ref, ol_ref, om_ref, pa_ref, pl_ref, pm_ref, ga_ref, gl_ref, gm_ref, o_ref):
    tm = o_ref.shape[0]
    sub = min(tm, MERGE_SUB)
    pa = pa_ref[...].astype(BF16)
    pl_ = pl_ref[...].astype(BF16)
    pm = pm_ref[...].astype(BF16)
    for r0 in range(0, tm, sub):
        rows = slice(r0, r0 + sub)
        ya = _dot(oa_ref[rows, :], pa)
        yl = _dot(ol_ref[rows, :], pl_)
        ym = _dot(om_ref[rows, :], pm)
        merged = (_sigmoid(ga_ref[rows, :]) * ya + _sigmoid(gl_ref[rows, :]) * yl
                  + _sigmoid(gm_ref[rows, :]) * ym)
        o_ref[rows, :] = merged.astype(o_ref.dtype)


def _merge(o_sb, o_lru, o_mem, p_attn, p_lru, p_mem, gates, tm):
    r = o_sb.shape[0]
    d = p_attn.shape[1]
    tn = 512
    assert r % tm == 0 and d % tn == 0
    nj = d // tn
    return pl.pallas_call(
        _merge_kernel,
        out_shape=jax.ShapeDtypeStruct((r, d), BF16),
        grid=(r // tm, nj),
        in_specs=[
            pl.BlockSpec((tm, SB_W), lambda i, j: (i, 0)),
            pl.BlockSpec((tm, LRU_W), lambda i, j: (i, 0)),
            pl.BlockSpec((tm, MEM_W), lambda i, j: (i, 0)),
            pl.BlockSpec((SB_W, tn), lambda i, j: (0, j)),
            pl.BlockSpec((LRU_W, tn), lambda i, j: (0, j)),
            pl.BlockSpec((MEM_W, tn), lambda i, j: (0, j)),
            pl.BlockSpec((tm, tn), lambda i, j: (i, j)),
            pl.BlockSpec((tm, tn), lambda i, j: (i, nj + j)),
            pl.BlockSpec((tm, tn), lambda i, j: (i, 2 * nj + j)),
        ],
        out_specs=pl.BlockSpec((tm, tn), lambda i, j: (i, j)),
        compiler_params=_cparams(("parallel", "parallel"), 44),
        name="merge",
    )(o_sb, o_lru, o_mem, p_attn, p_lru, p_mem, gates, gates, gates)


def _post_kernel(x_ref, m_ref, wo_ref, g_ref, wr_ref, br_ref, cnt0_ref,
                 hres_ref, hnp_ref, idx_ref, gate_ref, rank_ref, cnt_ref):
    i = pl.program_id(0)
    tm, d = x_ref.shape
    sub = min(tm, POST_SUB)
    starts = range(0, tm, sub)

    @pl.when(i == 0)
    def _():
        cnt_ref[...] = cnt0_ref[...]

    hres_all = [x_ref[r0:r0 + sub, :] + _dot(m_ref[r0:r0 + sub, :], wo_ref[...]) for r0 in starts]
    eid = lax.broadcasted_iota(jnp.int32, (wr_ref.shape[0], sub), 0)
    ne = eid.shape[0]
    r_ = lax.broadcasted_iota(jnp.int32, (sub, sub), 0)
    c_ = lax.broadcasted_iota(jnp.int32, (sub, sub), 1)
    before = jnp.where(r_ < c_, 1.0, 0.0).astype(BF16)
    cnt = cnt_ref[...]
    for r0, hres in zip(starts, hres_all):
        hres_ref[r0:r0 + sub, :] = hres
        ms = jnp.mean(hres * hres, axis=-1, keepdims=True)
        hn = hres * lax.rsqrt(ms + EPS) * g_ref[...]
        hb = hn.astype(BF16)
        wa = lax.bitcast_convert_type(hb[:, :d // 2].astype(F32), jnp.uint32)
        wb = lax.bitcast_convert_type(hb[:, d // 2:].astype(F32), jnp.uint32)
        hnp_ref[r0:r0 + sub, :] = wa | (wb >> 16)

        work = _dot_nt(wr_ref[...], hb) + br_ref[...]
        vals, idxs, onehots = [], [], []
        for _ in range(TOP_K):
            mx = jnp.max(work, axis=0, keepdims=True)
            sel = jnp.min(jnp.where(work == mx, eid, ne), axis=0, keepdims=True)
            oh = eid == sel
            vals.append(mx)
            idxs.append(sel)
            onehots.append(oh)
            work = jnp.where(oh, -jnp.inf, work)
        es = [jnp.exp(v - vals[0]) for v in vals]
        den = functools.reduce(jnp.add, es)
        gate_ref[:, r0:r0 + sub] = jnp.concatenate([e / den for e in es], axis=0)
        idx_ref[:, r0:r0 + sub] = jnp.concatenate(idxs, axis=0)

        chosen_f = jnp.where(functools.reduce(jnp.logical_or, onehots), 1.0, 0.0)
        prior = _dot(chosen_f.astype(BF16), before) + cnt
        rank_ref[:, r0:r0 + sub] = jnp.concatenate(
            [jnp.sum(jnp.where(oh, prior, 0.0), axis=0, keepdims=True) for oh in onehots], axis=0).astype(jnp.int32)
        cnt = cnt + jnp.sum(chosen_f, axis=1, keepdims=True)
    cnt_ref[...] = cnt


def _post(x, merged, wo_b, g, wr_t, b_router, cnt0, tm):
    r, d = x.shape
    ne = wr_t.shape[0]
    assert r % tm == 0
    full = lambda shape: pl.BlockSpec(shape, lambda i: (0,) * len(shape))
    return pl.pallas_call(
        _post_kernel,
        out_shape=(jax.ShapeDtypeStruct((r, d), F32),
                   jax.ShapeDtypeStruct((r, d // 2), jnp.uint32),
                   jax.ShapeDtypeStruct((TOP_K, r), jnp.int32),
                   jax.ShapeDtypeStruct((TOP_K, r), F32),
                   jax.ShapeDtypeStruct((TOP_K, r), jnp.int32),
                   jax.ShapeDtypeStruct((ne, 1), F32)),
        grid=(r // tm,),
        in_specs=[pl.BlockSpec((tm, d), lambda i: (i, 0)),
                  pl.BlockSpec((tm, d), lambda i: (i, 0)),
                  full((d, d)), full((1, d)), full((ne, d)), full((ne, 1)), full((ne, 1))],
        out_specs=(pl.BlockSpec((tm, d), lambda i: (i, 0)),
                   pl.BlockSpec((tm, d // 2), lambda i: (i, 0)),
                   pl.BlockSpec((TOP_K, tm), lambda i: (0, i)),
                   pl.BlockSpec((TOP_K, tm), lambda i: (0, i)),
                   pl.BlockSpec((TOP_K, tm), lambda i: (0, i)),
                   full((ne, 1))),
        compiler_params=_cparams(("arbitrary",), 56),
        name="post",
    )(x, merged, wo_b, g.reshape(1, d), wr_t, b_router.reshape(ne, 1), cnt0)


def _dispatch_kernel(tail_ref, dest_ref, hn_a_ref, hn_b_ref, xs_ref, zero_ref, sem, zsem, *, tiles_a):
    i = pl.program_id(0)
    tm = dest_ref.shape[2] // TOP_K

    @pl.when(i == 0)
    def _():
        zero_ref[...] = jnp.zeros_like(zero_ref)

        def tail_copy(e):
            start = pl.multiple_of(tail_ref[e], MOE_SUB)
            return pltpu.make_async_copy(zero_ref, xs_ref.at[pl.ds(start, MOE_SUB)], zsem)

        def zstart(e, c):
            pl.when(tail_ref[e] >= 0)(lambda: tail_copy(e).start())
            return c

        def zwait(e, c):
            pl.when(tail_ref[e] >= 0)(lambda: tail_copy(e).wait())
            return c

        lax.fori_loop(0, tail_ref.shape[0], zstart, 0)
        lax.fori_loop(0, tail_ref.shape[0], zwait, 0)

    def row_copy(src_ref, row, t, k):
        return pltpu.make_async_copy(src_ref.at[pl.ds(row, 1)],
                                     xs_ref.at[pl.ds(dest_ref[0, 0, k * tm + t], 1)], sem)

    def start_tile(src_ref, first_row):
        def body(t, c):
            for k in range(TOP_K):
                row_copy(src_ref, first_row + t, t, k).start()
            return c
        lax.fori_loop(0, tm, body, 0)

    def wait_tile():
        def body(t, c):
            for k in range(TOP_K):
                row_copy(hn_a_ref, 0, t, k).wait()
            return c
        lax.fori_loop(0, tm, body, 0)

    pl.when(i < tiles_a)(lambda: start_tile(hn_a_ref, i * tm))
    pl.when(i >= tiles_a)(lambda: start_tile(hn_b_ref, (i - tiles_a) * tm))
    pl.when(i > 0)(wait_tile)
    pl.when(i == pl.num_programs(0) - 1)(wait_tile)


def _dispatch(hnp_a, hnp_b, dest, tail, n_rows, tm):
    (ra, w), rb_ = hnp_a.shape, hnp_b.shape[0]
    assert ra % tm == 0 and rb_ % tm == 0 and hnp_b.shape[1] == w and dest.shape[1] == ra + rb_
    nt = (ra + rb_) // tm
    dest_t = dest.reshape(TOP_K, nt, tm).transpose(1, 0, 2).reshape(nt, 1, TOP_K * tm)
    grid_spec = pltpu.PrefetchScalarGridSpec(
        num_scalar_prefetch=1,
        grid=(nt,),
        in_specs=[pl.BlockSpec((1, 1, TOP_K * tm), lambda i, tl: (i, 0, 0), memory_space=pltpu.SMEM),
                  pl.BlockSpec(memory_space=pl.ANY), pl.BlockSpec(memory_space=pl.ANY)],
        out_specs=pl.BlockSpec(memory_space=pl.ANY),
        scratch_shapes=[pltpu.VMEM((MOE_SUB, w), hnp_a.dtype), pltpu.SemaphoreType.DMA, pltpu.SemaphoreType.DMA],
    )
    return pl.pallas_call(
        functools.partial(_dispatch_kernel, tiles_a=ra // tm),
        out_shape=jax.ShapeDtypeStruct((n_rows, w), hnp_a.dtype),
        grid_spec=grid_spec,
        compiler_params=_cparams(("arbitrary",), 16),
        name="dispatch",
    )(tail, dest_t, hnp_a, hnp_b)


def _moe_kernel(ce_ref, cb_ref, ns_ref, xs_ref, wg_ref, wu_ref, bg_ref, bu_ref, wd_ref, bd_ref, o_ref,
                xa_ref, xb_ref):
    del ce_ref, cb_ref
    c = pl.program_id(0)
    s = pl.program_id(1)
    nsub = ns_ref[c]
    half = xa_ref.shape[1]
    d = o_ref.shape[1]

    def for_row_blocks(fn):
        done = 0
        for size in MOE_BLOCK_SUBS:
            count = (nsub - done) // size
            rows = size * MOE_SUB

            def body(i, carry, done=done, rows=rows):
                fn(pl.ds(pl.multiple_of(done * MOE_SUB + i * rows, MOE_SUB), rows))
                return carry

            lax.fori_loop(0, count, body, 0)
            done = done + count * size

    @pl.when(s == 0)
    def _():
        def unpack(rows):
            word = xs_ref[rows, :]
            xa_ref[rows, :] = lax.bitcast_convert_type(word & jnp.uint32(0xFFFF0000), F32).astype(BF16)
            xb_ref[rows, :] = lax.bitcast_convert_type(word << 16, F32).astype(BF16)
            o_ref[rows, :] = jnp.broadcast_to(bd_ref[...], (rows.size, d))
        for_row_blocks(unpack)

    bg = bg_ref[...]
    bu = bu_ref[...]

    def step(rows):
        xa = xa_ref[rows, :]
        xb = xb_ref[rows, :]
        gate = (_dot(xa, wg_ref[:half, :].astype(BF16)) + _dot(xb, wg_ref[half:, :].astype(BF16)) + bg)
        upv = (_dot(xa, wu_ref[:half, :].astype(BF16)) + _dot(xb, wu_ref[half:, :].astype(BF16)) + bu)
        gate = jnp.minimum(gate, SWIGLU_LIMIT)
        upv = jnp.clip(upv, -SWIGLU_LIMIT, SWIGLU_LIMIT)
        act = (gate * _sigmoid(SWIGLU_ALPHA * gate) * (upv + 1.0)).astype(BF16)
        for n0 in range(0, d, MOE_TN):
            o_ref[rows, n0:n0 + MOE_TN] += _dot(act, wd_ref[:, n0:n0 + MOE_TN].astype(BF16))
    for_row_blocks(step)


def _moe(xs, n_used, chunk_expert, chunk_block, chunk_nsub, w_up, b_up, w_down, b_down):
    ne, d, dff2 = w_up.shape
    dff = dff2 // 2
    rb = MOE_SUB * MOE_CHUNK_SUBS
    n_chunks = xs.shape[0] // rb
    tf = MOE_TF
    nf = dff // tf
    assert dff % tf == 0 and d % MOE_TN == 0

    grid_spec = pltpu.PrefetchScalarGridSpec(
        num_scalar_prefetch=3,
        grid=(n_used, nf),
        in_specs=[
            pl.BlockSpec((rb, d // 2), lambda c, s, ce, cb, ns: (cb[c], 0)),
            pl.BlockSpec((None, d, tf), lambda c, s, ce, cb, ns: (ce[c], 0, s)),
            pl.BlockSpec((None, d, tf), lambda c, s, ce, cb, ns: (ce[c], 0, nf + s)),
            pl.BlockSpec((None, 1, tf), lambda c, s, ce, cb, ns: (ce[c], 0, s)),
            pl.BlockSpec((None, 1, tf), lambda c, s, ce, cb, ns: (ce[c], 0, nf + s)),
            pl.BlockSpec((None, tf, d), lambda c, s, ce, cb, ns: (ce[c], s, 0)),
            pl.BlockSpec((None, 1, d), lambda c, s, ce, cb, ns: (ce[c], 0, 0)),
        ],
        out_specs=pl.BlockSpec((rb, d), lambda c, s, ce, cb, ns: (cb[c], 0)),
        scratch_shapes=[pltpu.VMEM((rb, d // 2), BF16), pltpu.VMEM((rb, d // 2), BF16)],
    )
    return pl.pallas_call(
        _moe_kernel,
        out_shape=jax.ShapeDtypeStruct((n_chunks * rb, d), F32),
        grid_spec=grid_spec,
        compiler_params=_cparams(("arbitrary", "arbitrary"), 56),
        name="moe",
    )(chunk_expert, chunk_block, chunk_nsub, xs, w_up, w_up, b_up.reshape(ne, 1, dff2), b_up.reshape(ne, 1, dff2),
      w_down, b_down.reshape(ne, 1, d))


def _combine_kernel(dest_ref, dest_next_ref, hres_ref, gate_ref, yb_ref, o_ref, buf_ref, sem):
    i = pl.program_id(0)
    tm = hres_ref.shape[0]
    slot = i % 2

    def row_copy(dref, s, t, k):
        return pltpu.make_async_copy(yb_ref.at[pl.ds(dref[0, 0, k * tm + t], 1)],
                                     buf_ref.at[s, k, pl.ds(t, 1)], sem.at[s])

    def issue(dref, s):
        def body(t, c):
            for k in range(TOP_K):
                row_copy(dref, s, t, k).start()
            return c
        lax.fori_loop(0, tm, body, 0)

    def drain(dref, s):
        def body(t, c):
            for k in range(TOP_K):
                row_copy(dref, s, t, k).wait()
            return c
        lax.fori_loop(0, tm, body, 0)

    pl.when(i == 0)(lambda: issue(dest_ref, slot))
    pl.when(i + 1 < pl.num_programs(0))(lambda: issue(dest_next_ref, 1 - slot))
    drain(dest_ref, slot)
    g = gate_ref[...]
    y = hres_ref[...]
    for k in range(TOP_K):
        y = y + g[:, k:k + 1] * buf_ref[slot, k]
    o_ref[...] = y


def _combine(hres, gates, dest, yb, tm):
    r, d = hres.shape
    assert r % tm == 0
    nt = r // tm
    dest_t = dest.reshape(TOP_K, nt, tm).transpose(1, 0, 2).reshape(nt, 1, TOP_K * tm)
    return pl.pallas_call(
        _combine_kernel,
        out_shape=jax.ShapeDtypeStruct((r, d), F32),
        grid=(nt,),
        in_specs=[pl.BlockSpec((1, 1, TOP_K * tm), lambda i: (i, 0, 0), memory_space=pltpu.SMEM),
                  pl.BlockSpec((1, 1, TOP_K * tm), lambda i: (jnp.minimum(i + 1, nt - 1), 0, 0),
                               memory_space=pltpu.SMEM),
                  pl.BlockSpec((tm, d), lambda i: (i, 0)),
                  pl.BlockSpec((tm, TOP_K), lambda i: (i, 0)),
                  pl.BlockSpec(memory_space=pl.ANY)],
        out_specs=pl.BlockSpec((tm, d), lambda i: (i, 0)),
        scratch_shapes=[pltpu.VMEM((2, TOP_K, tm, d), F32), pltpu.SemaphoreType.DMA((2,))],
        compiler_params=_cparams(("arbitrary",), 40),
        name="combine",
    )(dest_t, dest_t, hres, gates.T, yb)


def _row_tile(r, cap):
    tm = min(r, cap)
    assert r % tm == 0
    return tm


def kernel(x_prompt, x_sample, mem_prompt, cache_sb_k, cache_sb_v, page_table, cache_mem_k, cache_mem_v, state_conv, state_lru, norm_mix_g, norm_mem_g, w_in, q_sb_g, k_sb_g, sb_bias, q_mem_g, k_mem_g, w_mem_kv, conv_w, conv_b, lru_wa, lru_ba, lru_wx, lru_bx, lru_lambda, p_attn, p_lru, p_mem, w_o, norm_ffn_g, w_router, b_router, w_up, b_up, w_down, b_down):
    bp, t, d = x_prompt.shape
    bs, ts, _ = x_sample.shape
    assert ts == 1, "sample group decodes one token per sequence"
    n_mem = mem_prompt.shape[1]
    q_gain = jnp.tile(q_sb_g, SB_HEADS)
    k_gain = jnp.tile(k_sb_g, SB_HEADS)
    qm_gain = jnp.tile(q_mem_g, MEM_HEADS)
    heads_major = lambda a: jnp.transpose(a, (0, 2, 1, 3))

    memn = _rmsnorm(mem_prompt.reshape(bp * n_mem, d), norm_mem_g, _row_tile(bp * n_mem, 256))
    mem_tm = _row_tile(n_mem, 256)
    mem_k_hm = _proj(memn, w_mem_kv, jnp.tile(k_mem_g, MEM_HEADS), col0=0, ncols=MEM_W, norm_cols=(0, MEM_W),
                     tm=mem_tm, tn=MEM_W, heads_out=(bp, n_mem))
    mem_v_hm = _proj(memn, w_mem_kv, None, col0=MEM_W, ncols=MEM_W, tm=mem_tm, tn=MEM_W, heads_out=(bp, n_mem))

    rest_w = COL_GATE - COL_XL
    qm_cols = (COL_QM - COL_XL, rest_w)
    gap = -COL_XL % WIDE_TN
    w_b = _cast_bf16(w_in, gap_at=COL_XL, gap=gap)
    q_at, k_at, v_at, rest_at, gates_at = COL_Q, COL_K, COL_V, COL_XL + gap, COL_GATE + gap
    rest_gain = jnp.concatenate([jnp.ones((2 * LRU_W,), F32), qm_gain])

    xp2 = x_prompt.reshape(bp * t, d)
    tm_p = _row_tile(t, ROW_TILE)
    xn_p = _rmsnorm(xp2, norm_mix_g, _row_tile(bp * t, 512))
    _, q_pb = _proj(xn_p, w_b, q_gain, col0=q_at, ncols=SB_W, norm_cols=(0, SB_W), tm=tm_p, tn=SB_W,
                    bf16_copy=True)
    k_p, k_pb = _proj(xn_p, w_b, k_gain, col0=k_at, ncols=SB_W, norm_cols=(0, SB_W), tm=tm_p, tn=SB_W,
                      heads_out=(bp, t), bf16_copy=True)
    v_p, v_pb = _proj(xn_p, w_b, None, col0=v_at, ncols=SB_W, tm=tm_p, tn=SB_W, heads_out=(bp, t), bf16_copy=True)
    rest_p = _proj(xn_p, w_b, rest_gain, col0=rest_at, ncols=rest_w, norm_cols=qm_cols, tm=tm_p, tn=WIDE_TN)
    gates_p = _proj(xn_p, w_b, None, col0=gates_at, ncols=N_BRANCH * d, tm=tm_p, tn=WIDE_TN)
    o_sb_p = _sb_prompt(q_pb, k_pb, v_pb, sb_bias)
    conv0 = jnp.zeros((bp, CONV_W - 1, LRU_W), x_prompt.dtype)
    lru0 = jnp.zeros((bp, LRU_W), state_lru.dtype)
    o_lru_p, lru_prompt = _lru_prompt(rest_p, conv0, lru0, conv_w, conv_b, lru_wa, lru_ba, lru_wx, lru_bx,
                                      lru_lambda, bp, t)
    o_mem_p = _mem_prompt(rest_p, COL_QM - COL_XL, mem_k_hm, mem_v_hm, t)
    xl_tail = rest_p.reshape(bp, t, -1)[:, t - (CONV_W - 1):, :LRU_W]
    conv_prompt = jnp.concatenate([conv0, xl_tail], axis=1)[:, -(CONV_W - 1):]

    xs2 = x_sample.reshape(bs, d)
    xn_s = _rmsnorm(xs2, norm_mix_g, bs)
    qkv_gain = jnp.concatenate([q_gain, k_gain, jnp.ones((SB_W,), F32)])
    qkv_s = _proj(xn_s, w_b, qkv_gain, col0=q_at, ncols=3 * SB_W, norm_cols=(0, 2 * SB_W), tm=bs, tn=SB_W)
    q_s, k_s, v_s = qkv_s[:, :SB_W], qkv_s[:, SB_W:2 * SB_W], qkv_s[:, 2 * SB_W:]
    rest_s = _proj(xn_s, w_b, rest_gain, col0=rest_at, ncols=rest_w, norm_cols=qm_cols, tm=bs, tn=WIDE_TN)
    gates_s = _proj(xn_s, w_b, None, col0=gates_at, ncols=N_BRANCH * d, tm=bs, tn=WIDE_TN)
    o_sb_s = _sb_sample(q_s, sb_bias, heads_major(cache_sb_k), heads_major(cache_sb_v), page_table)
    xl_s = rest_s[:, :LRU_W]
    o_lru_s, lru_sample = _lru_step(xl_s, rest_s[:, LRU_W:2 * LRU_W], state_conv, state_lru, conv_w, conv_b,
                                    lru_wa, lru_ba, lru_wx, lru_bx, lru_lambda)
    conv_sample = jnp.concatenate([state_conv, xl_s[:, None, :]], axis=1)[:, -(CONV_W - 1):]
    o_mem_s = _mem_sample(rest_s[:, 2 * LRU_W:], cache_mem_k, cache_mem_v)

    wo_b = _cast_bf16(w_o)
    wr_t = w_router.T.astype(BF16)
    merged_p = _merge(o_sb_p, o_lru_p, o_mem_p, p_attn, p_lru, p_mem, gates_p, _row_tile(bp * t, ROW_TILE))
    merged_s = _merge(o_sb_s, o_lru_s, o_mem_s, p_attn, p_lru, p_mem, gates_s, _row_tile(bs, ROW_TILE))
    cnt0 = jnp.zeros((N_EXPERTS, 1), F32)
    hres_p, hnp_p, idx_p, gate_p, rank_p, cnt_p = _post(xp2, merged_p, wo_b, norm_ffn_g, wr_t, b_router,
                                                        cnt0, _row_tile(bp * t, POST_TM))
    hres_s, hnp_s, idx_s, gate_s, rank_s, cnt_all = _post(xs2, merged_s, wo_b, norm_ffn_g, wr_t, b_router,
                                                          cnt_p, _row_tile(bs, POST_TM))

    rb = MOE_SUB * MOE_CHUNK_SUBS
    n_assign = (bp * t + bs) * TOP_K
    n_chunks = -(-n_assign // rb) + N_EXPERTS
    counts = cnt_all[:, 0].astype(jnp.int32)
    chunks_e = (counts + rb - 1) // rb
    chunk_end = jnp.cumsum(chunks_e)
    chunk_start = chunk_end - chunks_e
    used = chunk_end[-1]
    cidx = jnp.arange(n_chunks, dtype=jnp.int32)
    last = jnp.minimum(cidx, used - 1)
    chunk_expert = jnp.clip(jnp.searchsorted(chunk_end, last, side='right'), 0, N_EXPERTS - 1).astype(jnp.int32)
    rows_left = counts[chunk_expert] - (last - chunk_start[chunk_expert]) * rb
    chunk_nsub = jnp.where(cidx < used, (jnp.clip(rows_left, 0, rb) + MOE_SUB - 1) // MOE_SUB, 0).astype(jnp.int32)
    chunk_block = last.astype(jnp.int32)
    row_start = chunk_start * rb
    experts = jnp.arange(N_EXPERTS, dtype=jnp.int32)

    def slots(idx, rank):
        return jnp.sum(jnp.where(idx[..., None] == experts, row_start, 0), axis=-1) + rank

    dest_p = slots(idx_p, rank_p)
    dest_s = slots(idx_s, rank_s)
    tail = jnp.where(counts % MOE_SUB != 0, row_start + counts // MOE_SUB * MOE_SUB, -1).astype(jnp.int32)

    dest = jnp.concatenate([dest_p, dest_s], axis=1)
    xs = _dispatch(hnp_p, hnp_s, dest, tail, n_chunks * rb, math.gcd(bp * t, bs, DISPATCH_TM))
    yb = _moe(xs, used.astype(jnp.int32), chunk_expert, chunk_block, chunk_nsub, w_up, b_up, w_down, b_down)
    y_prompt = _combine(hres_p, gate_p, dest_p, yb, _row_tile(bp * t, COMBINE_TM)).reshape(bp, t, d)
    y_sample = _combine(hres_s, gate_s, dest_s, yb, _row_tile(bs, COMBINE_TM)).reshape(bs, 1, d)

    sb_k_sample = k_s.reshape(bs, 1, SB_HEADS, HEAD_DIM)
    sb_v_sample = v_s.reshape(bs, 1, SB_HEADS, HEAD_DIM)
    return (y_prompt, y_sample, heads_major(k_p), heads_major(v_p), heads_major(mem_k_hm), heads_major(mem_v_hm),
            conv_prompt, lru_prompt, sb_k_sample, sb_v_sample, conv_sample, lru_sample)
```

```python
import functools
import math

import jax
import jax.numpy as jnp
from jax import lax
from jax.experimental import pallas as pl
from jax.experimental.pallas import tpu as pltpu

F32 = jnp.float32
BF16 = jnp.bfloat16

HEAD_DIM = 128
SB_HEADS = 6
SB_W = SB_HEADS * HEAD_DIM
LRU_HEADS = 6
LRU_W = LRU_HEADS * HEAD_DIM
LRU_C = 8.0
CONV_W = 4
MEM_HEADS = 4
MEM_W = MEM_HEADS * HEAD_DIM
N_BRANCH = 3
N_EXPERTS = 32
TOP_K = 4
SWIGLU_LIMIT = 7.0
SWIGLU_ALPHA = 1.702
EPS = 1e-6
QK_SCALE = HEAD_DIM ** -0.5

COL_Q, COL_K, COL_V = 0, SB_W, 2 * SB_W
COL_XL, COL_GL = 3 * SB_W, 3 * SB_W + LRU_W
COL_QM = 3 * SB_W + 2 * LRU_W
COL_GATE = COL_QM + MEM_W

MIB = 1024 * 1024
PROJ_TN = 256
ROW_TILE = 1024
SB_BLK = 256
SB_HEADS_PER_STEP = 6
LRU_TT = 256
MOE_SUB = 128
MOE_BLOCK_SUBS = (9, 8, 1)
MOE_CHUNK_SUBS = 9
MOE_TF = 256
MOE_TN = 512
PROJ_SUB = 256
MERGE_SUB = 256
POST_TM = 512
POST_SUB = 256
COMBINE_TM = 256
DISPATCH_TM = 128
WIDE_TN = 1024


def _cparams(sem, vmem_mib):
    return pltpu.CompilerParams(dimension_semantics=sem, vmem_limit_bytes=vmem_mib * MIB)


def _softplus(z):
    return jnp.maximum(z, 0.0) + jnp.log1p(jnp.exp(-jnp.abs(z)))


def _softplus_fast(z):
    return jnp.maximum(z, 0.0) + jnp.log(1.0 + jnp.exp(-jnp.abs(z)))


def _sigmoid(z):
    return 1.0 / (1.0 + jnp.exp(-z))


def _split_bf16(x):
    hi = x.astype(BF16)
    lo = (x - hi.astype(F32)).astype(BF16)
    return hi, lo


def _dot(a, b):
    return jnp.dot(a, b, preferred_element_type=F32)


def _dot_nt(a, b):
    return lax.dot_general(a, b, (((1,), (1,)), ((), ())), preferred_element_type=F32)


def _rmsnorm_kernel(x_ref, g_ref, o_ref):
    x = x_ref[...]
    ms = jnp.mean(x * x, axis=-1, keepdims=True)
    o_ref[...] = (x * lax.rsqrt(ms + EPS) * g_ref[...]).astype(o_ref.dtype)


def _rmsnorm(x, g, tm):
    r, d = x.shape
    assert r % tm == 0
    return pl.pallas_call(
        _rmsnorm_kernel,
        out_shape=jax.ShapeDtypeStruct((r, d), BF16),
        grid=(r // tm,),
        in_specs=[pl.BlockSpec((tm, d), lambda i: (i, 0)), pl.BlockSpec((1, d), lambda i: (0, 0))],
        out_specs=pl.BlockSpec((tm, d), lambda i: (i, 0)),
        compiler_params=_cparams(("parallel",), 32),
        name="rmsnorm",
    )(x, g.reshape(1, d))


def _cast_kernel(x_ref, o_ref, *, gap_lo, gap_hi):
    j = pl.program_id(0)
    is_gap = (j >= gap_lo) & (j < gap_hi)

    @pl.when(is_gap)
    def _():
        o_ref[...] = jnp.zeros_like(o_ref)

    @pl.when(jnp.logical_not(is_gap))
    def _():
        o_ref[...] = x_ref[...].astype(o_ref.dtype)


def _cast_bf16(w, gap_at=0, gap=0):
    r, c = w.shape
    tc = PROJ_TN
    assert c % tc == 0 and gap_at % tc == 0 and gap % tc == 0
    g0, g1 = gap_at // tc, (gap_at + gap) // tc

    def src_tile(j):
        return jnp.where(j < g1, jnp.minimum(j, max(g0 - 1, 0)), j - (g1 - g0))

    return pl.pallas_call(
        functools.partial(_cast_kernel, gap_lo=g0, gap_hi=g1),
        out_shape=jax.ShapeDtypeStruct((r, c + gap), BF16),
        grid=((c + gap) // tc,),
        in_specs=[pl.BlockSpec((r, tc), lambda j: (0, src_tile(j)))],
        out_specs=pl.BlockSpec((r, tc), lambda j: (0, j)),
        compiler_params=_cparams(("parallel",), 32),
        name="cast_bf16",
    )(w)


def _proj_kernel(xn_ref, w_ref, gain_ref, *o_refs, patterns, heads_out):
    j = pl.program_id(1)
    tm = xn_ref.shape[0]
    nh = w_ref.shape[1] // HEAD_DIM
    w = w_ref[...].astype(BF16)
    sub = min(tm, PROJ_SUB) if (heads_out or any(any(p) for p in patterns)) else tm
    ys = [(r0, _dot(xn_ref[r0:r0 + sub, :], w)) for r0 in range(0, tm, sub)]

    def store(r0, c, val):
        for o_ref in o_refs:
            if heads_out:
                o_ref[c, r0:r0 + sub, :] = val.astype(o_ref.dtype)
            else:
                o_ref[r0:r0 + sub, c * HEAD_DIM:(c + 1) * HEAD_DIM] = val.astype(o_ref.dtype)

    def emit(pattern):
        gain = gain_ref[...] if any(pattern) else None
        for r0, y in ys:
            for c in range(nh):
                yc = y[:, c * HEAD_DIM:(c + 1) * HEAD_DIM]
                if pattern[c]:
                    ms = jnp.mean(yc * yc, axis=-1, keepdims=True)
                    yc = yc * lax.rsqrt(ms + EPS) * gain[:, c * HEAD_DIM:(c + 1) * HEAD_DIM]
                store(r0, c, yc)

    distinct = sorted(set(patterns))
    if len(distinct) == 1:
        emit(distinct[0])
    else:
        for pattern in distinct:
            tiles = [jj for jj, p in enumerate(patterns) if p == pattern]
            pred = functools.reduce(jnp.logical_or, [j == jj for jj in tiles])
            pl.when(pred)(functools.partial(emit, pattern))


def _proj(xn, w, gain, *, col0=0, ncols, norm_cols=(0, 0), tm, tn, heads_out=None, bf16_copy=False):
    r, d = xn.shape
    assert r % tm == 0 and col0 % tn == 0 and ncols % tn == 0 and tn % HEAD_DIM == 0
    assert norm_cols[0] % HEAD_DIM == 0 and norm_cols[1] % HEAD_DIM == 0
    col_tile0 = col0 // tn
    patterns = tuple(tuple(norm_cols[0] <= jj * tn + c * HEAD_DIM < norm_cols[1] for c in range(tn // HEAD_DIM))
                     for jj in range(ncols // tn))
    if gain is None:
        gain = jnp.ones((ncols,), F32)
    if heads_out is None:
        out_shape = jax.ShapeDtypeStruct((r, ncols), F32)
        out_spec = pl.BlockSpec((tm, tn), lambda i, j: (i, j))
    else:
        bsz, t = heads_out
        assert bsz * t == r and t % tm == 0
        nt = t // tm
        hpt = tn // HEAD_DIM
        out_shape = jax.ShapeDtypeStruct((bsz, ncols // HEAD_DIM, t, HEAD_DIM), F32)
        out_spec = pl.BlockSpec((None, hpt, tm, HEAD_DIM), lambda i, j: (i // nt, j, i % nt, 0))
    kern = functools.partial(_proj_kernel, patterns=patterns, heads_out=heads_out is not None)
    if bf16_copy:
        out_shape = (out_shape, jax.ShapeDtypeStruct(out_shape.shape, BF16))
        out_spec = (out_spec, out_spec)
    return pl.pallas_call(
        kern,
        out_shape=out_shape,
        grid=(r // tm, ncols // tn),
        in_specs=[
            pl.BlockSpec((tm, d), lambda i, j: (i, 0)),
            pl.BlockSpec((d, tn), lambda i, j: (0, j + col_tile0)),
            pl.BlockSpec((1, tn), lambda i, j: (0, j)),
        ],
        out_specs=out_spec,
        compiler_params=_cparams(("parallel", "parallel"), 40),
        name="proj",
    )(xn, w, gain.reshape(1, ncols))


def _sb_prompt_kernel(bias_ref, q_ref, kb_ref, vb_ref, o_ref):
    hg = pl.program_id(1)
    qi = pl.program_id(2)
    blk = q_ref.shape[0]
    nh = kb_ref.shape[0]

    qs = [q_ref[:, j * HEAD_DIM:(j + 1) * HEAD_DIM] for j in range(nh)]
    biases = [bias_ref[hg * nh + j] for j in range(nh)]
    row = lax.broadcasted_iota(jnp.int32, (blk, blk), 0)
    col = lax.broadcasted_iota(jnp.int32, (blk, blk), 1)
    tri = jnp.where(row >= col, 1.0, 0.0).astype(BF16)
    causal = col < row

    def block(kb, state, masked):
        start = pl.multiple_of(kb * blk, blk)
        heads = range(nh)
        zs = [_dot_nt(qs[j], kb_ref[j, pl.ds(start, blk), :]) * QK_SCALE + biases[j] for j in heads]
        drops = [_softplus_fast(z) for z in zs]
        if masked:
            drops = [jnp.where(causal, dr, 0.0) for dr in drops]
        splits = [_split_bf16(dr) for dr in drops]
        suffixes = [_dot(hi, tri) + _dot(lo, tri) for hi, lo in splits]
        ws = [jnp.exp(zs[j] - suffixes[j] - state[j][0]) for j in heads]
        if masked:
            ws = [jnp.where(causal, w, 0.0) for w in ws]
        pvs = [_dot(ws[j].astype(BF16), vb_ref[j, pl.ds(start, blk), :]) for j in heads]
        return tuple((state[j][0] + suffixes[j][:, 0:1], state[j][1] + pvs[j]) for j in heads)

    zero = (jnp.zeros((blk, 1), F32), jnp.zeros((blk, HEAD_DIM), F32))
    state = block(qi, (zero,) * nh, True)
    state = lax.fori_loop(0, qi, lambda it, st: block(qi - 1 - it, st, False), state)
    for j in range(nh):
        o_ref[:, j * HEAD_DIM:(j + 1) * HEAD_DIM] = state[j][1].astype(o_ref.dtype)


def _sb_prompt(q, k, v, sb_bias):
    bsz, heads, t, _ = k.shape
    blk = SB_BLK
    nh = SB_HEADS_PER_STEP
    assert t % blk == 0 and heads % nh == 0
    assert q.dtype == BF16 and k.dtype == BF16 and v.dtype == BF16
    nq = t // blk
    kv_spec = pl.BlockSpec((None, nh, t, HEAD_DIM), lambda b, h, i: (b, h, 0, 0))
    return pl.pallas_call(
        _sb_prompt_kernel,
        out_shape=jax.ShapeDtypeStruct((bsz * t, SB_W), BF16),
        grid=(bsz, heads // nh, nq),
        in_specs=[
            pl.BlockSpec(memory_space=pltpu.SMEM),
            pl.BlockSpec((blk, nh * HEAD_DIM), lambda b, h, i: (b * nq + i, h)),
            kv_spec, kv_spec,
        ],
        out_specs=pl.BlockSpec((blk, nh * HEAD_DIM), lambda b, h, i: (b * nq + i, h)),
        compiler_params=_cparams(("parallel", "parallel", "parallel"), 40),
        name="sb_prompt",
    )(sb_bias, q, k, v)


def _sb_sample_kernel(pt_ref, bias_ref, q_ref, *refs, n_pages):
    del pt_ref
    k_refs = refs[:n_pages]
    v_refs = refs[n_pages:2 * n_pages]
    o_ref = refs[2 * n_pages]
    heads, page = k_refs[0].shape[:2]
    rep = 8
    n = n_pages * rep

    row = lax.broadcasted_iota(jnp.int32, (page, page), 0)
    col = lax.broadcasted_iota(jnp.int32, (page, page), 1)
    tri = jnp.where(row >= col, 1.0, 0.0).astype(BF16)
    rn = lax.broadcasted_iota(jnp.int32, (n, n), 0)
    cn = lax.broadcasted_iota(jnp.int32, (n, n), 1)
    later = jnp.where(((rn % rep) == (cn % rep)) & (cn // rep > rn // rep), 1.0, 0.0).astype(BF16)
    q = q_ref[0]
    zs = []
    for h in range(heads):
        q8 = jnp.broadcast_to(q[h:h + 1, :], (rep, HEAD_DIM)).astype(BF16)
        z = jnp.concatenate([_dot_nt(q8, k_refs[p][h].astype(BF16)) for p in range(n_pages)], axis=0)
        zs.append(z * QK_SCALE + bias_ref[h])
    z = jnp.concatenate(zs, axis=0)
    hi, lo = _split_bf16(-_softplus_fast(z))
    suffix = _dot(hi, tri) + _dot(lo, tri)
    shi, slo = _split_bf16(suffix)
    carry = jnp.concatenate(
        [(_dot(later, shi[h * n:(h + 1) * n]) + _dot(later, slo[h * n:(h + 1) * n]))[:, 0:1] for h in range(heads)],
        axis=0)
    w = jnp.exp(z + suffix + carry).astype(BF16)
    outs = []
    for h in range(heads):
        acc = jnp.zeros((rep, HEAD_DIM), F32)
        for p in range(n_pages):
            r0 = h * n + p * rep
            acc = acc + _dot(w[r0:r0 + rep, :], v_refs[p][h].astype(BF16))
        outs.append(acc[0:1, :])
    o_ref[0] = jnp.concatenate(outs, axis=0).astype(o_ref.dtype)


def _sb_sample(q, sb_bias, cache_k, cache_v, page_table):
    bs, n_pages = page_table.shape
    heads, page = cache_k.shape[1:3]
    assert heads == SB_HEADS and cache_k.shape[3] == HEAD_DIM

    def page_spec(p):
        return pl.BlockSpec((None, heads, page, HEAD_DIM), lambda b, pt: (pt[b, p], 0, 0, 0))

    grid_spec = pltpu.PrefetchScalarGridSpec(
        num_scalar_prefetch=1,
        grid=(bs,),
        in_specs=[pl.BlockSpec(memory_space=pltpu.SMEM),
                  pl.BlockSpec((1, heads, HEAD_DIM), lambda b, pt: (b, 0, 0))]
                 + [page_spec(p) for p in range(n_pages)] * 2,
        out_specs=pl.BlockSpec((1, heads, HEAD_DIM), lambda b, pt: (b, 0, 0)),
    )
    out = pl.pallas_call(
        functools.partial(_sb_sample_kernel, n_pages=n_pages),
        out_shape=jax.ShapeDtypeStruct((bs, heads, HEAD_DIM), BF16),
        grid_spec=grid_spec,
        compiler_params=_cparams(("arbitrary",), 40),
        name="sb_sample",
    )(page_table, sb_bias, q.reshape(bs, heads, HEAD_DIM), *([cache_k] * n_pages), *([cache_v] * n_pages))
    return out.reshape(bs, SB_W)


def _lru_gates(xc, wa_ref, ba, wx_ref, bx, nsp_lambda):
    xb = xc.astype(BF16)
    r_parts, i_parts = [], []
    for h in range(LRU_HEADS):
        xh = xb[:, h * HEAD_DIM:(h + 1) * HEAD_DIM]
        r_parts.append(_dot(xh, wa_ref[h].astype(BF16)))
        i_parts.append(_dot(xh, wx_ref[h].astype(BF16)))
    r = _sigmoid(jnp.concatenate(r_parts, axis=1) + ba)
    i = _sigmoid(jnp.concatenate(i_parts, axis=1) + bx)
    log_a = -LRU_C * r * nsp_lambda
    a = jnp.exp(log_a)
    u = jnp.sqrt(1.0 - jnp.exp(2.0 * log_a)) * (i * xc)
    return a, u


def _gelu_tanh(x):
    return 0.5 * x * (1.0 + jnp.tanh(0.7978845608028654 * (x + 0.044715 * (x * x * x))))


def _lru_prompt_kernel(x_ref, gate_ref, prev_ref, h0_ref, cw_ref, cb_ref, wa_ref, ba_ref, wx_ref, bx_ref,
                       lam_ref, o_ref, hlast_ref, xp_ref, h_ref):
    ti = pl.program_id(1)
    tt = x_ref.shape[0]
    pad = 8

    @pl.when(ti == 0)
    def _():
        xp_ref[pad - (CONV_W - 1):pad, :] = prev_ref[0]
        h_ref[...] = h0_ref[0]

    x = x_ref[...]
    xp_ref[pad:pad + tt, :] = x
    cw = cw_ref[...]
    xc = cb_ref[...] + cw[CONV_W - 1:CONV_W, :] * x
    for j in range(CONV_W - 1):
        xc = xc + cw[j:j + 1, :] * xp_ref[pad - (CONV_W - 1) + j:pad - (CONV_W - 1) + j + tt, :]
    xp_ref[pad - (CONV_W - 1):pad, :] = x[tt - (CONV_W - 1):tt, :]

    nsp = _softplus(-lam_ref[...])
    a, u = _lru_gates(xc, wa_ref, ba_ref[...], wx_ref, bx_ref[...], nsp)
    rows = lax.broadcasted_iota(jnp.int32, (tt, LRU_W), 0)
    b = jnp.where(rows == 0, u + a * h_ref[...], u)
    s = 1
    while s < tt:
        keep = rows >= s
        b = jnp.where(keep, a * pltpu.roll(b, s, 0) + b, b)
        if 2 * s < tt:
            a = jnp.where(keep, a * pltpu.roll(a, s, 0), a)
        s *= 2
    h_ref[...] = b[tt - 1:tt, :]
    o_ref[...] = (b * _gelu_tanh(gate_ref[...])).astype(o_ref.dtype)

    @pl.when(ti == pl.num_programs(1) - 1)
    def _():
        hlast_ref[0] = b[tt - 1:tt, :]


def _lru_prompt(proj, conv_prev, h0, conv_w, conv_b, lru_wa, lru_ba, lru_wx, lru_bx, lru_lambda, bsz, t):
    tt = LRU_TT
    assert t % tt == 0
    nt = t // tt
    xl, gl = 0, 1
    full = lambda shape: pl.BlockSpec(shape, lambda b, i: (0,) * len(shape))
    o_lru, h_last = pl.pallas_call(
        _lru_prompt_kernel,
        out_shape=(jax.ShapeDtypeStruct((bsz * t, LRU_W), BF16),
                   jax.ShapeDtypeStruct((bsz, 1, LRU_W), F32)),
        grid=(bsz, nt),
        in_specs=[
            pl.BlockSpec((tt, LRU_W), lambda b, i: (b * nt + i, xl)),
            pl.BlockSpec((tt, LRU_W), lambda b, i: (b * nt + i, gl)),
            pl.BlockSpec((1, CONV_W - 1, LRU_W), lambda b, i: (b, 0, 0)),
            pl.BlockSpec((1, 1, LRU_W), lambda b, i: (b, 0, 0)),
            full((CONV_W, LRU_W)), full((1, LRU_W)),
            full((LRU_HEADS, HEAD_DIM, HEAD_DIM)), full((1, LRU_W)),
            full((LRU_HEADS, HEAD_DIM, HEAD_DIM)), full((1, LRU_W)),
            full((1, LRU_W)),
        ],
        out_specs=(pl.BlockSpec((tt, LRU_W), lambda b, i: (b * nt + i, 0)),
                   pl.BlockSpec((1, 1, LRU_W), lambda b, i: (b, 0, 0))),
        scratch_shapes=[pltpu.VMEM((8 + tt, LRU_W), F32), pltpu.VMEM((1, LRU_W), F32)],
        compiler_params=_cparams(("parallel", "arbitrary"), 32),
        name="lru_prompt",
    )(proj, proj, conv_prev, h0.reshape(bsz, 1, LRU_W), conv_w, conv_b.reshape(1, LRU_W),
      lru_wa, lru_ba.reshape(1, LRU_W), lru_wx, lru_bx.reshape(1, LRU_W), lru_lambda.reshape(1, LRU_W))
    return o_lru, h_last.reshape(bsz, LRU_W)


def _lru_step_kernel(x_ref, gate_ref, p0_ref, p1_ref, p2_ref, h0_ref, cw_ref, cb_ref, wa_ref, ba_ref, wx_ref,
                     bx_ref, lam_ref, o_ref, h_ref):
    cw = cw_ref[...]
    xc = (cb_ref[...] + cw[0:1, :] * p0_ref[...] + cw[1:2, :] * p1_ref[...] + cw[2:3, :] * p2_ref[...]
          + cw[3:4, :] * x_ref[...])
    nsp = _softplus(-lam_ref[...])
    a, u = _lru_gates(xc, wa_ref, ba_ref[...], wx_ref, bx_ref[...], nsp)
    h = u + a * h0_ref[...]
    h_ref[...] = h
    o_ref[...] = (h * _gelu_tanh(gate_ref[...])).astype(o_ref.dtype)


def _lru_step(x_lru, gate_lru, state_conv, h0, conv_w, conv_b, lru_wa, lru_ba, lru_wx, lru_bx, lru_lambda):
    bs = x_lru.shape[0]
    assert CONV_W == 4
    prevs = [state_conv[:, j, :] for j in range(CONV_W - 1)]
    o_lru, h_new = pl.pallas_call(
        _lru_step_kernel,
        out_shape=(jax.ShapeDtypeStruct((bs, LRU_W), BF16), jax.ShapeDtypeStruct((bs, LRU_W), F32)),
        name="lru_step",
    )(x_lru, gate_lru, *prevs, h0, conv_w, conv_b.reshape(1, LRU_W), lru_wa, lru_ba.reshape(1, LRU_W),
      lru_wx, lru_bx.reshape(1, LRU_W), lru_lambda.reshape(1, LRU_W))
    return o_lru, h_new


def _mem_prompt_kernel(q_ref, k_ref, v_ref, o_ref):
    heads = range(k_ref.shape[0])
    cols = [slice(h * HEAD_DIM, (h + 1) * HEAD_DIM) for h in heads]
    scores = [_dot_nt(q_ref[:, cols[h]].astype(BF16), k_ref[h].astype(BF16)) * QK_SCALE for h in heads]
    exps = [jnp.exp(s - jnp.max(s, axis=-1, keepdims=True)) for s in scores]
    probs = [e / jnp.sum(e, axis=-1, keepdims=True) for e in exps]
    for h in heads:
        o_ref[:, cols[h]] = _dot(probs[h].astype(BF16), v_ref[h].astype(BF16)).astype(o_ref.dtype)


def _mem_prompt(proj, qcol0, mem_k, mem_v, t):
    bsz, heads, n_mem, _ = mem_k.shape
    tq = _row_tile(t, ROW_TILE)
    assert heads == MEM_HEADS and qcol0 % MEM_W == 0
    nt = t // tq
    qc = qcol0 // MEM_W
    kv_spec = pl.BlockSpec((None, heads, n_mem, HEAD_DIM), lambda b, i: (b, 0, 0, 0))
    return pl.pallas_call(
        _mem_prompt_kernel,
        out_shape=jax.ShapeDtypeStruct((bsz * t, MEM_W), BF16),
        grid=(bsz, nt),
        in_specs=[pl.BlockSpec((tq, MEM_W), lambda b, i: (b * nt + i, qc)), kv_spec, kv_spec],
        out_specs=pl.BlockSpec((tq, MEM_W), lambda b, i: (b * nt + i, 0)),
        compiler_params=_cparams(("parallel", "parallel"), 32),
        name="mem_prompt",
    )(proj, mem_k, mem_v)


def _mem_sample_kernel(q_ref, k_ref, v_ref, o_ref):
    group = q_ref.shape[0]
    for g in range(group):
        q = q_ref[g]
        s = jnp.sum(k_ref[g] * q[None], axis=-1, keepdims=True) * QK_SCALE
        m = jnp.max(s, axis=0, keepdims=True)
        e = jnp.exp(s - m)
        p = e / jnp.sum(e, axis=0, keepdims=True)
        o_ref[g] = jnp.sum(p * v_ref[g], axis=0).astype(o_ref.dtype)


def _mem_sample(q, cache_k, cache_v):
    bs, n_mem, heads, _ = cache_k.shape
    assert heads == MEM_HEADS and cache_k.shape[3] == HEAD_DIM
    group = 8
    assert bs % group == 0
    kv_spec = pl.BlockSpec((group, n_mem, heads, HEAD_DIM), lambda i: (i, 0, 0, 0))
    out = pl.pallas_call(
        _mem_sample_kernel,
        out_shape=jax.ShapeDtypeStruct((bs, heads, HEAD_DIM), BF16),
        grid=(bs // group,),
        in_specs=[pl.BlockSpec((group, heads, HEAD_DIM), lambda i: (i, 0, 0)), kv_spec, kv_spec],
        out_specs=pl.BlockSpec((group, heads, HEAD_DIM), lambda i: (i, 0, 0)),
        compiler_params=_cparams(("parallel",), 32),
        name="mem_sample",
    )(q.reshape(bs, heads, HEAD_DIM), cache_k, cache_v)
    return out.reshape(bs, MEM_W)


def _merge_kernel(oa_ref, ol_ref, om_ref, pa_ref, pl_ref, pm_ref, ga_ref, gl_ref, gm_ref, o_ref):
    tm = o_ref.shape[0]
    sub = min(tm, MERGE_SUB)
    pa = pa_ref[...].astype(BF16)
    pl_ = pl_ref[...].astype(BF16)
    pm = pm_ref[...].astype(BF16)
    for r0 in range(0, tm, sub):
        rows = slice(r0, r0 + sub)
        ya = _dot(oa_ref[rows, :], pa)
        yl = _dot(ol_ref[rows, :], pl_)
        ym = _dot(om_ref[rows, :], pm)
        merged = (_sigmoid(ga_ref[rows, :]) * ya + _sigmoid(gl_ref[rows, :]) * yl
                  + _sigmoid(gm_ref[rows, :]) * ym)
        o_ref[rows, :] = merged.astype(o_ref.dtype)


def _merge(o_sb, o_lru, o_mem, p_attn, p_lru, p_mem, gates, tm):
    r = o_sb.shape[0]
    d = p_attn.shape[1]
    tn = 512
    assert r % tm == 0 and d % tn == 0
    nj = d // tn
    return pl.pallas_call(
        _merge_kernel,
        out_shape=jax.ShapeDtypeStruct((r, d), BF16),
        grid=(r // tm, nj),
        in_specs=[
            pl.BlockSpec((tm, SB_W), lambda i, j: (i, 0)),
            pl.BlockSpec((tm, LRU_W), lambda i, j: (i, 0)),
            pl.BlockSpec((tm, MEM_W), lambda i, j: (i, 0)),
            pl.BlockSpec((SB_W, tn), lambda i, j: (0, j)),
            pl.BlockSpec((LRU_W, tn), lambda i, j: (0, j)),
            pl.BlockSpec((MEM_W, tn), lambda i, j: (0, j)),
            pl.BlockSpec((tm, tn), lambda i, j: (i, j)),
            pl.BlockSpec((tm, tn), lambda i, j: (i, nj + j)),
            pl.BlockSpec((tm, tn), lambda i, j: (i, 2 * nj + j)),
        ],
        out_specs=pl.BlockSpec((tm, tn), lambda i, j: (i, j)),
        compiler_params=_cparams(("parallel", "parallel"), 44),
        name="merge",
    )(o_sb, o_lru, o_mem, p_attn, p_lru, p_mem, gates, gates, gates)


def _post_kernel(x_ref, m_ref, wo_ref, g_ref, wr_ref, br_ref, cnt0_ref,
                 hres_ref, hnp_ref, idx_ref, gate_ref, rank_ref, cnt_ref):
    i = pl.program_id(0)
    tm, d = x_ref.shape
    sub = min(tm, POST_SUB)
    starts = range(0, tm, sub)

    @pl.when(i == 0)
    def _():
        cnt_ref[...] = cnt0_ref[...]

    hres_all = [x_ref[r0:r0 + sub, :] + _dot(m_ref[r0:r0 + sub, :], wo_ref[...]) for r0 in starts]
    eid = lax.broadcasted_iota(jnp.int32, (wr_ref.shape[0], sub), 0)
    ne = eid.shape[0]
    r_ = lax.broadcasted_iota(jnp.int32, (sub, sub), 0)
    c_ = lax.broadcasted_iota(jnp.int32, (sub, sub), 1)
    before = jnp.where(r_ < c_, 1.0, 0.0).astype(BF16)
    cnt = cnt_ref[...]
    for r0, hres in zip(starts, hres_all):
        hres_ref[r0:r0 + sub, :] = hres
        ms = jnp.mean(hres * hres, axis=-1, keepdims=True)
        hn = hres * lax.rsqrt(ms + EPS) * g_ref[...]
        hb = hn.astype(BF16)
        wa = lax.bitcast_convert_type(hb[:, :d // 2].astype(F32), jnp.uint32)
        wb = lax.bitcast_convert_type(hb[:, d // 2:].astype(F32), jnp.uint32)
        hnp_ref[r0:r0 + sub, :] = wa | (wb >> 16)

        work = _dot_nt(wr_ref[...], hb) + br_ref[...]
        vals, idxs, onehots = [], [], []
        for _ in range(TOP_K):
            mx = jnp.max(work, axis=0, keepdims=True)
            sel = jnp.min(jnp.where(work == mx, eid, ne), axis=0, keepdims=True)
            oh = eid == sel
            vals.append(mx)
            idxs.append(sel)
            onehots.append(oh)
            work = jnp.where(oh, -jnp.inf, work)
        es = [jnp.exp(v - vals[0]) for v in vals]
        den = functools.reduce(jnp.add, es)
        gate_ref[:, r0:r0 + sub] = jnp.concatenate([e / den for e in es], axis=0)
        idx_ref[:, r0:r0 + sub] = jnp.concatenate(idxs, axis=0)

        chosen_f = jnp.where(functools.reduce(jnp.logical_or, onehots), 1.0, 0.0)
        prior = _dot(chosen_f.astype(BF16), before) + cnt
        rank_ref[:, r0:r0 + sub] = jnp.concatenate(
            [jnp.sum(jnp.where(oh, prior, 0.0), axis=0, keepdims=True) for oh in onehots], axis=0).astype(jnp.int32)
        cnt = cnt + jnp.sum(chosen_f, axis=1, keepdims=True)
    cnt_ref[...] = cnt


def _post(x, merged, wo_b, g, wr_t, b_router, cnt0, tm):
    r, d = x.shape
    ne = wr_t.shape[0]
    assert r % tm == 0
    full = lambda shape: pl.BlockSpec(shape, lambda i: (0,) * len(shape))
    return pl.pallas_call(
        _post_kernel,
        out_shape=(jax.ShapeDtypeStruct((r, d), F32),
                   jax.ShapeDtypeStruct((r, d // 2), jnp.uint32),
                   jax.ShapeDtypeStruct((TOP_K, r), jnp.int32),
                   jax.ShapeDtypeStruct((TOP_K, r), F32),
                   jax.ShapeDtypeStruct((TOP_K, r), jnp.int32),
                   jax.ShapeDtypeStruct((ne, 1), F32)),
        grid=(r // tm,),
        in_specs=[pl.BlockSpec((tm, d), lambda i: (i, 0)),
                  pl.BlockSpec((tm, d), lambda i: (i, 0)),
                  full((d, d)), full((1, d)), full((ne, d)), full((ne, 1)), full((ne, 1))],
        out_specs=(pl.BlockSpec((tm, d), lambda i: (i, 0)),
                   pl.BlockSpec((tm, d // 2), lambda i: (i, 0)),
                   pl.BlockSpec((TOP_K, tm), lambda i: (0, i)),
                   pl.BlockSpec((TOP_K, tm), lambda i: (0, i)),
                   pl.BlockSpec((TOP_K, tm), lambda i: (0, i)),
                   full((ne, 1))),
        compiler_params=_cparams(("arbitrary",), 56),
        name="post",
    )(x, merged, wo_b, g.reshape(1, d), wr_t, b_router.reshape(ne, 1), cnt0)


def _dispatch_kernel(tail_ref, dest_ref, hn_a_ref, hn_b_ref, xs_ref, zero_ref, sem, zsem, *, tiles_a):
    i = pl.program_id(0)
    tm = dest_ref.shape[2] // TOP_K

    @pl.when(i == 0)
    def _():
        zero_ref[...] = jnp.zeros_like(zero_ref)

        def tail_copy(e):
            start = pl.multiple_of(tail_ref[e], MOE_SUB)
            return pltpu.make_async_copy(zero_ref, xs_ref.at[pl.ds(start, MOE_SUB)], zsem)

        def zstart(e, c):
            pl.when(tail_ref[e] >= 0)(lambda: tail_copy(e).start())
            return c

        def zwait(e, c):
            pl.when(tail_ref[e] >= 0)(lambda: tail_copy(e).wait())
            return c

        lax.fori_loop(0, tail_ref.shape[0], zstart, 0)
        lax.fori_loop(0, tail_ref.shape[0], zwait, 0)

    def row_copy(src_ref, row, t, k):
        return pltpu.make_async_copy(src_ref.at[pl.ds(row, 1)],
                                     xs_ref.at[pl.ds(dest_ref[0, 0, k * tm + t], 1)], sem)

    def start_tile(src_ref, first_row):
        def body(t, c):
            for k in range(TOP_K):
                row_copy(src_ref, first_row + t, t, k).start()
            return c
        lax.fori_loop(0, tm, body, 0)

    def wait_tile():
        def body(t, c):
            for k in range(TOP_K):
                row_copy(hn_a_ref, 0, t, k).wait()
            return c
        lax.fori_loop(0, tm, body, 0)

    pl.when(i < tiles_a)(lambda: start_tile(hn_a_ref, i * tm))
    pl.when(i >= tiles_a)(lambda: start_tile(hn_b_ref, (i - tiles_a) * tm))
    pl.when(i > 0)(wait_tile)
    pl.when(i == pl.num_programs(0) - 1)(wait_tile)


def _dispatch(hnp_a, hnp_b, dest, tail, n_rows, tm):
    (ra, w), rb_ = hnp_a.shape, hnp_b.shape[0]
    assert ra % tm == 0 and rb_ % tm == 0 and hnp_b.shape[1] == w and dest.shape[1] == ra + rb_
    nt = (ra + rb_) // tm
    dest_t = dest.reshape(TOP_K, nt, tm).transpose(1, 0, 2).reshape(nt, 1, TOP_K * tm)
    grid_spec = pltpu.PrefetchScalarGridSpec(
        num_scalar_prefetch=1,
        grid=(nt,),
        in_specs=[pl.BlockSpec((1, 1, TOP_K * tm), lambda i, tl: (i, 0, 0), memory_space=pltpu.SMEM),
                  pl.BlockSpec(memory_space=pl.ANY), pl.BlockSpec(memory_space=pl.ANY)],
        out_specs=pl.BlockSpec(memory_space=pl.ANY),
        scratch_shapes=[pltpu.VMEM((MOE_SUB, w), hnp_a.dtype), pltpu.SemaphoreType.DMA, pltpu.SemaphoreType.DMA],
    )
    return pl.pallas_call(
        functools.partial(_dispatch_kernel, tiles_a=ra // tm),
        out_shape=jax.ShapeDtypeStruct((n_rows, w), hnp_a.dtype),
        grid_spec=grid_spec,
        compiler_params=_cparams(("arbitrary",), 16),
        name="dispatch",
    )(tail, dest_t, hnp_a, hnp_b)


def _moe_kernel(ce_ref, cb_ref, ns_ref, xs_ref, wg_ref, wu_ref, bg_ref, bu_ref, wd_ref, bd_ref, o_ref,
                xa_ref, xb_ref):
    del ce_ref, cb_ref
    c = pl.program_id(0)
    s = pl.program_id(1)
    nsub = ns_ref[c]
    half = xa_ref.shape[1]
    d = o_ref.shape[1]

    def for_row_blocks(fn):
        done = 0
        for size in MOE_BLOCK_SUBS:
            count = (nsub - done) // size
            rows = size * MOE_SUB

            def body(i, carry, done=done, rows=rows):
                fn(pl.ds(pl.multiple_of(done * MOE_SUB + i * rows, MOE_SUB), rows))
                return carry

            lax.fori_loop(0, count, body, 0)
            done = done + count * size

    @pl.when(s == 0)
    def _():
        def unpack(rows):
            word = xs_ref[rows, :]
            xa_ref[rows, :] = lax.bitcast_convert_type(word & jnp.uint32(0xFFFF0000), F32).astype(BF16)
            xb_ref[rows, :] = lax.bitcast_convert_type(word << 16, F32).astype(BF16)
            o_ref[rows, :] = jnp.broadcast_to(bd_ref[...], (rows.size, d))
        for_row_blocks(unpack)

    bg = bg_ref[...]
    bu = bu_ref[...]

    def step(rows):
        xa = xa_ref[rows, :]
        xb = xb_ref[rows, :]
        gate = (_dot(xa, wg_ref[:half, :].astype(BF16)) + _dot(xb, wg_ref[half:, :].astype(BF16)) + bg)
        upv = (_dot(xa, wu_ref[:half, :].astype(BF16)) + _dot(xb, wu_ref[half:, :].astype(BF16)) + bu)
        gate = jnp.minimum(gate, SWIGLU_LIMIT)
        upv = jnp.clip(upv, -SWIGLU_LIMIT, SWIGLU_LIMIT)
        act = (gate * _sigmoid(SWIGLU_ALPHA * gate) * (upv + 1.0)).astype(BF16)
        for n0 in range(0, d, MOE_TN):
            o_ref[rows, n0:n0 + MOE_TN] += _dot(act, wd_ref[:, n0:n0 + MOE_TN].astype(BF16))
    for_row_blocks(step)


def _moe(xs, n_used, chunk_expert, chunk_block, chunk_nsub, w_up, b_up, w_down, b_down):
    ne, d, dff2 = w_up.shape
    dff = dff2 // 2
    rb = MOE_SUB * MOE_CHUNK_SUBS
    n_chunks = xs.shape[0] // rb
    tf = MOE_TF
    nf = dff // tf
    assert dff % tf == 0 and d % MOE_TN == 0

    grid_spec = pltpu.PrefetchScalarGridSpec(
        num_scalar_prefetch=3,
        grid=(n_used, nf),
        in_specs=[
            pl.BlockSpec((rb, d // 2), lambda c, s, ce, cb, ns: (cb[c], 0)),
            pl.BlockSpec((None, d, tf), lambda c, s, ce, cb, ns: (ce[c], 0, s)),
            pl.BlockSpec((None, d, tf), lambda c, s, ce, cb, ns: (ce[c], 0, nf + s)),
            pl.BlockSpec((None, 1, tf), lambda c, s, ce, cb, ns: (ce[c], 0, s)),
            pl.BlockSpec((None, 1, tf), lambda c, s, ce, cb, ns: (ce[c], 0, nf + s)),
            pl.BlockSpec((None, tf, d), lambda c, s, ce, cb, ns: (ce[c], s, 0)),
            pl.BlockSpec((None, 1, d), lambda c, s, ce, cb, ns: (ce[c], 0, 0)),
        ],
        out_specs=pl.BlockSpec((rb, d), lambda c, s, ce, cb, ns: (cb[c], 0)),
        scratch_shapes=[pltpu.VMEM((rb, d // 2), BF16), pltpu.VMEM((rb, d // 2), BF16)],
    )
    return pl.pallas_call(
        _moe_kernel,
        out_shape=jax.ShapeDtypeStruct((n_chunks * rb, d), F32),
        grid_spec=grid_spec,
        compiler_params=_cparams(("arbitrary", "arbitrary"), 56),
        name="moe",
    )(chunk_expert, chunk_block, chunk_nsub, xs, w_up, w_up, b_up.reshape(ne, 1, dff2), b_up.reshape(ne, 1, dff2),
      w_down, b_down.reshape(ne, 1, d))


def _combine_kernel(dest_ref, dest_next_ref, hres_ref, gate_ref, yb_ref, o_ref, buf_ref, sem):
    i = pl.program_id(0)
    tm = hres_ref.shape[0]
    slot = i % 2

    def row_copy(dref, s, t, k):
        return pltpu.make_async_copy(yb_ref.at[pl.ds(dref[0, 0, k * tm + t], 1)],
                                     buf_ref.at[s, k, pl.ds(t, 1)], sem.at[s])

    def issue(dref, s):
        def body(t, c):
            for k in range(TOP_K):
                row_copy(dref, s, t, k).start()
            return c
        lax.fori_loop(0, tm, body, 0)

    def drain(dref, s):
        def body(t, c):
            for k in range(TOP_K):
                row_copy(dref, s, t, k).wait()
            return c
        lax.fori_loop(0, tm, body, 0)

    pl.when(i == 0)(lambda: issue(dest_ref, slot))
    pl.when(i + 1 < pl.num_programs(0))(lambda: issue(dest_next_ref, 1 - slot))
    drain(dest_ref, slot)
    g = gate_ref[...]
    y = hres_ref[...]
    for k in range(TOP_K):
        y = y + g[:, k:k + 1] * buf_ref[slot, k]
    o_ref[...] = y


def _combine(hres, gates, dest, yb, tm):
    r, d = hres.shape
    assert r % tm == 0
    nt = r // tm
    dest_t = dest.reshape(TOP_K, nt, tm).transpose(1, 0, 2).reshape(nt, 1, TOP_K * tm)
    return pl.pallas_call(
        _combine_kernel,
        out_shape=jax.ShapeDtypeStruct((r, d), F32),
        grid=(nt,),
        in_specs=[pl.BlockSpec((1, 1, TOP_K * tm), lambda i: (i, 0, 0), memory_space=pltpu.SMEM),
                  pl.BlockSpec((1, 1, TOP_K * tm), lambda i: (jnp.minimum(i + 1, nt - 1), 0, 0),
                               memory_space=pltpu.SMEM),
                  pl.BlockSpec((tm, d), lambda i: (i, 0)),
                  pl.BlockSpec((tm, TOP_K), lambda i: (i, 0)),
                  pl.BlockSpec(memory_space=pl.ANY)],
        out_specs=pl.BlockSpec((tm, d), lambda i: (i, 0)),
        scratch_shapes=[pltpu.VMEM((2, TOP_K, tm, d), F32), pltpu.SemaphoreType.DMA((2,))],
        compiler_params=_cparams(("arbitrary",), 40),
        name="combine",
    )(dest_t, dest_t, hres, gates.T, yb)


def _row_tile(r, cap):
    tm = min(r, cap)
    assert r % tm == 0
    return tm


def kernel(x_prompt, x_sample, mem_prompt, cache_sb_k, cache_sb_v, page_table, cache_mem_k, cache_mem_v, state_conv, state_lru, norm_mix_g, norm_mem_g, w_in, q_sb_g, k_sb_g, sb_bias, q_mem_g, k_mem_g, w_mem_kv, conv_w, conv_b, lru_wa, lru_ba, lru_wx, lru_bx, lru_lambda, p_attn, p_lru, p_mem, w_o, norm_ffn_g, w_router, b_router, w_up, b_up, w_down, b_down):
    bp, t, d = x_prompt.shape
    bs, ts, _ = x_sample.shape
    assert ts == 1, "sample group decodes one token per sequence"
    n_mem = mem_prompt.shape[1]
    q_gain = jnp.tile(q_sb_g, SB_HEADS)
    k_gain = jnp.tile(k_sb_g, SB_HEADS)
    qm_gain = jnp.tile(q_mem_g, MEM_HEADS)
    heads_major = lambda a: jnp.transpose(a, (0, 2, 1, 3))

    memn = _rmsnorm(mem_prompt.reshape(bp * n_mem, d), norm_mem_g, _row_tile(bp * n_mem, 256))
    mem_tm = _row_tile(n_mem, 256)
    mem_k_hm = _proj(memn, w_mem_kv, jnp.tile(k_mem_g, MEM_HEADS), col0=0, ncols=MEM_W, norm_cols=(0, MEM_W),
                     tm=mem_tm, tn=MEM_W, heads_out=(bp, n_mem))
    mem_v_hm = _proj(memn, w_mem_kv, None, col0=MEM_W, ncols=MEM_W, tm=mem_tm, tn=MEM_W, heads_out=(bp, n_mem))

    rest_w = COL_GATE - COL_XL
    qm_cols = (COL_QM - COL_XL, rest_w)
    gap = -COL_XL % WIDE_TN
    w_b = _cast_bf16(w_in, gap_at=COL_XL, gap=gap)
    q_at, k_at, v_at, rest_at, gates_at = COL_Q, COL_K, COL_V, COL_XL + gap, COL_GATE + gap
    rest_gain = jnp.concatenate([jnp.ones((2 * LRU_W,), F32), qm_gain])

    xp2 = x_prompt.reshape(bp * t, d)
    tm_p = _row_tile(t, ROW_TILE)
    xn_p = _rmsnorm(xp2, norm_mix_g, _row_tile(bp * t, 512))
    _, q_pb = _proj(xn_p, w_b, q_gain, col0=q_at, ncols=SB_W, norm_cols=(0, SB_W), tm=tm_p, tn=SB_W,
                    bf16_copy=True)
    k_p, k_pb = _proj(xn_p, w_b, k_gain, col0=k_at, ncols=SB_W, norm_cols=(0, SB_W), tm=tm_p, tn=SB_W,
                      heads_out=(bp, t), bf16_copy=True)
    v_p, v_pb = _proj(xn_p, w_b, None, col0=v_at, ncols=SB_W, tm=tm_p, tn=SB_W, heads_out=(bp, t), bf16_copy=True)
    rest_p = _proj(xn_p, w_b, rest_gain, col0=rest_at, ncols=rest_w, norm_cols=qm_cols, tm=tm_p, tn=WIDE_TN)
    gates_p = _proj(xn_p, w_b, None, col0=gates_at, ncols=N_BRANCH * d, tm=tm_p, tn=WIDE_TN)
    o_sb_p = _sb_prompt(q_pb, k_pb, v_pb, sb_bias)
    conv0 = jnp.zeros((bp, CONV_W - 1, LRU_W), x_prompt.dtype)
    lru0 = jnp.zeros((bp, LRU_W), state_lru.dtype)
    o_lru_p, lru_prompt = _lru_prompt(rest_p, conv0, lru0, conv_w, conv_b, lru_wa, lru_ba, lru_wx, lru_bx,
                                      lru_lambda, bp, t)
    o_mem_p = _mem_prompt(rest_p, COL_QM - COL_XL, mem_k_hm, mem_v_hm, t)
    xl_tail = rest_p.reshape(bp, t, -1)[:, t - (CONV_W - 1):, :LRU_W]
    conv_prompt = jnp.concatenate([conv0, xl_tail], axis=1)[:, -(CONV_W - 1):]

    xs2 = x_sample.reshape(bs, d)
    xn_s = _rmsnorm(xs2, norm_mix_g, bs)
    qkv_gain = jnp.concatenate([q_gain, k_gain, jnp.ones((SB_W,), F32)])
    qkv_s = _proj(xn_s, w_b, qkv_gain, col0=q_at, ncols=3 * SB_W, norm_cols=(0, 2 * SB_W), tm=bs, tn=SB_W)
    q_s, k_s, v_s = qkv_s[:, :SB_W], qkv_s[:, SB_W:2 * SB_W], qkv_s[:, 2 * SB_W:]
    rest_s = _proj(xn_s, w_b, rest_gain, col0=rest_at, ncols=rest_w, norm_cols=qm_cols, tm=bs, tn=WIDE_TN)
    gates_s = _proj(xn_s, w_b, None, col0=gates_at, ncols=N_BRANCH * d, tm=bs, tn=WIDE_TN)
    o_sb_s = _sb_sample(q_s, sb_bias, heads_major(cache_sb_k), heads_major(cache_sb_v), page_table)
    xl_s = rest_s[:, :LRU_W]
    o_lru_s, lru_sample = _lru_step(xl_s, rest_s[:, LRU_W:2 * LRU_W], state_conv, state_lru, conv_w, conv_b,
                                    lru_wa, lru_ba, lru_wx, lru_bx, lru_lambda)
    conv_sample = jnp.concatenate([state_conv, xl_s[:, None, :]], axis=1)[:, -(CONV_W - 1):]
    o_mem_s = _mem_sample(rest_s[:, 2 * LRU_W:], cache_mem_k, cache_mem_v)

    wo_b = _cast_bf16(w_o)
    wr_t = w_router.T.astype(BF16)
    merged_p = _merge(o_sb_p, o_lru_p, o_mem_p, p_attn, p_lru, p_mem, gates_p, _row_tile(bp * t, ROW_TILE))
    merged_s = _merge(o_sb_s, o_lru_s, o_mem_s, p_attn, p_lru, p_mem, gates_s, _row_tile(bs, ROW_TILE))
    cnt0 = jnp.zeros((N_EXPERTS, 1), F32)
    hres_p, hnp_p, idx_p, gate_p, rank_p, cnt_p = _post(xp2, merged_p, wo_b, norm_ffn_g, wr_t, b_router,
                                                        cnt0, _row_tile(bp * t, POST_TM))
    hres_s, hnp_s, idx_s, gate_s, rank_s, cnt_all = _post(xs2, merged_s, wo_b, norm_ffn_g, wr_t, b_router,
                                                          cnt_p, _row_tile(bs, POST_TM))

    rb = MOE_SUB * MOE_CHUNK_SUBS
    n_assign = (bp * t + bs) * TOP_K
    n_chunks = -(-n_assign // rb) + N_EXPERTS
    counts = cnt_all[:, 0].astype(jnp.int32)
    chunks_e = (counts + rb - 1) // rb
    chunk_end = jnp.cumsum(chunks_e)
    chunk_start = chunk_end - chunks_e
    used = chunk_end[-1]
    cidx = jnp.arange(n_chunks, dtype=jnp.int32)
    last = jnp.minimum(cidx, used - 1)
    chunk_expert = jnp.clip(jnp.searchsorted(chunk_end, last, side='right'), 0, N_EXPERTS - 1).astype(jnp.int32)
    rows_left = counts[chunk_expert] - (last - chunk_start[chunk_expert]) * rb
    chunk_nsub = jnp.where(cidx < used, (jnp.clip(rows_left, 0, rb) + MOE_SUB - 1) // MOE_SUB, 0).astype(jnp.int32)
    chunk_block = last.astype(jnp.int32)
    row_start = chunk_start * rb
    experts = jnp.arange(N_EXPERTS, dtype=jnp.int32)

    def slots(idx, rank):
        return jnp.sum(jnp.where(idx[..., None] == experts, row_start, 0), axis=-1) + rank

    dest_p = slots(idx_p, rank_p)
    dest_s = slots(idx_s, rank_s)
    tail = jnp.where(counts % MOE_SUB != 0, row_start + counts // MOE_SUB * MOE_SUB, -1).astype(jnp.int32)

    dest = jnp.concatenate([dest_p, dest_s], axis=1)
    xs = _dispatch(hnp_p, hnp_s, dest, tail, n_chunks * rb, math.gcd(bp * t, bs, DISPATCH_TM))
    yb = _moe(xs, used.astype(jnp.int32), chunk_expert, chunk_block, chunk_nsub, w_up, b_up, w_down, b_down)
    y_prompt = _combine(hres_p, gate_p, dest_p, yb, _row_tile(bp * t, COMBINE_TM)).reshape(bp, t, d)
    y_sample = _combine(hres_s, gate_s, dest_s, yb, _row_tile(bs, COMBINE_TM)).reshape(bs, 1, d)

    sb_k_sample = k_s.reshape(bs, 1, SB_HEADS, HEAD_DIM)
    sb_v_sample = v_s.reshape(bs, 1, SB_HEADS, HEAD_DIM)
    return (y_prompt, y_sample, heads_major(k_p), heads_major(v_p), heads_major(mem_k_hm), heads_major(mem_v_hm),
            conv_prompt, lru_prompt, sb_k_sample, sb_v_sample, conv_sample, lru_sample)
```

```python
import functools
import math

import jax
import jax.numpy as jnp
from jax import lax
from jax.experimental import pallas as pl
from jax.experimental.pallas import tpu as pltpu

F32 = jnp.float32
BF16 = jnp.bfloat16

HEAD_DIM = 128
SB_HEADS = 6
SB_W = SB_HEADS * HEAD_DIM
LRU_HEADS = 6
LRU_W = LRU_HEADS * HEAD_DIM
LRU_C = 8.0
CONV_W = 4
MEM_HEADS = 4
MEM_W = MEM_HEADS * HEAD_DIM
N_BRANCH = 3
N_EXPERTS = 32
TOP_K = 4
SWIGLU_LIMIT = 7.0
SWIGLU_ALPHA = 1.702
EPS = 1e-6
QK_SCALE = HEAD_DIM ** -0.5

COL_Q, COL_K, COL_V = 0, SB_W, 2 * SB_W
COL_XL, COL_GL = 3 * SB_W, 3 * SB_W + LRU_W
COL_QM = 3 * SB_W + 2 * LRU_W
COL_GATE = COL_QM + MEM_W

MIB = 1024 * 1024
PROJ_TN = 256
ROW_TILE = 1024
SB_BLK = 256
SB_HEADS_PER_STEP = 6
LRU_TT = 256
MOE_SUB = 128
MOE_BLOCK_SUBS = (9, 8, 1)
MOE_CHUNK_SUBS = 9
MOE_TF = 256
MOE_TN = 512
PROJ_SUB = 256
MERGE_SUB = 256
POST_TM = 512
POST_SUB = 256
COMBINE_TM = 256
DISPATCH_TM = 128
WIDE_TN = 1024


def _cparams(sem, vmem_mib):
    return pltpu.CompilerParams(dimension_semantics=sem, vmem_limit_bytes=vmem_mib * MIB)


def _softplus(z):
    return jnp.maximum(z, 0.0) + jnp.log1p(jnp.exp(-jnp.abs(z)))


def _softplus_fast(z):
    return jnp.maximum(z, 0.0) + jnp.log(1.0 + jnp.exp(-jnp.abs(z)))


def _sigmoid(z):
    return 1.0 / (1.0 + jnp.exp(-z))


def _split_bf16(x):
    hi = x.astype(BF16)
    lo = (x - hi.astype(F32)).astype(BF16)
    return hi, lo


def _dot(a, b):
    return jnp.dot(a, b, preferred_element_type=F32)


def _dot_nt(a, b):
    return lax.dot_general(a, b, (((1,), (1,)), ((), ())), preferred_element_type=F32)


def _rmsnorm_kernel(x_ref, g_ref, o_ref):
    x = x_ref[...]
    ms = jnp.mean(x * x, axis=-1, keepdims=True)
    o_ref[...] = (x * lax.rsqrt(ms + EPS) * g_ref[...]).astype(o_ref.dtype)


def _rmsnorm(x, g, tm):
    r, d = x.shape
    assert r % tm == 0
    return pl.pallas_call(
        _rmsnorm_kernel,
        out_shape=jax.ShapeDtypeStruct((r, d), BF16),
        grid=(r // tm,),
        in_specs=[pl.BlockSpec((tm, d), lambda i: (i, 0)), pl.BlockSpec((1, d), lambda i: (0, 0))],
        out_specs=pl.BlockSpec((tm, d), lambda i: (i, 0)),
        compiler_params=_cparams(("parallel",), 32),
        name="rmsnorm",
    )(x, g.reshape(1, d))


def _cast_kernel(x_ref, o_ref, *, gap_lo, gap_hi):
    j = pl.program_id(0)
    is_gap = (j >= gap_lo) & (j < gap_hi)

    @pl.when(is_gap)
    def _():
        o_ref[...] = jnp.zeros_like(o_ref)

    @pl.when(jnp.logical_not(is_gap))
    def _():
        o_ref[...] = x_ref[...].astype(o_ref.dtype)


def _cast_bf16(w, gap_at=0, gap=0):
    r, c = w.shape
    tc = PROJ_TN
    assert c % tc == 0 and gap_at % tc == 0 and gap % tc == 0
    g0, g1 = gap_at // tc, (gap_at + gap) // tc

    def src_tile(j):
        return jnp.where(j < g1, jnp.minimum(j, max(g0 - 1, 0)), j - (g1 - g0))

    return pl.pallas_call(
        functools.partial(_cast_kernel, gap_lo=g0, gap_hi=g1),
        out_shape=jax.ShapeDtypeStruct((r, c + gap), BF16),
        grid=((c + gap) // tc,),
        in_specs=[pl.BlockSpec((r, tc), lambda j: (0, src_tile(j)))],
        out_specs=pl.BlockSpec((r, tc), lambda j: (0, j)),
        compiler_params=_cparams(("parallel",), 32),
        name="cast_bf16",
    )(w)


def _proj_kernel(xn_ref, w_ref, gain_ref, *o_refs, patterns, heads_out):
    j = pl.program_id(1)
    tm = xn_ref.shape[0]
    nh = w_ref.shape[1] // HEAD_DIM
    w = w_ref[...].astype(BF16)
    sub = min(tm, PROJ_SUB) if (heads_out or any(any(p) for p in patterns)) else tm
    ys = [(r0, _dot(xn_ref[r0:r0 + sub, :], w)) for r0 in range(0, tm, sub)]

    def store(r0, c, val):
        for o_ref in o_refs:
            if heads_out:
                o_ref[c, r0:r0 + sub, :] = val.astype(o_ref.dtype)
            else:
                o_ref[r0:r0 + sub, c * HEAD_DIM:(c + 1) * HEAD_DIM] = val.astype(o_ref.dtype)

    def emit(pattern):
        gain = gain_ref[...] if any(pattern) else None
        for r0, y in ys:
            for c in range(nh):
                yc = y[:, c * HEAD_DIM:(c + 1) * HEAD_DIM]
                if pattern[c]:
                    ms = jnp.mean(yc * yc, axis=-1, keepdims=True)
                    yc = yc * lax.rsqrt(ms + EPS) * gain[:, c * HEAD_DIM:(c + 1) * HEAD_DIM]
                store(r0, c, yc)

    distinct = sorted(set(patterns))
    if len(distinct) == 1:
        emit(distinct[0])
    else:
        for pattern in distinct:
            tiles = [jj for jj, p in enumerate(patterns) if p == pattern]
            pred = functools.reduce(jnp.logical_or, [j == jj for jj in tiles])
            pl.when(pred)(functools.partial(emit, pattern))


def _proj_qkv_kernel(xn_ref, w_ref, gain_ref, q_ref, k_ref, kb_ref, v_ref, vb_ref):
    j = pl.program_id(1)
    tm = xn_ref.shape[0]
    sub = min(tm, PROJ_SUB)

    def row_groups():
        w = w_ref[...].astype(BF16)
        return [(r0, _dot(xn_ref[r0:r0 + sub, :], w)) for r0 in range(0, tm, sub)]

    def head_cols(y, normed):
        gain = gain_ref[...]
        for c in range(SB_HEADS):
            cols = slice(c * HEAD_DIM, (c + 1) * HEAD_DIM)
            yc = y[:, cols]
            if normed:
                ms = jnp.mean(yc * yc, axis=-1, keepdims=True)
                yc = yc * lax.rsqrt(ms + EPS) * gain[:, cols]
            yield c, cols, yc

    @pl.when(j == 0)
    def _():
        for r0, y in row_groups():
            for _, cols, yc in head_cols(y, True):
                q_ref[r0:r0 + sub, cols] = yc.astype(q_ref.dtype)

    def heads_store(f32_ref, bf16_ref, normed):
        for r0, y in row_groups():
            for c, _, yc in head_cols(y, normed):
                f32_ref[c, r0:r0 + sub, :] = yc
                bf16_ref[c, r0:r0 + sub, :] = yc.astype(bf16_ref.dtype)

    pl.when(j == 1)(lambda: heads_store(k_ref, kb_ref, True))
    pl.when(j == 2)(lambda: heads_store(v_ref, vb_ref, False))


def _proj_qkv(xn, w, q_gain, k_gain, bsz, t, tm):
    r, d = xn.shape
    assert bsz * t == r and t % tm == 0
    nt = t // tm
    gain = jnp.concatenate([q_gain, k_gain, jnp.ones((SB_W,), F32)]).reshape(1, 3 * SB_W)
    hm_shape = (bsz, SB_HEADS, t, HEAD_DIM)
    hm_spec = pl.BlockSpec((None, SB_HEADS, tm, HEAD_DIM), lambda i, j: (i // nt, 0, i % nt, 0))
    return pl.pallas_call(
        _proj_qkv_kernel,
        out_shape=(jax.ShapeDtypeStruct((r, SB_W), BF16),
                   jax.ShapeDtypeStruct(hm_shape, F32), jax.ShapeDtypeStruct(hm_shape, BF16),
                   jax.ShapeDtypeStruct(hm_shape, F32), jax.ShapeDtypeStruct(hm_shape, BF16)),
        grid=(r // tm, 3),
        in_specs=[pl.BlockSpec((tm, d), lambda i, j: (i, 0)),
                  pl.BlockSpec((d, SB_W), lambda i, j: (0, j)),
                  pl.BlockSpec((1, SB_W), lambda i, j: (0, j))],
        out_specs=(pl.BlockSpec((tm, SB_W), lambda i, j: (i, 0)), hm_spec, hm_spec, hm_spec, hm_spec),
        compiler_params=_cparams(("parallel", "arbitrary"), 40),
        name="proj_qkv",
    )(xn, w, gain)


def _proj(xn, w, gain, *, col0=0, ncols, norm_cols=(0, 0), tm, tn, heads_out=None, bf16_copy=False):
    r, d = xn.shape
    assert r % tm == 0 and col0 % tn == 0 and ncols % tn == 0 and tn % HEAD_DIM == 0
    assert norm_cols[0] % HEAD_DIM == 0 and norm_cols[1] % HEAD_DIM == 0
    col_tile0 = col0 // tn
    patterns = tuple(tuple(norm_cols[0] <= jj * tn + c * HEAD_DIM < norm_cols[1] for c in range(tn // HEAD_DIM))
                     for jj in range(ncols // tn))
    if gain is None:
        gain = jnp.ones((ncols,), F32)
    if heads_out is None:
        out_shape = jax.ShapeDtypeStruct((r, ncols), F32)
        out_spec = pl.BlockSpec((tm, tn), lambda i, j: (i, j))
    else:
        bsz, t = heads_out
        assert bsz * t == r and t % tm == 0
        nt = t // tm
        hpt = tn // HEAD_DIM
        out_shape = jax.ShapeDtypeStruct((bsz, ncols // HEAD_DIM, t, HEAD_DIM), F32)
        out_spec = pl.BlockSpec((None, hpt, tm, HEAD_DIM), lambda i, j: (i // nt, j, i % nt, 0))
    kern = functools.partial(_proj_kernel, patterns=patterns, heads_out=heads_out is not None)
    if bf16_copy:
        out_shape = (out_shape, jax.ShapeDtypeStruct(out_shape.shape, BF16))
        out_spec = (out_spec, out_spec)
    return pl.pallas_call(
        kern,
        out_shape=out_shape,
        grid=(r // tm, ncols // tn),
        in_specs=[
            pl.BlockSpec((tm, d), lambda i, j: (i, 0)),
            pl.BlockSpec((d, tn), lambda i, j: (0, j + col_tile0)),
            pl.BlockSpec((1, tn), lambda i, j: (0, j)),
        ],
        out_specs=out_spec,
        compiler_params=_cparams(("parallel", "parallel"), 40),
        name="proj",
    )(xn, w, gain.reshape(1, ncols))


def _sb_prompt_kernel(bias_ref, q_ref, kb_ref, vb_ref, o_ref):
    hg = pl.program_id(1)
    qi = pl.program_id(2)
    blk = q_ref.shape[0]
    nh = kb_ref.shape[0]

    qs = [q_ref[:, j * HEAD_DIM:(j + 1) * HEAD_DIM] for j in range(nh)]
    biases = [bias_ref[hg * nh + j] for j in range(nh)]
    row = lax.broadcasted_iota(jnp.int32, (blk, blk), 0)
    col = lax.broadcasted_iota(jnp.int32, (blk, blk), 1)
    tri = jnp.where(row >= col, 1.0, 0.0).astype(BF16)
    causal = col < row

    def block(kb, state, masked):
        start = pl.multiple_of(kb * blk, blk)
        heads = range(nh)
        zs = [_dot_nt(qs[j], kb_ref[j, pl.ds(start, blk), :]) * QK_SCALE + biases[j] for j in heads]
        drops = [_softplus_fast(z) for z in zs]
        if masked:
            drops = [jnp.where(causal, dr, 0.0) for dr in drops]
        splits = [_split_bf16(dr) for dr in drops]
        suffixes = [_dot(hi, tri) + _dot(lo, tri) for hi, lo in splits]
        ws = [jnp.exp(zs[j] - suffixes[j] - state[j][0]) for j in heads]
        if masked:
            ws = [jnp.where(causal, w, 0.0) for w in ws]
        pvs = [_dot(ws[j].astype(BF16), vb_ref[j, pl.ds(start, blk), :]) for j in heads]
        return tuple((state[j][0] + suffixes[j][:, 0:1], state[j][1] + pvs[j]) for j in heads)

    zero = (jnp.zeros((blk, 1), F32), jnp.zeros((blk, HEAD_DIM), F32))
    state = block(qi, (zero,) * nh, True)
    state = lax.fori_loop(0, qi, lambda it, st: block(qi - 1 - it, st, False), state)
    for j in range(nh):
        o_ref[:, j * HEAD_DIM:(j + 1) * HEAD_DIM] = state[j][1].astype(o_ref.dtype)


def _sb_prompt(q, k, v, sb_bias):
    bsz, heads, t, _ = k.shape
    blk = SB_BLK
    nh = SB_HEADS_PER_STEP
    assert t % blk == 0 and heads % nh == 0
    assert q.dtype == BF16 and k.dtype == BF16 and v.dtype == BF16
    nq = t // blk
    kv_spec = pl.BlockSpec((None, nh, t, HEAD_DIM), lambda b, h, i: (b, h, 0, 0))
    return pl.pallas_call(
        _sb_prompt_kernel,
        out_shape=jax.ShapeDtypeStruct((bsz * t, SB_W), BF16),
        grid=(bsz, heads // nh, nq),
        in_specs=[
            pl.BlockSpec(memory_space=pltpu.SMEM),
            pl.BlockSpec((blk, nh * HEAD_DIM), lambda b, h, i: (b * nq + i, h)),
            kv_spec, kv_spec,
        ],
        out_specs=pl.BlockSpec((blk, nh * HEAD_DIM), lambda b, h, i: (b * nq + i, h)),
        compiler_params=_cparams(("parallel", "parallel", "parallel"), 40),
        name="sb_prompt",
    )(sb_bias, q, k, v)


def _sb_sample_kernel(pt_ref, bias_ref, q_ref, *refs, n_pages):
    del pt_ref
    k_refs = refs[:n_pages]
    v_refs = refs[n_pages:2 * n_pages]
    o_ref = refs[2 * n_pages]
    heads, page = k_refs[0].shape[:2]
    rep = 8
    n = n_pages * rep

    row = lax.broadcasted_iota(jnp.int32, (page, page), 0)
    col = lax.broadcasted_iota(jnp.int32, (page, page), 1)
    tri = jnp.where(row >= col, 1.0, 0.0).astype(BF16)
    rn = lax.broadcasted_iota(jnp.int32, (n, n), 0)
    cn = lax.broadcasted_iota(jnp.int32, (n, n), 1)
    later = jnp.where(((rn % rep) == (cn % rep)) & (cn // rep > rn // rep), 1.0, 0.0).astype(BF16)
    q = q_ref[0]
    zs = []
    for h in range(heads):
        q8 = jnp.broadcast_to(q[h:h + 1, :], (rep, HEAD_DIM)).astype(BF16)
        z = jnp.concatenate([_dot_nt(q8, k_refs[p][h].astype(BF16)) for p in range(n_pages)], axis=0)
        zs.append(z * QK_SCALE + bias_ref[h])
    z = jnp.concatenate(zs, axis=0)
    hi, lo = _split_bf16(-_softplus_fast(z))
    suffix = _dot(hi, tri) + _dot(lo, tri)
    shi, slo = _split_bf16(suffix)
    carry = jnp.concatenate(
        [(_dot(later, shi[h * n:(h + 1) * n]) + _dot(later, slo[h * n:(h + 1) * n]))[:, 0:1] for h in range(heads)],
        axis=0)
    w = jnp.exp(z + suffix + carry).astype(BF16)
    outs = []
    for h in range(heads):
        acc = jnp.zeros((rep, HEAD_DIM), F32)
        for p in range(n_pages):
            r0 = h * n + p * rep
            acc = acc + _dot(w[r0:r0 + rep, :], v_refs[p][h].astype(BF16))
        outs.append(acc[0:1, :])
    o_ref[0] = jnp.concatenate(outs, axis=0).astype(o_ref.dtype)


def _sb_sample(q, sb_bias, cache_k, cache_v, page_table):
    bs, n_pages = page_table.shape
    heads, page = cache_k.shape[1:3]
    assert heads == SB_HEADS and cache_k.shape[3] == HEAD_DIM

    def page_spec(p):
        return pl.BlockSpec((None, heads, page, HEAD_DIM), lambda b, pt: (pt[b, p], 0, 0, 0))

    grid_spec = pltpu.PrefetchScalarGridSpec(
        num_scalar_prefetch=1,
        grid=(bs,),
        in_specs=[pl.BlockSpec(memory_space=pltpu.SMEM),
                  pl.BlockSpec((1, heads, HEAD_DIM), lambda b, pt: (b, 0, 0))]
                 + [page_spec(p) for p in range(n_pages)] * 2,
        out_specs=pl.BlockSpec((1, heads, HEAD_DIM), lambda b, pt: (b, 0, 0)),
    )
    out = pl.pallas_call(
        functools.partial(_sb_sample_kernel, n_pages=n_pages),
        out_shape=jax.ShapeDtypeStruct((bs, heads, HEAD_DIM), BF16),
        grid_spec=grid_spec,
        compiler_params=_cparams(("arbitrary",), 40),
        name="sb_sample",
    )(page_table, sb_bias, q.reshape(bs, heads, HEAD_DIM), *([cache_k] * n_pages), *([cache_v] * n_pages))
    return out.reshape(bs, SB_W)


def _lru_gates(xc, wa_ref, ba, wx_ref, bx, nsp_lambda):
    xb = xc.astype(BF16)
    r_parts, i_parts = [], []
    for h in range(LRU_HEADS):
        xh = xb[:, h * HEAD_DIM:(h + 1) * HEAD_DIM]
        r_parts.append(_dot(xh, wa_ref[h].astype(BF16)))
        i_parts.append(_dot(xh, wx_ref[h].astype(BF16)))
    r = _sigmoid(jnp.concatenate(r_parts, axis=1) + ba)
    i = _sigmoid(jnp.concatenate(i_parts, axis=1) + bx)
    log_a = -LRU_C * r * nsp_lambda
    a = jnp.exp(log_a)
    u = jnp.sqrt(1.0 - jnp.exp(2.0 * log_a)) * (i * xc)
    return a, u


def _gelu_tanh(x):
    return 0.5 * x * (1.0 + jnp.tanh(0.7978845608028654 * (x + 0.044715 * (x * x * x))))


def _lru_prompt_kernel(x_ref, gate_ref, prev_ref, h0_ref, cw_ref, cb_ref, wa_ref, ba_ref, wx_ref, bx_ref,
                       lam_ref, o_ref, hlast_ref, xp_ref, h_ref):
    ti = pl.program_id(1)
    tt = x_ref.shape[0]
    pad = 8

    @pl.when(ti == 0)
    def _():
        xp_ref[pad - (CONV_W - 1):pad, :] = prev_ref[0]
        h_ref[...] = h0_ref[0]

    x = x_ref[...]
    xp_ref[pad:pad + tt, :] = x
    cw = cw_ref[...]
    xc = cb_ref[...] + cw[CONV_W - 1:CONV_W, :] * x
    for j in range(CONV_W - 1):
        xc = xc + cw[j:j + 1, :] * xp_ref[pad - (CONV_W - 1) + j:pad - (CONV_W - 1) + j + tt, :]
    xp_ref[pad - (CONV_W - 1):pad, :] = x[tt - (CONV_W - 1):tt, :]

    nsp = _softplus(-lam_ref[...])
    a, u = _lru_gates(xc, wa_ref, ba_ref[...], wx_ref, bx_ref[...], nsp)
    rows = lax.broadcasted_iota(jnp.int32, (tt, LRU_W), 0)
    b = jnp.where(rows == 0, u + a * h_ref[...], u)
    s = 1
    while s < tt:
        keep = rows >= s
        b = jnp.where(keep, a * pltpu.roll(b, s, 0) + b, b)
        if 2 * s < tt:
            a = jnp.where(keep, a * pltpu.roll(a, s, 0), a)
        s *= 2
    h_ref[...] = b[tt - 1:tt, :]
    o_ref[...] = (b * _gelu_tanh(gate_ref[...])).astype(o_ref.dtype)

    @pl.when(ti == pl.num_programs(1) - 1)
    def _():
        hlast_ref[0] = b[tt - 1:tt, :]


def _lru_prompt(proj, conv_prev, h0, conv_w, conv_b, lru_wa, lru_ba, lru_wx, lru_bx, lru_lambda, bsz, t):
    tt = LRU_TT
    assert t % tt == 0
    nt = t // tt
    xl, gl = 0, 1
    full = lambda shape: pl.BlockSpec(shape, lambda b, i: (0,) * len(shape))
    o_lru, h_last = pl.pallas_call(
        _lru_prompt_kernel,
        out_shape=(jax.ShapeDtypeStruct((bsz * t, LRU_W), BF16),
                   jax.ShapeDtypeStruct((bsz, 1, LRU_W), F32)),
        grid=(bsz, nt),
        in_specs=[
            pl.BlockSpec((tt, LRU_W), lambda b, i: (b * nt + i, xl)),
            pl.BlockSpec((tt, LRU_W), lambda b, i: (b * nt + i, gl)),
            pl.BlockSpec((1, CONV_W - 1, LRU_W), lambda b, i: (b, 0, 0)),
            pl.BlockSpec((1, 1, LRU_W), lambda b, i: (b, 0, 0)),
            full((CONV_W, LRU_W)), full((1, LRU_W)),
            full((LRU_HEADS, HEAD_DIM, HEAD_DIM)), full((1, LRU_W)),
            full((LRU_HEADS, HEAD_DIM, HEAD_DIM)), full((1, LRU_W)),
            full((1, LRU_W)),
        ],
        out_specs=(pl.BlockSpec((tt, LRU_W), lambda b, i: (b * nt + i, 0)),
                   pl.BlockSpec((1, 1, LRU_W), lambda b, i: (b, 0, 0))),
        scratch_shapes=[pltpu.VMEM((8 + tt, LRU_W), F32), pltpu.VMEM((1, LRU_W), F32)],
        compiler_params=_cparams(("parallel", "arbitrary"), 32),
        name="lru_prompt",
    )(proj, proj, conv_prev, h0.reshape(bsz, 1, LRU_W), conv_w, conv_b.reshape(1, LRU_W),
      lru_wa, lru_ba.reshape(1, LRU_W), lru_wx, lru_bx.reshape(1, LRU_W), lru_lambda.reshape(1, LRU_W))
    return o_lru, h_last.reshape(bsz, LRU_W)


def _lru_step_kernel(x_ref, gate_ref, p0_ref, p1_ref, p2_ref, h0_ref, cw_ref, cb_ref, wa_ref, ba_ref, wx_ref,
                     bx_ref, lam_ref, o_ref, h_ref):
    cw = cw_ref[...]
    xc = (cb_ref[...] + cw[0:1, :] * p0_ref[...] + cw[1:2, :] * p1_ref[...] + cw[2:3, :] * p2_ref[...]
          + cw[3:4, :] * x_ref[...])
    nsp = _softplus(-lam_ref[...])
    a, u = _lru_gates(xc, wa_ref, ba_ref[...], wx_ref, bx_ref[...], nsp)
    h = u + a * h0_ref[...]
    h_ref[...] = h
    o_ref[...] = (h * _gelu_tanh(gate_ref[...])).astype(o_ref.dtype)


def _lru_step(x_lru, gate_lru, state_conv, h0, conv_w, conv_b, lru_wa, lru_ba, lru_wx, lru_bx, lru_lambda):
    bs = x_lru.shape[0]
    assert CONV_W == 4
    prevs = [state_conv[:, j, :] for j in range(CONV_W - 1)]
    o_lru, h_new = pl.pallas_call(
        _lru_step_kernel,
        out_shape=(jax.ShapeDtypeStruct((bs, LRU_W), BF16), jax.ShapeDtypeStruct((bs, LRU_W), F32)),
        name="lru_step",
    )(x_lru, gate_lru, *prevs, h0, conv_w, conv_b.reshape(1, LRU_W), lru_wa, lru_ba.reshape(1, LRU_W),
      lru_wx, lru_bx.reshape(1, LRU_W), lru_lambda.reshape(1, LRU_W))
    return o_lru, h_new


def _mem_prompt_kernel(q_ref, k_ref, v_ref, o_ref):
    heads = range(k_ref.shape[0])
    cols = [slice(h * HEAD_DIM, (h + 1) * HEAD_DIM) for h in heads]
    scores = [_dot_nt(q_ref[:, cols[h]].astype(BF16), k_ref[h].astype(BF16)) * QK_SCALE for h in heads]
    exps = [jnp.exp(s - jnp.max(s, axis=-1, keepdims=True)) for s in scores]
    probs = [e / jnp.sum(e, axis=-1, keepdims=True) for e in exps]
    for h in heads:
        o_ref[:, cols[h]] = _dot(probs[h].astype(BF16), v_ref[h].astype(BF16)).astype(o_ref.dtype)


def _mem_prompt(proj, qcol0, mem_k, mem_v, t):
    bsz, heads, n_mem, _ = mem_k.shape
    tq = _row_tile(t, ROW_TILE)
    assert heads == MEM_HEADS and qcol0 % MEM_W == 0
    nt = t // tq
    qc = qcol0 // MEM_W
    kv_spec = pl.BlockSpec((None, heads, n_mem, HEAD_DIM), lambda b, i: (b, 0, 0, 0))
    return pl.pallas_call(
        _mem_prompt_kernel,
        out_shape=jax.ShapeDtypeStruct((bsz * t, MEM_W), BF16),
        grid=(bsz, nt),
        in_specs=[pl.BlockSpec((tq, MEM_W), lambda b, i: (b * nt + i, qc)), kv_spec, kv_spec],
        out_specs=pl.BlockSpec((tq, MEM_W), lambda b, i: (b * nt + i, 0)),
        compiler_params=_cparams(("parallel", "parallel"), 32),
        name="mem_prompt",
    )(proj, mem_k, mem_v)


def _mem_sample_kernel(q_ref, k_ref, v_ref, o_ref):
    group = q_ref.shape[0]
    for g in range(group):
        q = q_ref[g]
        s = jnp.sum(k_ref[g] * q[None], axis=-1, keepdims=True) * QK_SCALE
        m = jnp.max(s, axis=0, keepdims=True)
        e = jnp.exp(s - m)
        p = e / jnp.sum(e, axis=0, keepdims=True)
        o_ref[g] = jnp.sum(p * v_ref[g], axis=0).astype(o_ref.dtype)


def _mem_sample(q, cache_k, cache_v):
    bs, n_mem, heads, _ = cache_k.shape
    assert heads == MEM_HEADS and cache_k.shape[3] == HEAD_DIM
    group = 8
    assert bs % group == 0
    kv_spec = pl.BlockSpec((group, n_mem, heads, HEAD_DIM), lambda i: (i, 0, 0, 0))
    out = pl.pallas_call(
        _mem_sample_kernel,
        out_shape=jax.ShapeDtypeStruct((bs, heads, HEAD_DIM), BF16),
        grid=(bs // group,),
        in_specs=[pl.BlockSpec((group, heads, HEAD_DIM), lambda i: (i, 0, 0)), kv_spec, kv_spec],
        out_specs=pl.BlockSpec((group, heads, HEAD_DIM), lambda i: (i, 0, 0)),
        compiler_params=_cparams(("parallel",), 32),
        name="mem_sample",
    )(q.reshape(bs, heads, HEAD_DIM), cache_k, cache_v)
    return out.reshape(bs, MEM_W)


def _merge_kernel(oa_ref, ol_ref, om_ref, pa_ref, pl_ref, pm_ref, ga_ref, gl_ref, gm_ref, o_ref):
    tm = o_ref.shape[0]
    sub = min(tm, MERGE_SUB)
    pa = pa_ref[...].astype(BF16)
    pl_ = pl_ref[...].astype(BF16)
    pm = pm_ref[...].astype(BF16)
    for r0 in range(0, tm, sub):
        rows = slice(r0, r0 + sub)
        ya = _dot(oa_ref[rows, :], pa)
        yl = _dot(ol_ref[rows, :], pl_)
        ym = _dot(om_ref[rows, :], pm)
        merged = (_sigmoid(ga_ref[rows, :]) * ya + _sigmoid(gl_ref[rows, :]) * yl
                  + _sigmoid(gm_ref[rows, :]) * ym)
        o_ref[rows, :] = merged.astype(o_ref.dtype)


def _merge(o_sb, o_lru, o_mem, p_attn, p_lru, p_mem, gates, tm):
    r = o_sb.shape[0]
    d = p_attn.shape[1]
    tn = 512
    assert r % tm == 0 and d % tn == 0
    nj = d // tn
    return pl.pallas_call(
        _merge_kernel,
        out_shape=jax.ShapeDtypeStruct((r, d), BF16),
        grid=(r // tm, nj),
        in_specs=[
            pl.BlockSpec((tm, SB_W), lambda i, j: (i, 0)),
            pl.BlockSpec((tm, LRU_W), lambda i, j: (i, 0)),
            pl.BlockSpec((tm, MEM_W), lambda i, j: (i, 0)),
            pl.BlockSpec((SB_W, tn), lambda i, j: (0, j)),
            pl.BlockSpec((LRU_W, tn), lambda i, j: (0, j)),
            pl.BlockSpec((MEM_W, tn), lambda i, j: (0, j)),
            pl.BlockSpec((tm, tn), lambda i, j: (i, j)),
            pl.BlockSpec((tm, tn), lambda i, j: (i, nj + j)),
            pl.BlockSpec((tm, tn), lambda i, j: (i, 2 * nj + j)),
        ],
        out_specs=pl.BlockSpec((tm, tn), lambda i, j: (i, j)),
        compiler_params=_cparams(("parallel", "parallel"), 44),
        name="merge",
    )(o_sb, o_lru, o_mem, p_attn, p_lru, p_mem, gates, gates, gates)


def _post_kernel(x_ref, m_ref, wo_ref, g_ref, wr_ref, br_ref, cnt0_ref,
                 hres_ref, hnp_ref, idx_ref, gate_ref, rank_ref, cnt_ref):
    i = pl.program_id(0)
    tm, d = x_ref.shape
    sub = min(tm, POST_SUB)
    starts = range(0, tm, sub)

    @pl.when(i == 0)
    def _():
        cnt_ref[...] = cnt0_ref[...]

    hres_all = [x_ref[r0:r0 + sub, :] + _dot(m_ref[r0:r0 + sub, :], wo_ref[...]) for r0 in starts]
    eid = lax.broadcasted_iota(jnp.int32, (wr_ref.shape[0], sub), 0)
    ne = eid.shape[0]
    r_ = lax.broadcasted_iota(jnp.int32, (sub, sub), 0)
    c_ = lax.broadcasted_iota(jnp.int32, (sub, sub), 1)
    before = jnp.where(r_ < c_, 1.0, 0.0).astype(BF16)
    cnt = cnt_ref[...]
    for r0, hres in zip(starts, hres_all):
        hres_ref[r0:r0 + sub, :] = hres
        ms = jnp.mean(hres * hres, axis=-1, keepdims=True)
        hn = hres * lax.rsqrt(ms + EPS) * g_ref[...]
        hb = hn.astype(BF16)
        wa = lax.bitcast_convert_type(hb[:, :d // 2].astype(F32), jnp.uint32)
        wb = lax.bitcast_convert_type(hb[:, d // 2:].astype(F32), jnp.uint32)
        hnp_ref[r0:r0 + sub, :] = wa | (wb >> 16)

        work = _dot_nt(wr_ref[...], hb) + br_ref[...]
        vals, idxs, onehots = [], [], []
        for _ in range(TOP_K):
            mx = jnp.max(work, axis=0, keepdims=True)
            sel = jnp.min(jnp.where(work == mx, eid, ne), axis=0, keepdims=True)
            oh = eid == sel
            vals.append(mx)
            idxs.append(sel)
            onehots.append(oh)
            work = jnp.where(oh, -jnp.inf, work)
        es = [jnp.exp(v - vals[0]) for v in vals]
        den = functools.reduce(jnp.add, es)
        gate_ref[:, r0:r0 + sub] = jnp.concatenate([e / den for e in es], axis=0)
        idx_ref[:, r0:r0 + sub] = jnp.concatenate(idxs, axis=0)

        chosen_f = jnp.where(functools.reduce(jnp.logical_or, onehots), 1.0, 0.0)
        prior = _dot(chosen_f.astype(BF16), before) + cnt
        rank_ref[:, r0:r0 + sub] = jnp.concatenate(
            [jnp.sum(jnp.where(oh, prior, 0.0), axis=0, keepdims=True) for oh in onehots], axis=0).astype(jnp.int32)
        cnt = cnt + jnp.sum(chosen_f, axis=1, keepdims=True)
    cnt_ref[...] = cnt


def _post(x, merged, wo_b, g, wr_t, b_router, cnt0, tm):
    r, d = x.shape
    ne = wr_t.shape[0]
    assert r % tm == 0
    full = lambda shape: pl.BlockSpec(shape, lambda i: (0,) * len(shape))
    return pl.pallas_call(
        _post_kernel,
        out_shape=(jax.ShapeDtypeStruct((r, d), F32),
                   jax.ShapeDtypeStruct((r, d // 2), jnp.uint32),
                   jax.ShapeDtypeStruct((TOP_K, r), jnp.int32),
                   jax.ShapeDtypeStruct((TOP_K, r), F32),
                   jax.ShapeDtypeStruct((TOP_K, r), jnp.int32),
                   jax.ShapeDtypeStruct((ne, 1), F32)),
        grid=(r // tm,),
        in_specs=[pl.BlockSpec((tm, d), lambda i: (i, 0)),
                  pl.BlockSpec((tm, d), lambda i: (i, 0)),
                  full((d, d)), full((1, d)), full((ne, d)), full((ne, 1)), full((ne, 1))],
        out_specs=(pl.BlockSpec((tm, d), lambda i: (i, 0)),
                   pl.BlockSpec((tm, d // 2), lambda i: (i, 0)),
                   pl.BlockSpec((TOP_K, tm), lambda i: (0, i)),
                   pl.BlockSpec((TOP_K, tm), lambda i: (0, i)),
                   pl.BlockSpec((TOP_K, tm), lambda i: (0, i)),
                   full((ne, 1))),
        compiler_params=_cparams(("arbitrary",), 56),
        name="post",
    )(x, merged, wo_b, g.reshape(1, d), wr_t, b_router.reshape(ne, 1), cnt0)


def _dispatch_kernel(tail_ref, dest_ref, hn_a_ref, hn_b_ref, xs_ref, zero_ref, sem, zsem, *, tiles_a):
    i = pl.program_id(0)
    tm = dest_ref.shape[2] // TOP_K

    @pl.when(i == 0)
    def _():
        zero_ref[...] = jnp.zeros_like(zero_ref)

        def tail_copy(e):
            start = pl.multiple_of(tail_ref[e], MOE_SUB)
            return pltpu.make_async_copy(zero_ref, xs_ref.at[pl.ds(start, MOE_SUB)], zsem)

        def zstart(e, c):
            pl.when(tail_ref[e] >= 0)(lambda: tail_copy(e).start())
            return c

        def zwait(e, c):
            pl.when(tail_ref[e] >= 0)(lambda: tail_copy(e).wait())
            return c

        lax.fori_loop(0, tail_ref.shape[0], zstart, 0)
        lax.fori_loop(0, tail_ref.shape[0], zwait, 0)

    def row_copy(src_ref, row, t, k):
        return pltpu.make_async_copy(src_ref.at[pl.ds(row, 1)],
                                     xs_ref.at[pl.ds(dest_ref[0, 0, k * tm + t], 1)], sem)

    def start_tile(src_ref, first_row):
        def body(t, c):
            for k in range(TOP_K):
                row_copy(src_ref, first_row + t, t, k).start()
            return c
        lax.fori_loop(0, tm, body, 0)

    def wait_tile():
        def body(t, c):
            for k in range(TOP_K):
                row_copy(hn_a_ref, 0, t, k).wait()
            return c
        lax.fori_loop(0, tm, body, 0)

    pl.when(i < tiles_a)(lambda: start_tile(hn_a_ref, i * tm))
    pl.when(i >= tiles_a)(lambda: start_tile(hn_b_ref, (i - tiles_a) * tm))
    pl.when(i > 0)(wait_tile)
    pl.when(i == pl.num_programs(0) - 1)(wait_tile)


def _dispatch(hnp_a, hnp_b, dest, tail, n_rows, tm):
    (ra, w), rb_ = hnp_a.shape, hnp_b.shape[0]
    assert ra % tm == 0 and rb_ % tm == 0 and hnp_b.shape[1] == w and dest.shape[1] == ra + rb_
    nt = (ra + rb_) // tm
    dest_t = dest.reshape(TOP_K, nt, tm).transpose(1, 0, 2).reshape(nt, 1, TOP_K * tm)
    grid_spec = pltpu.PrefetchScalarGridSpec(
        num_scalar_prefetch=1,
        grid=(nt,),
        in_specs=[pl.BlockSpec((1, 1, TOP_K * tm), lambda i, tl: (i, 0, 0), memory_space=pltpu.SMEM),
                  pl.BlockSpec(memory_space=pl.ANY), pl.BlockSpec(memory_space=pl.ANY)],
        out_specs=pl.BlockSpec(memory_space=pl.ANY),
        scratch_shapes=[pltpu.VMEM((MOE_SUB, w), hnp_a.dtype), pltpu.SemaphoreType.DMA, pltpu.SemaphoreType.DMA],
    )
    return pl.pallas_call(
        functools.partial(_dispatch_kernel, tiles_a=ra // tm),
        out_shape=jax.ShapeDtypeStruct((n_rows, w), hnp_a.dtype),
        grid_spec=grid_spec,
        compiler_params=_cparams(("arbitrary",), 16),
        name="dispatch",
    )(tail, dest_t, hnp_a, hnp_b)


def _moe_kernel(ce_ref, cb_ref, ns_ref, xs_ref, wg_ref, wu_ref, bg_ref, bu_ref, wd_ref, bd_ref, o_ref,
                xa_ref, xb_ref):
    del ce_ref, cb_ref
    c = pl.program_id(0)
    s = pl.program_id(1)
    nsub = ns_ref[c]
    half = xa_ref.shape[1]
    d = o_ref.shape[1]

    def for_row_blocks(fn):
        done = 0
        for size in MOE_BLOCK_SUBS:
            count = (nsub - done) // size
            rows = size * MOE_SUB

            def body(i, carry, done=done, rows=rows):
                fn(pl.ds(pl.multiple_of(done * MOE_SUB + i * rows, MOE_SUB), rows))
                return carry

            lax.fori_loop(0, count, body, 0)
            done = done + count * size

    @pl.when(s == 0)
    def _():
        def unpack(rows):
            word = xs_ref[rows, :]
            xa_ref[rows, :] = lax.bitcast_convert_type(word & jnp.uint32(0xFFFF0000), F32).astype(BF16)
            xb_ref[rows, :] = lax.bitcast_convert_type(word << 16, F32).astype(BF16)
            o_ref[rows, :] = jnp.broadcast_to(bd_ref[...], (rows.size, d))
        for_row_blocks(unpack)

    bg = bg_ref[...]
    bu = bu_ref[...]

    def step(rows):
        xa = xa_ref[rows, :]
        xb = xb_ref[rows, :]
        gate = (_dot(xa, wg_ref[:half, :].astype(BF16)) + _dot(xb, wg_ref[half:, :].astype(BF16)) + bg)
        upv = (_dot(xa, wu_ref[:half, :].astype(BF16)) + _dot(xb, wu_ref[half:, :].astype(BF16)) + bu)
        gate = jnp.minimum(gate, SWIGLU_LIMIT)
        upv = jnp.clip(upv, -SWIGLU_LIMIT, SWIGLU_LIMIT)
        act = (gate * _sigmoid(SWIGLU_ALPHA * gate) * (upv + 1.0)).astype(BF16)
        for n0 in range(0, d, MOE_TN):
            o_ref[rows, n0:n0 + MOE_TN] += _dot(act, wd_ref[:, n0:n0 + MOE_TN].astype(BF16))
    for_row_blocks(step)


def _moe(xs, n_used, chunk_expert, chunk_block, chunk_nsub, w_up, b_up, w_down, b_down):
    ne, d, dff2 = w_up.shape
    dff = dff2 // 2
    rb = MOE_SUB * MOE_CHUNK_SUBS
    n_chunks = xs.shape[0] // rb
    tf = MOE_TF
    nf = dff // tf
    assert dff % tf == 0 and d % MOE_TN == 0

    grid_spec = pltpu.PrefetchScalarGridSpec(
        num_scalar_prefetch=3,
        grid=(n_used, nf),
        in_specs=[
            pl.BlockSpec((rb, d // 2), lambda c, s, ce, cb, ns: (cb[c], 0)),
            pl.BlockSpec((None, d, tf), lambda c, s, ce, cb, ns: (ce[c], 0, s)),
            pl.BlockSpec((None, d, tf), lambda c, s, ce, cb, ns: (ce[c], 0, nf + s)),
            pl.BlockSpec((None, 1, tf), lambda c, s, ce, cb, ns: (ce[c], 0, s)),
            pl.BlockSpec((None, 1, tf), lambda c, s, ce, cb, ns: (ce[c], 0, nf + s)),
            pl.BlockSpec((None, tf, d), lambda c, s, ce, cb, ns: (ce[c], s, 0)),
            pl.BlockSpec((None, 1, d), lambda c, s, ce, cb, ns: (ce[c], 0, 0)),
        ],
        out_specs=pl.BlockSpec((rb, d), lambda c, s, ce, cb, ns: (cb[c], 0)),
        scratch_shapes=[pltpu.VMEM((rb, d // 2), BF16), pltpu.VMEM((rb, d // 2), BF16)],
    )
    return pl.pallas_call(
        _moe_kernel,
        out_shape=jax.ShapeDtypeStruct((n_chunks * rb, d), F32),
        grid_spec=grid_spec,
        compiler_params=_cparams(("arbitrary", "arbitrary"), 56),
        name="moe",
    )(chunk_expert, chunk_block, chunk_nsub, xs, w_up, w_up, b_up.reshape(ne, 1, dff2), b_up.reshape(ne, 1, dff2),
      w_down, b_down.reshape(ne, 1, d))


def _combine_kernel(dest_ref, dest_next_ref, hres_ref, gate_ref, yb_ref, o_ref, buf_ref, sem):
    i = pl.program_id(0)
    tm = hres_ref.shape[0]
    slot = i % 2

    def row_copy(dref, s, t, k):
        return pltpu.make_async_copy(yb_ref.at[pl.ds(dref[0, 0, k * tm + t], 1)],
                                     buf_ref.at[s, k, pl.ds(t, 1)], sem.at[s])

    def issue(dref, s):
        def body(t, c):
            for k in range(TOP_K):
                row_copy(dref, s, t, k).start()
            return c
        lax.fori_loop(0, tm, body, 0)

    def drain(dref, s):
        def body(t, c):
            for k in range(TOP_K):
                row_copy(dref, s, t, k).wait()
            return c
        lax.fori_loop(0, tm, body, 0)

    pl.when(i == 0)(lambda: issue(dest_ref, slot))
    pl.when(i + 1 < pl.num_programs(0))(lambda: issue(dest_next_ref, 1 - slot))
    drain(dest_ref, slot)
    g = gate_ref[...]
    y = hres_ref[...]
    for k in range(TOP_K):
        y = y + g[:, k:k + 1] * buf_ref[slot, k]
    o_ref[...] = y


def _combine(hres, gates, dest, yb, tm):
    r, d = hres.shape
    assert r % tm == 0
    nt = r // tm
    dest_t = dest.reshape(TOP_K, nt, tm).transpose(1, 0, 2).reshape(nt, 1, TOP_K * tm)
    return pl.pallas_call(
        _combine_kernel,
        out_shape=jax.ShapeDtypeStruct((r, d), F32),
        grid=(nt,),
        in_specs=[pl.BlockSpec((1, 1, TOP_K * tm), lambda i: (i, 0, 0), memory_space=pltpu.SMEM),
                  pl.BlockSpec((1, 1, TOP_K * tm), lambda i: (jnp.minimum(i + 1, nt - 1), 0, 0),
                               memory_space=pltpu.SMEM),
                  pl.BlockSpec((tm, d), lambda i: (i, 0)),
                  pl.BlockSpec((tm, TOP_K), lambda i: (i, 0)),
                  pl.BlockSpec(memory_space=pl.ANY)],
        out_specs=pl.BlockSpec((tm, d), lambda i: (i, 0)),
        scratch_shapes=[pltpu.VMEM((2, TOP_K, tm, d), F32), pltpu.SemaphoreType.DMA((2,))],
        compiler_params=_cparams(("arbitrary",), 40),
        name="combine",
    )(dest_t, dest_t, hres, gates.T, yb)


def _row_tile(r, cap):
    tm = min(r, cap)
    assert r % tm == 0
    return tm


def kernel(x_prompt, x_sample, mem_prompt, cache_sb_k, cache_sb_v, page_table, cache_mem_k, cache_mem_v, state_conv, state_lru, norm_mix_g, norm_mem_g, w_in, q_sb_g, k_sb_g, sb_bias, q_mem_g, k_mem_g, w_mem_kv, conv_w, conv_b, lru_wa, lru_ba, lru_wx, lru_bx, lru_lambda, p_attn, p_lru, p_mem, w_o, norm_ffn_g, w_router, b_router, w_up, b_up, w_down, b_down):
    bp, t, d = x_prompt.shape
    bs, ts, _ = x_sample.shape
    assert ts == 1, "sample group decodes one token per sequence"
    n_mem = mem_prompt.shape[1]
    q_gain = jnp.tile(q_sb_g, SB_HEADS)
    k_gain = jnp.tile(k_sb_g, SB_HEADS)
    qm_gain = jnp.tile(q_mem_g, MEM_HEADS)
    heads_major = lambda a: jnp.transpose(a, (0, 2, 1, 3))

    memn = _rmsnorm(mem_prompt.reshape(bp * n_mem, d), norm_mem_g, _row_tile(bp * n_mem, 256))
    mem_tm = _row_tile(n_mem, 256)
    mem_k_hm = _proj(memn, w_mem_kv, jnp.tile(k_mem_g, MEM_HEADS), col0=0, ncols=MEM_W, norm_cols=(0, MEM_W),
                     tm=mem_tm, tn=MEM_W, heads_out=(bp, n_mem))
    mem_v_hm = _proj(memn, w_mem_kv, None, col0=MEM_W, ncols=MEM_W, tm=mem_tm, tn=MEM_W, heads_out=(bp, n_mem))

    rest_w = COL_GATE - COL_XL
    qm_cols = (COL_QM - COL_XL, rest_w)
    gap = -COL_XL % WIDE_TN
    w_b = _cast_bf16(w_in, gap_at=COL_XL, gap=gap)
    q_at, k_at, v_at, rest_at, gates_at = COL_Q, COL_K, COL_V, COL_XL + gap, COL_GATE + gap
    rest_gain = jnp.concatenate([jnp.ones((2 * LRU_W,), F32), qm_gain])

    xp2 = x_prompt.reshape(bp * t, d)
    tm_p = _row_tile(t, ROW_TILE)
    xn_p = _rmsnorm(xp2, norm_mix_g, _row_tile(bp * t, 512))
    assert (q_at, k_at, v_at) == (0, SB_W, 2 * SB_W)
    q_pb, k_p, k_pb, v_p, v_pb = _proj_qkv(xn_p, w_b, q_gain, k_gain, bp, t, tm_p)
    rest_p = _proj(xn_p, w_b, rest_gain, col0=rest_at, ncols=rest_w, norm_cols=qm_cols, tm=tm_p, tn=WIDE_TN)
    gates_p = _proj(xn_p, w_b, None, col0=gates_at, ncols=N_BRANCH * d, tm=tm_p, tn=WIDE_TN)
    o_sb_p = _sb_prompt(q_pb, k_pb, v_pb, sb_bias)
    conv0 = jnp.zeros((bp, CONV_W - 1, LRU_W), x_prompt.dtype)
    lru0 = jnp.zeros((bp, LRU_W), state_lru.dtype)
    o_lru_p, lru_prompt = _lru_prompt(rest_p, conv0, lru0, conv_w, conv_b, lru_wa, lru_ba, lru_wx, lru_bx,
                                      lru_lambda, bp, t)
    o_mem_p = _mem_prompt(rest_p, COL_QM - COL_XL, mem_k_hm, mem_v_hm, t)
    xl_tail = rest_p.reshape(bp, t, -1)[:, t - (CONV_W - 1):, :LRU_W]
    conv_prompt = jnp.concatenate([conv0, xl_tail], axis=1)[:, -(CONV_W - 1):]

    xs2 = x_sample.reshape(bs, d)
    xn_s = _rmsnorm(xs2, norm_mix_g, bs)
    qkv_gain = jnp.concatenate([q_gain, k_gain, jnp.ones((SB_W,), F32)])
    qkv_s = _proj(xn_s, w_b, qkv_gain, col0=q_at, ncols=3 * SB_W, norm_cols=(0, 2 * SB_W), tm=bs, tn=SB_W)
    q_s, k_s, v_s = qkv_s[:, :SB_W], qkv_s[:, SB_W:2 * SB_W], qkv_s[:, 2 * SB_W:]
    rest_s = _proj(xn_s, w_b, rest_gain, col0=rest_at, ncols=rest_w, norm_cols=qm_cols, tm=bs, tn=WIDE_TN)
    gates_s = _proj(xn_s, w_b, None, col0=gates_at, ncols=N_BRANCH * d, tm=bs, tn=WIDE_TN)
    o_sb_s = _sb_sample(q_s, sb_bias, heads_major(cache_sb_k), heads_major(cache_sb_v), page_table)
    xl_s = rest_s[:, :LRU_W]
    o_lru_s, lru_sample = _lru_step(xl_s, rest_s[:, LRU_W:2 * LRU_W], state_conv, state_lru, conv_w, conv_b,
                                    lru_wa, lru_ba, lru_wx, lru_bx, lru_lambda)
    conv_sample = jnp.concatenate([state_conv, xl_s[:, None, :]], axis=1)[:, -(CONV_W - 1):]
    o_mem_s = _mem_sample(rest_s[:, 2 * LRU_W:], cache_mem_k, cache_mem_v)

    wo_b = _cast_bf16(w_o)
    wr_t = w_router.T.astype(BF16)
    merged_p = _merge(o_sb_p, o_lru_p, o_mem_p, p_attn, p_lru, p_mem, gates_p, _row_tile(bp * t, ROW_TILE))
    merged_s = _merge(o_sb_s, o_lru_s, o_mem_s, p_attn, p_lru, p_mem, gates_s, _row_tile(bs, ROW_TILE))
    cnt0 = jnp.zeros((N_EXPERTS, 1), F32)
    hres_p, hnp_p, idx_p, gate_p, rank_p, cnt_p = _post(xp2, merged_p, wo_b, norm_ffn_g, wr_t, b_router,
                                                        cnt0, _row_tile(bp * t, POST_TM))
    hres_s, hnp_s, idx_s, gate_s, rank_s, cnt_all = _post(xs2, merged_s, wo_b, norm_ffn_g, wr_t, b_router,
                                                          cnt_p, _row_tile(bs, POST_TM))

    rb = MOE_SUB * MOE_CHUNK_SUBS
    n_assign = (bp * t + bs) * TOP_K
    n_chunks = -(-n_assign // rb) + N_EXPERTS
    counts = cnt_all[:, 0].astype(jnp.int32)
    chunks_e = (counts + rb - 1) // rb
    chunk_end = jnp.cumsum(chunks_e)
    chunk_start = chunk_end - chunks_e
    used = chunk_end[-1]
    cidx = jnp.arange(n_chunks, dtype=jnp.int32)
    last = jnp.minimum(cidx, used - 1)
    chunk_expert = jnp.clip(jnp.searchsorted(chunk_end, last, side='right'), 0, N_EXPERTS - 1).astype(jnp.int32)
    rows_left = counts[chunk_expert] - (last - chunk_start[chunk_expert]) * rb
    chunk_nsub = jnp.where(cidx < used, (jnp.clip(rows_left, 0, rb) + MOE_SUB - 1) // MOE_SUB, 0).astype(jnp.int32)
    chunk_block = last.astype(jnp.int32)
    row_start = chunk_start * rb
    experts = jnp.arange(N_EXPERTS, dtype=jnp.int32)

    def slots(idx, rank):
        return jnp.sum(jnp.where(idx[..., None] == experts, row_start, 0), axis=-1) + rank

    dest_p = slots(idx_p, rank_p)
    dest_s = slots(idx_s, rank_s)
    tail = jnp.where(counts % MOE_SUB != 0, row_start + counts // MOE_SUB * MOE_SUB, -1).astype(jnp.int32)

    dest = jnp.concatenate([dest_p, dest_s], axis=1)
    xs = _dispatch(hnp_p, hnp_s, dest, tail, n_chunks * rb, math.gcd(bp * t, bs, DISPATCH_TM))
    yb = _moe(xs, used.astype(jnp.int32), chunk_expert, chunk_block, chunk_nsub, w_up, b_up, w_down, b_down)
    y_prompt = _combine(hres_p, gate_p, dest_p, yb, _row_tile(bp * t, COMBINE_TM)).reshape(bp, t, d)
    y_sample = _combine(hres_s, gate_s, dest_s, yb, _row_tile(bs, COMBINE_TM)).reshape(bs, 1, d)

    sb_k_sample = k_s.reshape(bs, 1, SB_HEADS, HEAD_DIM)
    sb_v_sample = v_s.reshape(bs, 1, SB_HEADS, HEAD_DIM)
    return (y_prompt, y_sample, heads_major(k_p), heads_major(v_p), heads_major(mem_k_hm), heads_major(mem_v_hm),
            conv_prompt, lru_prompt, sb_k_sample, sb_v_sample, conv_sample, lru_sample)
```
